```python
import math
import jax, jax.numpy as jnp
from jax import lax
import numpy as np

D_MODEL = 2048
BATCH = 32
SEQ = 256
DEPTH = 4
DEC_BATCH = 2
DEC_SEQ = 4096
PAST_LEN = 256

GRID_W = 64
EPS = 1e-6

NA_HEADS = 8
NA_HEAD_DIM = 128
NA_WIDTH = NA_HEADS * NA_HEAD_DIM
NA_WIN_ROWS = 8
NA_WIN_COLS = 16
ATT_Q_BLOCK = 128

SSM_HEADS = 16
SSM_HEAD_DIM = 64
SSM_INNER = SSM_HEADS * SSM_HEAD_DIM
SSM_GROUPS = 2
SSM_STATE = 128
SSM_CONV = 4
SSM_CHUNK = 128
SSM_CONV_DIM = SSM_INNER + 2 * SSM_GROUPS * SSM_STATE

MLP_GROUPS = 8
MLP_WIDTH = 1024
MLP_CHUNK = 128

PEER_HEADS = 8
PEER_KEYS = 128
PEER_EXPERTS = PEER_KEYS * PEER_KEYS
PEER_QDIM = 256
PEER_TOPK = 16
PEER_BLOCK = 128

N_BRANCH = 3
IN_SPLITS = (NA_WIDTH, NA_WIDTH, NA_WIDTH, SSM_INNER, SSM_CONV_DIM, 2 * SSM_HEADS, MLP_WIDTH, MLP_WIDTH, N_BRANCH * D_MODEL)
IN_WIDTH = 3 * NA_WIDTH + SSM_INNER + SSM_CONV_DIM + 2 * SSM_HEADS + 2 * MLP_WIDTH + N_BRANCH * D_MODEL

kernel_name = 'hybrid_na_ssd_gmlp_peer_diffusion_step'


def rms_norm(x, g):
    xf = x.astype(jnp.float32)
    y = xf * lax.rsqrt(jnp.mean(xf * xf, axis=-1, keepdims=True) + EPS)
    return (y * g.astype(jnp.float32)).astype(x.dtype)


def _modulation(cvec, w_mod, b_mod):
    m = jax.nn.silu(cvec) @ w_mod + b_mod
    return jnp.split(m, 6, axis=-1)


def _split_cols(p):
    outs = []
    start = 0
    for w in IN_SPLITS:
        outs.append(p[..., start:start + w])
        start += w
    return outs


def context_attention(q, k, v):
    b, s, h, hd = q.shape
    nb = s // ATT_Q_BLOCK
    qb = q.reshape(b, nb, ATT_Q_BLOCK, h, hd).transpose(1, 0, 2, 3, 4)
    scale = hd ** -0.5

    def blk(qi):
        sc = jnp.einsum('bqhd,bkhd->bhqk', qi, k).astype(jnp.float32) * scale
        p = jax.nn.softmax(sc, axis=-1).astype(v.dtype)
        return jnp.einsum('bhqk,bkhd->bqhd', p, v)

    out = lax.map(blk, qb)
    return out.transpose(1, 0, 2, 3, 4).reshape(b, s, h * hd)


def neighbourhood_attention(q, k, v, k_ctx, v_ctx, rel_bias):
    b, L, h, hd = q.shape
    rows = L // GRID_W
    wr = min(NA_WIN_ROWS, rows)
    scale = hd ** -0.5
    qg = q.reshape(b, rows, GRID_W, h, hd)
    kg = k.reshape(b, rows, GRID_W, h, hd)
    vg = v.reshape(b, rows, GRID_W, h, hd)
    col = np.arange(GRID_W)
    col_start = np.clip(col - NA_WIN_COLS // 2, 0, GRID_W - NA_WIN_COLS)
    col_idx = col_start[:, None] + np.arange(NA_WIN_COLS)[None, :]
    dc = col_idx - col[:, None]

    def row_block(r):
        rs = jnp.clip(r - wr // 2, 0, rows - wr)
        q_r = lax.dynamic_index_in_dim(qg, r, axis=1, keepdims=False)
        k_band = lax.dynamic_slice_in_dim(kg, rs, wr, axis=1)
        v_band = lax.dynamic_slice_in_dim(vg, rs, wr, axis=1)
        k_win = k_band[:, :, col_idx]
        v_win = v_band[:, :, col_idx]
        dr = rs + jnp.arange(wr) - r
        rb = rel_bias[:, (dr + NA_WIN_ROWS - 1)[None, :, None], (dc + NA_WIN_COLS - 1)[:, None, :]]
        s_win = jnp.einsum('bqhd,biqjhd->bhqij', q_r, k_win).astype(jnp.float32) * scale
        s_win = (s_win + rb[None].astype(jnp.float32)).reshape(b, h, GRID_W, wr * NA_WIN_COLS)
        s_ctx = jnp.einsum('bqhd,bkhd->bhqk', q_r, k_ctx).astype(jnp.float32) * scale
        p = jax.nn.softmax(jnp.concatenate([s_win, s_ctx], axis=-1), axis=-1).astype(v.dtype)
        p_win = p[..., :wr * NA_WIN_COLS].reshape(b, h, GRID_W, wr, NA_WIN_COLS)
        p_ctx = p[..., wr * NA_WIN_COLS:]
        return jnp.einsum('bhqij,biqjhd->bqhd', p_win, v_win) + jnp.einsum('bhqk,bkhd->bqhd', p_ctx, v_ctx)

    out = lax.map(row_block, jnp.arange(rows))
    return out.transpose(1, 0, 2, 3, 4).reshape(b, L, h * hd)


def centred_dwconv(x, w, bias):
    ch = x.shape[-1]
    kw = w.shape[0]
    lo = kw // 2
    hi = kw - 1 - lo
    y = lax.conv_general_dilated(x, w.astype(x.dtype)[:, None, :], window_strides=(1,), padding=[(lo, hi)],
                                 dimension_numbers=('NWC', 'WIO', 'NWC'), feature_group_count=ch)
    return y + bias.astype(x.dtype)


def ssd_scan(xh, dt, a_neg, bm, cm, init_state):
    b, l, h, p = xh.shape
    n = bm.shape[-1]
    q = SSM_CHUNK
    c = l // q
    a = (dt * a_neg).reshape(b, c, q, h)
    xd = (xh * dt[..., None]).reshape(b, c, q, h, p)
    bc = bm.reshape(b, c, q, h, n)
    cc = cm.reshape(b, c, q, h, n)
    a_cum = jnp.cumsum(a, axis=2)
    lower = jnp.tril(jnp.ones((q, q), dtype=bool))[None, None, :, :, None]
    seg = a_cum[:, :, :, None, :] - a_cum[:, :, None, :, :]
    decay = jnp.exp(jnp.where(lower, seg, -jnp.inf))
    scores = jnp.einsum('bclhn,bcshn->bclsh', cc, bc) * decay
    y_diag = jnp.einsum('bclsh,bcshp->bclhp', scores, xd)
    decay_to_end = jnp.exp(a_cum[:, :, -1:, :] - a_cum)
    chunk_states = jnp.einsum('bclhn,bclh,bclhp->bchpn', bc, decay_to_end, xd)
    chunk_decay = jnp.exp(a_cum[:, :, -1, :])

    def step(s, inp):
        st, dec = inp
        return s * dec[:, :, None, None] + st, s

    final, prev = lax.scan(step, init_state, (chunk_states.transpose(1, 0, 2, 3, 4), chunk_decay.transpose(1, 0, 2)))
    prev = prev.transpose(1, 0, 2, 3, 4)
    y_off = jnp.einsum('bclhn,bchpn,bclh->bclhp', cc, prev, jnp.exp(a_cum))
    return (y_diag + y_off).reshape(b, l, h, p), final


def ssm_mixer(z, xbc, dt_raw, init, lp):
    b, l, _ = z.shape
    f32 = jnp.float32
    xbc = jax.nn.silu(centred_dwconv(xbc, lp['conv_w'], lp['conv_b'])).astype(f32)
    xs = xbc[..., :SSM_INNER].reshape(b, l, SSM_HEADS, SSM_HEAD_DIM)
    rep = SSM_HEADS // SSM_GROUPS
    bm = jnp.repeat(xbc[..., SSM_INNER:SSM_INNER + SSM_GROUPS * SSM_STATE].reshape(b, l, SSM_GROUPS, SSM_STATE), rep, axis=2)
    cm = jnp.repeat(xbc[..., SSM_INNER + SSM_GROUPS * SSM_STATE:].reshape(b, l, SSM_GROUPS, SSM_STATE), rep, axis=2)
    dt = jax.nn.softplus(dt_raw.astype(f32).reshape(b, l, 2, SSM_HEADS) + lp['dt_bias'].astype(f32))
    a_neg = -jnp.exp(lp['a_log'].astype(f32))
    y_f, s_f = ssd_scan(xs, dt[:, :, 0], a_neg[0], bm, cm, init[:, 0])
    rev = lambda t: jnp.flip(t, axis=1)
    y_b, s_b = ssd_scan(rev(xs), rev(dt[:, :, 1]), a_neg[1], rev(bm), rev(cm), init[:, 1])
    y = y_f + rev(y_b) + lp['d_skip'].astype(f32)[:, None] * xs
    y = y.reshape(b, l, SSM_INNER) * jax.nn.silu(z.astype(f32))
    y = rms_norm(y, lp['ssm_norm_g'])
    return y.astype(z.dtype), jnp.stack([s_f, s_b], axis=1)


def chunk_mlp(u, v, lp):
    b, l, _ = u.shape
    nc = l // MLP_CHUNK
    v = rms_norm(v, lp['mlp_norm_g'])
    vc = v.reshape(b, nc, MLP_CHUNK, MLP_GROUPS, MLP_WIDTH // MLP_GROUPS)
    mixed = jnp.einsum('gts,bnsgc->bntgc', lp['mlp_w_s'], vc) + lp['mlp_b_s'].T[None, None, :, :, None]
    return u * mixed.reshape(b, l, MLP_WIDTH)


def peer(h, lp):
    b, l, d = h.shape
    t = b * l
    hf = h.reshape(t // PEER_BLOCK, PEER_BLOCK, d)
    w_q, sub_keys, u_tab, v_tab = lp['peer_w_q'], lp['peer_sub_keys'], lp['peer_u'], lp['peer_v']

    def block(xb):
        qv = (xb @ w_q).reshape(PEER_BLOCK, PEER_HEADS, 2, PEER_QDIM // 2)
        s = jnp.einsum('thkd,hknd->thkn', qv, sub_keys).astype(jnp.float32)
        top_s, top_i = lax.top_k(s, PEER_TOPK)
        cand_s = (top_s[:, :, 0, :, None] + top_s[:, :, 1, None, :]).reshape(PEER_BLOCK, PEER_HEADS, PEER_TOPK * PEER_TOPK)
        cand_i = (top_i[:, :, 0, :, None] * PEER_KEYS + top_i[:, :, 1, None, :]).reshape(PEER_BLOCK, PEER_HEADS, PEER_TOPK * PEER_TOPK)
        best_s, best_pos = lax.top_k(cand_s, PEER_TOPK)
        idx = jnp.take_along_axis(cand_i, best_pos, axis=-1)
        g = jax.nn.softmax(best_s, axis=-1)
        u_e = jnp.take(u_tab, idx, axis=0)
        act = jax.nn.gelu(jnp.einsum('thkd,td->thk', u_e, xb).astype(jnp.float32))
        v_e = jnp.take(v_tab, idx, axis=0)
        return jnp.einsum('thk,thkd->td', (g * act).astype(xb.dtype), v_e)

    return lax.map(block, hf).reshape(b, l, d)


def _layer(x, mod, lp, ctx):
    sh1, sc1, gt1, sh2, sc2, gt2 = mod
    b, n, _ = x.shape
    h = rms_norm(x, lp['g_mix']) * (1 + sc1) + sh1
    q, k, v, z, xbc, dt_raw, u, vm, gates = _split_cols(h @ lp['w_in'])
    q = q.reshape(b, n, NA_HEADS, NA_HEAD_DIM)
    k = k.reshape(b, n, NA_HEADS, NA_HEAD_DIM)
    v = v.reshape(b, n, NA_HEADS, NA_HEAD_DIM)
    if ctx is None:
        o_na = context_attention(q, k, v)
        init = jnp.zeros((b, 2, SSM_HEADS, SSM_HEAD_DIM, SSM_STATE), jnp.float32)
    else:
        k_ctx, v_ctx, init = ctx
        o_na = neighbourhood_attention(q, k, v, k_ctx.astype(x.dtype), v_ctx.astype(x.dtype), lp['rel_bias'])
    o_ssm, ssm_state = ssm_mixer(z, xbc, dt_raw, init.astype(jnp.float32), lp)
    o_mlp = chunk_mlp(jax.nn.gelu(u), jax.nn.gelu(vm), lp)
    g = jax.nn.sigmoid(gates.astype(jnp.float32)).astype(x.dtype).reshape(b, n, N_BRANCH, D_MODEL)
    merged = (g[:, :, 0] * (o_na @ lp['w_br_na']) + g[:, :, 1] * (o_ssm @ lp['w_br_ssm'])
              + g[:, :, 2] * (o_mlp @ lp['w_br_mlp']))
    x = x + gt1 * (merged @ lp['w_out'])
    h2 = rms_norm(x, lp['g_ffn']) * (1 + sc2) + sh2
    x = x + gt2 * peer(h2, lp)
    return x, k, v, ssm_state.astype(x.dtype)


def setup_inputs(seed: int = 0) -> dict:
    key = jax.random.key(seed)
    ks = iter(jax.random.split(key, 40))
    f32 = jnp.float32
    nrm = lambda shape, scale: jax.random.normal(next(ks), shape, f32) * scale
    dt0 = jnp.exp(jax.random.uniform(next(ks), (DEPTH, 2, SSM_HEADS), f32) * (math.log(0.1) - math.log(1e-3)) + math.log(1e-3))
    dt_bias = dt0 + jnp.log(-jnp.expm1(-dt0))
    a_log = jnp.log(jax.random.uniform(next(ks), (DEPTH, 2, SSM_HEADS), f32, minval=1.0, maxval=16.0))
    return {
        'x_prompt': nrm((BATCH, SEQ, D_MODEL), 1.0),
        'x_sample': nrm((DEC_BATCH, DEC_SEQ, D_MODEL), 1.0),
        'cache_k': nrm((DEC_BATCH, DEPTH, PAST_LEN, NA_HEADS, NA_HEAD_DIM), 1.0),
        'cache_v': nrm((DEC_BATCH, DEPTH, PAST_LEN, NA_HEADS, NA_HEAD_DIM), 1.0),
        'state_ssm': nrm((DEC_BATCH, DEPTH, 2, SSM_HEADS, SSM_HEAD_DIM, SSM_STATE), 0.1),
        'c': nrm((DEC_BATCH, D_MODEL), 1.0),
        'c_ctx': nrm((D_MODEL,), 1.0),
        'w_mod': nrm((DEPTH, D_MODEL, 6 * D_MODEL), 0.5 * D_MODEL ** -0.5),
        'b_mod': nrm((DEPTH, 6 * D_MODEL), 0.02),
        'g_norm_mix': 1.0 + nrm((DEPTH, D_MODEL), 0.02),
        'g_norm_ffn': 1.0 + nrm((DEPTH, D_MODEL), 0.02),
        'g_norm_final': 1.0 + nrm((D_MODEL,), 0.02),
        'w_in': nrm((DEPTH, D_MODEL, IN_WIDTH), D_MODEL ** -0.5),
        'na_rel_bias': nrm((DEPTH, NA_HEADS, 2 * NA_WIN_ROWS - 1, 2 * NA_WIN_COLS - 1), 0.1),
        'ssm_conv_w': nrm((DEPTH, SSM_CONV, SSM_CONV_DIM), SSM_CONV ** -0.5),
        'ssm_conv_b': nrm((DEPTH, SSM_CONV_DIM), 0.02),
        'ssm_dt_bias': dt_bias,
        'ssm_a_log': a_log,
        'ssm_d': 1.0 + nrm((DEPTH, SSM_HEADS), 0.02),
        'ssm_norm_g': 1.0 + nrm((DEPTH, SSM_INNER), 0.02),
        'mlp_norm_g': 1.0 + nrm((DEPTH, MLP_WIDTH), 0.02),
        'mlp_w_s': nrm((DEPTH, MLP_GROUPS, MLP_CHUNK, MLP_CHUNK), MLP_CHUNK ** -0.5),
        'mlp_b_s': 1.0 + nrm((DEPTH, MLP_GROUPS, MLP_CHUNK), 0.02),
        'w_br_na': nrm((DEPTH, NA_WIDTH, D_MODEL), NA_WIDTH ** -0.5),
        'w_br_ssm': nrm((DEPTH, SSM_INNER, D_MODEL), SSM_INNER ** -0.5),
        'w_br_mlp': nrm((DEPTH, MLP_WIDTH, D_MODEL), MLP_WIDTH ** -0.5),
        'w_out': nrm((DEPTH, D_MODEL, D_MODEL), D_MODEL ** -0.5),
        'peer_w_q': nrm((DEPTH, D_MODEL, PEER_HEADS * PEER_QDIM), D_MODEL ** -0.5),
        'peer_sub_keys': nrm((DEPTH, PEER_HEADS, 2, PEER_KEYS, PEER_QDIM // 2), (PEER_QDIM // 2) ** -0.5),
        'peer_u': nrm((DEPTH, PEER_EXPERTS, D_MODEL), D_MODEL ** -0.5),
        'peer_v': nrm((DEPTH, PEER_EXPERTS, D_MODEL), PEER_TOPK ** -0.5),
    }


def reference(x_prompt, x_sample, cache_k, cache_v, state_ssm, c, c_ctx, w_mod, b_mod, g_norm_mix, g_norm_ffn,
              g_norm_final, w_in, na_rel_bias, ssm_conv_w, ssm_conv_b, ssm_dt_bias, ssm_a_log, ssm_d, ssm_norm_g,
              mlp_norm_g, mlp_w_s, mlp_b_s, w_br_na, w_br_ssm, w_br_mlp, w_out, peer_w_q, peer_sub_keys, peer_u, peer_v):
    xp = x_prompt
    xs = x_sample
    new_k, new_v, new_s = [], [], []
    for l in range(DEPTH):
        lp = {
            'g_mix': g_norm_mix[l], 'g_ffn': g_norm_ffn[l], 'w_in': w_in[l], 'rel_bias': na_rel_bias[l],
            'conv_w': ssm_conv_w[l], 'conv_b': ssm_conv_b[l], 'dt_bias': ssm_dt_bias[l], 'a_log': ssm_a_log[l],
            'd_skip': ssm_d[l], 'ssm_norm_g': ssm_norm_g[l], 'mlp_norm_g': mlp_norm_g[l], 'mlp_w_s': mlp_w_s[l],
            'mlp_b_s': mlp_b_s[l], 'w_br_na': w_br_na[l], 'w_br_ssm': w_br_ssm[l], 'w_br_mlp': w_br_mlp[l],
            'w_out': w_out[l], 'peer_w_q': peer_w_q[l], 'peer_sub_keys': peer_sub_keys[l], 'peer_u': peer_u[l],
            'peer_v': peer_v[l],
        }
        mod_ctx = _modulation(c_ctx[None, None, :], w_mod[l], b_mod[l])
        mod_lat = _modulation(c[:, None, :], w_mod[l], b_mod[l])
        xp, k_l, v_l, s_l = _layer(xp, mod_ctx, lp, None)
        xs, _, _, _ = _layer(xs, mod_lat, lp, (cache_k[:, l], cache_v[:, l], state_ssm[:, l]))
        new_k.append(k_l)
        new_v.append(v_l)
        new_s.append(s_l)
    y_prompt = rms_norm(xp, g_norm_final)
    y_sample = rms_norm(xs, g_norm_final)
    new_cache_k = jnp.stack(new_k, axis=1)
    new_cache_v = jnp.stack(new_v, axis=1)
    new_state_ssm = jnp.stack(new_s, axis=1)
    return (y_prompt, y_sample, new_cache_k, new_cache_v, new_state_ssm)
```

```python
import functools
import math

import numpy as np
import jax
import jax.numpy as jnp
from jax import lax
from jax.experimental import pallas as pl
from jax.experimental.pallas import tpu as pltpu

F32 = jnp.float32
BF16 = jnp.bfloat16

D_MODEL = 2048
DEPTH = 4
GRID_W = 64
EPS = 1e-6
NA_HEADS = 8
NA_HEAD_DIM = 128
NA_WIDTH = NA_HEADS * NA_HEAD_DIM
NA_WIN_ROWS = 8
NA_WIN_COLS = 16
SSM_HEADS = 16
SSM_HEAD_DIM = 64
SSM_INNER = SSM_HEADS * SSM_HEAD_DIM
SSM_GROUPS = 2
SSM_STATE = 128
SSM_CHUNK = 128
SSM_BC = SSM_GROUPS * SSM_STATE
SSM_CONV_DIM = SSM_INNER + 2 * SSM_BC
MLP_GROUPS = 8
MLP_WIDTH = 1024
MLP_CHUNK = 128
PEER_HEADS = 8
PEER_KEYS = 128
PEER_EXPERTS = PEER_KEYS * PEER_KEYS
PEER_QDIM = 256
PEER_TOPK = 16
N_BRANCH = 3

LANES = 128
SSM_COLS = SSM_INNER + SSM_CONV_DIM + 2 * LANES
DT_BLOCK0 = (SSM_INNER + SSM_CONV_DIM) // LANES
VMEM_LIMIT = 60 * 1024 * 1024
NEG_BIG = -1e30


def _cparams(sem):
    return pltpu.CompilerParams(dimension_semantics=sem, vmem_limit_bytes=VMEM_LIMIT)


def _nt_dot(a, b):
    return lax.dot_general(a, b, (((1,), (1,)), ((), ())), preferred_element_type=F32)


def _tn_dot(a, b):
    return lax.dot_general(a, b, (((0,), (0,)), ((), ())), preferred_element_type=F32)


def _mod_kernel(c_ref, w_ref, b_ref, o_ref):
    c = c_ref[...]
    a = c * jax.nn.sigmoid(c)
    o_ref[...] = jnp.dot(a, w_ref[...], preferred_element_type=F32,
                         precision=lax.Precision.HIGHEST) + b_ref[...]


def modulation(cvec, w_mod, b_mod):
    depth, d, n = w_mod.shape
    tn = 1024
    return pl.pallas_call(
        _mod_kernel,
        grid=(depth, n // tn),
        in_specs=[
            pl.BlockSpec((8, d), lambda l, j: (0, 0)),
            pl.BlockSpec((None, d, tn), lambda l, j: (l, 0, j)),
            pl.BlockSpec((None, 1, tn), lambda l, j: (l, 0, j)),
        ],
        out_specs=pl.BlockSpec((None, 8, tn), lambda l, j: (l, 0, j)),
        out_shape=jax.ShapeDtypeStruct((depth, 8, n), F32),
        compiler_params=_cparams(("parallel", "parallel")),
        name="modulation",
    )(cvec, w_mod, b_mod.reshape(depth, 1, n))


def _norm_mod_kernel(x_ref, g_ref, sc_ref, sh_ref, o_ref, *, transpose):
    x = x_ref[...]
    y = x * lax.rsqrt(jnp.mean(x * x, axis=-1, keepdims=True) + EPS) * g_ref[...]
    h = y * (1.0 + sc_ref[...]) + sh_ref[...]
    if transpose:
        h = h.T
    o_ref[...] = h.astype(o_ref.dtype)


def norm_mod(x, g, sc, sh, seg_len, transpose=False, tm=512):
    t, d = x.shape
    spt = seg_len // tm
    seg_spec = pl.BlockSpec((None, 1, d), lambda i: (i // spt, 0, 0))
    if transpose:
        out_spec = pl.BlockSpec((d, tm), lambda i: (0, i))
        out_shape = jax.ShapeDtypeStruct((d, t), BF16)
    else:
        out_spec = pl.BlockSpec((tm, d), lambda i: (i, 0))
        out_shape = jax.ShapeDtypeStruct((t, d), BF16)
    return pl.pallas_call(
        functools.partial(_norm_mod_kernel, transpose=transpose),
        grid=(t // tm,),
        in_specs=[pl.BlockSpec((tm, d), lambda i: (i, 0)),
                  pl.BlockSpec((1, d), lambda i: (0, 0)), seg_spec, seg_spec],
        out_specs=out_spec,
        out_shape=out_shape,
        compiler_params=_cparams(("parallel",)),
        name="norm_mod_t" if transpose else "norm_mod",
    )(x, g, sc, sh)


def _final_norm_kernel(x_ref, g_ref, o_ref):
    x = x_ref[...]
    o_ref[...] = x * lax.rsqrt(jnp.mean(x * x, axis=-1, keepdims=True) + EPS) * g_ref[...]


def final_norm(x, g, tm=512):
    t, d = x.shape
    return pl.pallas_call(
        _final_norm_kernel,
        grid=(t // tm,),
        in_specs=[pl.BlockSpec((tm, d), lambda i: (i, 0)), pl.BlockSpec((1, d), lambda i: (0, 0))],
        out_specs=pl.BlockSpec((tm, d), lambda i: (i, 0)),
        out_shape=jax.ShapeDtypeStruct((t, d), F32),
        compiler_params=_cparams(("parallel",)),
        name="final_norm",
    )(x, g)


def _mm_kernel(a_ref, b_ref, o_ref):
    o_ref[...] = jnp.dot(a_ref[...], b_ref[...], preferred_element_type=F32).astype(o_ref.dtype)


def matmul(a, b, tm, tn, out_dtype=F32, name="matmul"):
    m, k = a.shape
    n = b.shape[1]
    tm = min(tm, m)
    return pl.pallas_call(
        _mm_kernel,
        grid=(m // tm, n // tn),
        in_specs=[pl.BlockSpec((tm, k), lambda i, j: (i, 0)),
                  pl.BlockSpec((k, tn), lambda i, j: (0, j))],
        out_specs=pl.BlockSpec((tm, tn), lambda i, j: (i, j)),
        out_shape=jax.ShapeDtypeStruct((m, n), out_dtype),
        compiler_params=_cparams(("parallel", "parallel")),
        name=name,
    )(a, b)


def _ctx_attn_kernel(q_ref, k_ref, v_ref, o_ref):
    scale = NA_HEAD_DIM ** -0.5
    q = q_ref[...].astype(BF16)
    k = k_ref[...].astype(BF16)
    v = v_ref[...].astype(BF16)
    s = _nt_dot(q, k) * scale
    m = jnp.max(s, axis=-1, keepdims=True)
    p = jnp.exp(s - m)
    p = p / jnp.sum(p, axis=-1, keepdims=True)
    o_ref[...] = jnp.dot(p.astype(BF16), v, preferred_element_type=F32).astype(o_ref.dtype)


def context_attention(qkv, nseq, seq):
    hd = NA_HEAD_DIM
    return pl.pallas_call(
        _ctx_attn_kernel,
        grid=(nseq, NA_HEADS),
        in_specs=[pl.BlockSpec((seq, hd), lambda b, h: (b, h)),
                  pl.BlockSpec((seq, hd), lambda b, h: (b, NA_HEADS + h)),
                  pl.BlockSpec((seq, hd), lambda b, h: (b, 2 * NA_HEADS + h))],
        out_specs=pl.BlockSpec((seq, hd), lambda b, h: (b, h)),
        out_shape=jax.ShapeDtypeStruct((nseq * seq, NA_WIDTH), BF16),
        compiler_params=_cparams(("parallel", "parallel")),
        name="ctx_attention",
    )(qkv, qkv, qkv)


def _na_kernel(q_ref, k_ref, v_ref, kc_ref, vc_ref, bias_ref, o_ref, kb_scr, vb_scr, *, rows):
    scale = NA_HEAD_DIM ** -0.5
    band = NA_WIN_ROWS * GRID_W
    kb_scr[...] = k_ref[...].astype(BF16)
    vb_scr[...] = v_ref[...].astype(BF16)
    kc = kc_ref[...].astype(BF16)
    vc = vc_ref[...].astype(BF16)

    def row_step(r, carry):
        rs = jnp.clip(r - NA_WIN_ROWS // 2, 0, rows - NA_WIN_ROWS)
        q = q_ref[pl.ds(pl.multiple_of(r * GRID_W, GRID_W), GRID_W), :].astype(BF16)
        k0 = pl.multiple_of(rs * GRID_W, GRID_W)
        kb = kb_scr[pl.ds(k0, band), :]
        vb = vb_scr[pl.ds(k0, band), :]
        s_win = _nt_dot(q, kb) * scale + bias_ref[r - rs]
        s_ctx = _nt_dot(q, kc) * scale
        m = jnp.maximum(jnp.max(s_win, axis=-1, keepdims=True), jnp.max(s_ctx, axis=-1, keepdims=True))
        p_win = jnp.exp(s_win - m)
        p_ctx = jnp.exp(s_ctx - m)
        den = jnp.sum(p_win, axis=-1, keepdims=True) + jnp.sum(p_ctx, axis=-1, keepdims=True)
        o = (jnp.dot((p_win / den).astype(BF16), vb, preferred_element_type=F32)
             + jnp.dot((p_ctx / den).astype(BF16), vc, preferred_element_type=F32))
        o_ref[pl.ds(pl.multiple_of(r * GRID_W, GRID_W), GRID_W), :] = o.astype(o_ref.dtype)
        return carry

    lax.fori_loop(0, rows, row_step, 0)


def _na_bias_table(rel_bias):
    o = np.arange(NA_WIN_ROWS)
    ridx = o[None, :] - o[:, None] + NA_WIN_ROWS - 1
    col = np.arange(GRID_W)
    cstart = np.clip(col - NA_WIN_COLS // 2, 0, GRID_W - NA_WIN_COLS)
    valid = (col[None, :] >= cstart[:, None]) & (col[None, :] < cstart[:, None] + NA_WIN_COLS)
    cidx = np.clip(col[None, :] - col[:, None] + NA_WIN_COLS - 1, 0, 2 * NA_WIN_COLS - 2)
    tab = rel_bias[:, ridx[:, None, :, None], cidx[None, :, None, :]]
    tab = jnp.where(valid[None, None, :, None, :], tab, NEG_BIG)
    return tab.reshape(rel_bias.shape[0], NA_WIN_ROWS, GRID_W, NA_WIN_ROWS * GRID_W).astype(F32)


def neighbourhood_attention(qkv, cache_k, cache_v, layer, bias_tab, nseq, seq):
    hd = NA_HEAD_DIM
    past = cache_k.shape[2]
    rows = seq // GRID_W
    band = NA_WIN_ROWS * GRID_W
    return pl.pallas_call(
        functools.partial(_na_kernel, rows=rows),
        grid=(nseq, NA_HEADS),
        in_specs=[pl.BlockSpec((seq, hd), lambda b, h: (b, h)),
                  pl.BlockSpec((seq, hd), lambda b, h: (b, NA_HEADS + h)),
                  pl.BlockSpec((seq, hd), lambda b, h: (b, 2 * NA_HEADS + h)),
                  pl.BlockSpec((None, None, past, hd), lambda b, h: (b, layer, 0, h)),
                  pl.BlockSpec((None, None, past, hd), lambda b, h: (b, layer, 0, h)),
                  pl.BlockSpec((None, NA_WIN_ROWS, GRID_W, band), lambda b, h: (h, 0, 0, 0))],
        out_specs=pl.BlockSpec((seq, hd), lambda b, h: (b, h)),
        out_shape=jax.ShapeDtypeStruct((nseq * seq, NA_WIDTH), BF16),
        scratch_shapes=[pltpu.VMEM((seq, hd), BF16), pltpu.VMEM((seq, hd), BF16)],
        compiler_params=_cparams(("parallel", "parallel")),
        name="nbr_attention",
    )(qkv, qkv, qkv, cache_k, cache_v, bias_tab)


def _conv_kernel(x_ref, w_ref, b_ref, o_ref):
    x = x_ref[...]
    n = x.shape[0]
    row = lax.broadcasted_iota(jnp.int32, x.shape, 0)
    xm2 = jnp.where(row >= 2, pltpu.roll(x, 2, 0), 0.0)
    xm1 = jnp.where(row >= 1, pltpu.roll(x, 1, 0), 0.0)
    xp1 = jnp.where(row < n - 1, pltpu.roll(x, n - 1, 0), 0.0)
    w = w_ref[...]
    y = w[0:1] * xm2 + w[1:2] * xm1 + w[2:3] * x + w[3:4] * xp1 + b_ref[...]
    o_ref[...] = y * jax.nn.sigmoid(y)


def conv_silu(ssm, conv_w, conv_b, nseq, seq):
    tc = 256
    c0 = SSM_INNER // tc
    return pl.pallas_call(
        _conv_kernel,
        grid=(nseq, SSM_CONV_DIM // tc),
        in_specs=[pl.BlockSpec((None, seq, tc), lambda s, j: (s, 0, c0 + j)),
                  pl.BlockSpec((4, tc), lambda s, j: (0, j)),
                  pl.BlockSpec((1, tc), lambda s, j: (0, j))],
        out_specs=pl.BlockSpec((None, seq, tc), lambda s, j: (s, 0, j)),
        out_shape=jax.ShapeDtypeStruct((nseq, seq, SSM_CONV_DIM), F32),
        compiler_params=_cparams(("parallel", "parallel")),
        name="conv_silu",
    )(ssm.reshape(nseq, seq, SSM_COLS), conv_w, conv_b)


def _ssd_kernel(*refs, has_init, nchunk):
    if has_init:
        xc_ref, dt_ref, tri_ref, dtb_ref, alog_ref, init_ref, y_ref, fin_ref, st_scr = refs
    else:
        xc_ref, dt_ref, tri_ref, dtb_ref, alog_ref, y_ref, fin_ref, st_scr = refs
        init_ref = None
    c = pl.program_id(2)

    @pl.when(c == 0)
    def _():
        if has_init:
            st_scr[...] = init_ref[...]
        else:
            st_scr[...] = jnp.zeros_like(st_scr)

    p = SSM_HEAD_DIM
    n = SSM_STATE
    xc = xc_ref[...]
    dt = jax.nn.softplus(dt_ref[...] + dtb_ref[...])
    a = dt * (-jnp.exp(alog_ref[...]))
    tri = tri_ref[...]
    mask = tri > 0.5
    cum = jnp.dot(tri, a, preferred_element_type=F32, precision=lax.Precision.HIGHEST)
    cum_t = cum.T
    tot = jnp.min(cum, axis=0, keepdims=True)
    e_cum = jnp.exp(cum)
    e_end = jnp.exp(tot - cum)
    e_tot = jnp.exp(tot)
    rep = SSM_HEADS // SSM_GROUPS
    gmat = []
    bmat = []
    cmat = []
    for g in range(SSM_GROUPS):
        b_g = xc[:, SSM_INNER + g * n:SSM_INNER + (g + 1) * n]
        c_g = xc[:, SSM_INNER + SSM_BC + g * n:SSM_INNER + SSM_BC + (g + 1) * n]
        bmat.append(b_g)
        cmat.append(c_g.astype(BF16))
        gmat.append(_nt_dot(c_g.astype(BF16), b_g.astype(BF16)))
    for h in range(SSM_HEADS):
        g = h // rep
        seg = cum[:, h:h + 1] - cum_t[h:h + 1, :]
        decay = jnp.exp(jnp.where(mask, seg, -jnp.inf))
        xd = (xc[:, h * p:(h + 1) * p] * dt[:, h:h + 1]).astype(BF16)
        y_diag = jnp.dot((gmat[g] * decay).astype(BF16), xd, preferred_element_type=F32)
        b_dec = (bmat[g] * e_end[:, h:h + 1]).astype(BF16)
        chunk_state = _tn_dot(xd, b_dec)
        prev = st_scr[h]
        y_off = _nt_dot(cmat[g], prev.astype(BF16)) * e_cum[:, h:h + 1]
        y_ref[:, h * p:(h + 1) * p] = y_diag + y_off
        st_scr[h] = prev * e_tot[:, h:h + 1] + chunk_state

    @pl.when(c == nchunk - 1)
    def _():
        fin_ref[...] = st_scr[...]


def ssd_scan(xc, ssm, dt_bias, a_log, init, layer):
    nseq, seq, _ = xc.shape
    q = SSM_CHUNK
    nc = seq // q
    idx = np.arange(q)
    tri = jnp.asarray(np.stack([idx[None, :] <= idx[:, None], idx[None, :] >= idx[:, None]]).astype(np.float32))

    def cc(d, c):
        return c + d * (nc - 1 - 2 * c)

    in_specs = [pl.BlockSpec((None, q, SSM_CONV_DIM), lambda d, s, c: (s, cc(d, c), 0)),
                pl.BlockSpec((None, q, LANES), lambda d, s, c: (s, cc(d, c), DT_BLOCK0 + d)),
                pl.BlockSpec((None, q, q), lambda d, s, c: (d, 0, 0)),
                pl.BlockSpec((None, 1, LANES), lambda d, s, c: (d, 0, 0)),
                pl.BlockSpec((None, 1, LANES), lambda d, s, c: (d, 0, 0))]
    args = [xc, ssm.reshape(nseq, seq, SSM_COLS), tri, dt_bias, a_log]
    if init is not None:
        in_specs.append(pl.BlockSpec((None, None, None, SSM_HEADS, SSM_HEAD_DIM, SSM_STATE),
                                     lambda d, s, c: (s, layer, d, 0, 0, 0)))
        args.append(init)
    return pl.pallas_call(
        functools.partial(_ssd_kernel, has_init=init is not None, nchunk=nc),
        grid=(2, nseq, nc),
        in_specs=in_specs,
        out_specs=[pl.BlockSpec((None, None, q, SSM_INNER), lambda d, s, c: (d, s, cc(d, c), 0)),
                   pl.BlockSpec((None, None, SSM_HEADS, SSM_HEAD_DIM, SSM_STATE), lambda d, s, c: (s, d, 0, 0, 0))],
        out_shape=[jax.ShapeDtypeStruct((2, nseq, seq, SSM_INNER), F32),
                   jax.ShapeDtypeStruct((nseq, 2, SSM_HEADS, SSM_HEAD_DIM, SSM_STATE), F32)],
        scratch_shapes=[pltpu.VMEM((SSM_HEADS, SSM_HEAD_DIM, SSM_STATE), F32)],
        compiler_params=_cparams(("parallel", "parallel", "arbitrary")),
        name="ssd_scan",
    )(*args)


def _ssm_finish_kernel(y_ref, xs_ref, z_ref, dsk_ref, g_ref, o_ref):
    z = z_ref[...]
    y = (y_ref[0] + y_ref[1] + dsk_ref[...] * xs_ref[...]) * (z * jax.nn.sigmoid(z))
    y = y * lax.rsqrt(jnp.mean(y * y, axis=-1, keepdims=True) + EPS) * g_ref[...]
    o_ref[...] = y.astype(o_ref.dtype)


def ssm_finish(y, xc, ssm, d_skip, g, tm=512):
    t = ssm.shape[0]
    w = SSM_INNER
    return pl.pallas_call(
        _ssm_finish_kernel,
        grid=(t // tm,),
        in_specs=[pl.BlockSpec((2, tm, w), lambda i: (0, i, 0)),
                  pl.BlockSpec((tm, w), lambda i: (i, 0)),
                  pl.BlockSpec((tm, w), lambda i: (i, 0)),
                  pl.BlockSpec((1, w), lambda i: (0, 0)),
                  pl.BlockSpec((1, w), lambda i: (0, 0))],
        out_specs=pl.BlockSpec((tm, w), lambda i: (i, 0)),
        out_shape=jax.ShapeDtypeStruct((t, w), BF16),
        compiler_params=_cparams(("parallel",)),
        name="ssm_finish",
    )(y, xc, ssm, d_skip, g)


def _gmlp_kernel(u_ref, v_ref, g_ref, w_ref, b_ref, o_ref):
    v = jax.nn.gelu(v_ref[...])
    v = (v * lax.rsqrt(jnp.mean(v * v, axis=-1, keepdims=True) + EPS) * g_ref[...]).astype(BF16)
    q = MLP_CHUNK
    gw = MLP_WIDTH // MLP_GROUPS
    for ch in range(v.shape[0] // q):
        for g in range(MLP_GROUPS):
            mixed = jnp.dot(w_ref[g], v[ch * q:(ch + 1) * q, g * gw:(g + 1) * gw],
                            preferred_element_type=F32) + b_ref[g]
            u = jax.nn.gelu(u_ref[ch * q:(ch + 1) * q, g * gw:(g + 1) * gw])
            o_ref[ch * q:(ch + 1) * q, g * gw:(g + 1) * gw] = (u * mixed).astype(o_ref.dtype)


def chunk_mlp(mlp, g, w_s, b_s, tm=512):
    t = mlp.shape[0]
    w = MLP_WIDTH
    return pl.pallas_call(
        _gmlp_kernel,
        grid=(t // tm,),
        in_specs=[pl.BlockSpec((tm, w), lambda i: (i, 0)),
                  pl.BlockSpec((tm, w), lambda i: (i, 1)),
                  pl.BlockSpec((1, w), lambda i: (0, 0)),
                  pl.BlockSpec(w_s.shape, lambda i: (0, 0, 0)),
                  pl.BlockSpec(b_s.shape, lambda i: (0, 0, 0))],
        out_specs=pl.BlockSpec((tm, w), lambda i: (i, 0)),
        out_shape=jax.ShapeDtypeStruct((t, w), BF16),
        compiler_params=_cparams(("parallel",)),
        name="chunk_mlp",
    )(mlp, mlp, g, w_s, b_s)


def _merge_kernel(a0_ref, a1_ref, a2_ref, w_ref, g0_ref, g1_ref, g2_ref, o_ref):
    acc = jax.nn.sigmoid(g0_ref[...]) * jnp.dot(a0_ref[...], w_ref[0], preferred_element_type=F32)
    acc += jax.nn.sigmoid(g1_ref[...]) * jnp.dot(a1_ref[...], w_ref[1], preferred_element_type=F32)
    acc += jax.nn.sigmoid(g2_ref[...]) * jnp.dot(a2_ref[...], w_ref[2], preferred_element_type=F32)
    o_ref[...] = acc.astype(o_ref.dtype)


def merge_branches(o_na, o_ssm, o_mlp, w_br, gates, tm=512, tn=512):
    t, k = o_na.shape
    d = w_br.shape[2]
    nb = d // tn
    a_spec = pl.BlockSpec((tm, k), lambda i, j: (i, 0))
    return pl.pallas_call(
        _merge_kernel,
        grid=(t // tm, nb),
        in_specs=[a_spec, a_spec, a_spec,
                  pl.BlockSpec((N_BRANCH, k, tn), lambda i, j: (0, 0, j)),
                  pl.BlockSpec((tm, tn), lambda i, j: (i, j)),
                  pl.BlockSpec((tm, tn), lambda i, j: (i, nb + j)),
                  pl.BlockSpec((tm, tn), lambda i, j: (i, 2 * nb + j))],
        out_specs=pl.BlockSpec((tm, tn), lambda i, j: (i, j)),
        out_shape=jax.ShapeDtypeStruct((t, d), BF16),
        compiler_params=_cparams(("parallel", "parallel")),
        name="merge_branches",
    )(o_na, o_ssm, o_mlp, w_br, gates, gates, gates)


def _out_proj_kernel(a_ref, w_ref, x_ref, gt_ref, o_ref):
    o_ref[...] = x_ref[...] + gt_ref[...] * jnp.dot(a_ref[...], w_ref[...], preferred_element_type=F32)


def out_proj(merged, w_out, x, gt, seg_len, tm=512, tn=512):
    t, k = merged.shape
    d = w_out.shape[1]
    spt = seg_len // tm
    return pl.pallas_call(
        _out_proj_kernel,
        grid=(t // tm, d // tn),
        in_specs=[pl.BlockSpec((tm, k), lambda i, j: (i, 0)),
                  pl.BlockSpec((k, tn), lambda i, j: (0, j)),
                  pl.BlockSpec((tm, tn), lambda i, j: (i, j)),
                  pl.BlockSpec((None, 1, tn), lambda i, j: (i // spt, 0, j))],
        out_specs=pl.BlockSpec((tm, tn), lambda i, j: (i, j)),
        out_shape=jax.ShapeDtypeStruct((t, d), F32),
        compiler_params=_cparams(("parallel", "parallel")),
        name="out_proj",
    )(merged, w_out, x, gt)


def _peer_scores_kernel(wq_ref, h_ref, key_ref, o_ref):
    qv = jnp.dot(wq_ref[...], h_ref[...], preferred_element_type=F32).astype(BF16)
    dq = PEER_QDIM // 2
    for hk in range(2 * PEER_HEADS):
        o_ref[hk * PEER_KEYS:(hk + 1) * PEER_KEYS, :] = jnp.dot(
            key_ref[hk], qv[hk * dq:(hk + 1) * dq, :], preferred_element_type=F32)


def peer_scores(wq_t, h_t, keys, tl=512):
    d, t = h_t.shape
    rows = 2 * PEER_HEADS * PEER_KEYS
    return pl.pallas_call(
        _peer_scores_kernel,
        grid=(t // tl,),
        in_specs=[pl.BlockSpec(wq_t.shape, lambda i: (0, 0)),
                  pl.BlockSpec((d, tl), lambda i: (0, i)),
                  pl.BlockSpec(keys.shape, lambda i: (0, 0, 0))],
        out_specs=pl.BlockSpec((rows, tl), lambda i: (0, i)),
        out_shape=jax.ShapeDtypeStruct((rows, t), F32),
        compiler_params=_cparams(("parallel",)),
        name="peer_scores",
    )(wq_t, h_t, keys)


def _top16(s):
    nk = s.shape[0]
    kio = lax.broadcasted_iota(jnp.int32, s.shape, 0).astype(F32)
    rank = jnp.full(s.shape, float(PEER_TOPK), F32)
    vals = []
    for a in range(PEER_TOPK):
        m = jnp.max(s, axis=0, keepdims=True)
        idx = jnp.min(jnp.where(s == m, kio, float(nk)), axis=0, keepdims=True)
        hit = kio == idx
        rank = jnp.where(hit, float(a), rank)
        s = jnp.where(hit, -jnp.inf, s)
        vals.append(m)
    return jnp.concatenate(vals, axis=0), rank


def _peer_select_kernel(s_ref, rank2_ref, cnt_ref, e1_ref, e2_ref):
    nk = PEER_KEYS
    k = PEER_TOPK
    for h in range(PEER_HEADS):
        s1 = s_ref[(2 * h) * nk:(2 * h + 1) * nk, :]
        s2 = s_ref[(2 * h + 1) * nk:(2 * h + 2) * nk, :]
        tv1, rank1 = _top16(s1)
        tv2, rank2 = _top16(s2)
        cand = jnp.concatenate([tv1[a:a + 1, :] + tv2 for a in range(k)], axis=0)
        pio = lax.broadcasted_iota(jnp.int32, cand.shape, 0).astype(F32)
        aio = lax.broadcasted_iota(jnp.int32, (k, cand.shape[1]), 0).astype(F32)
        cnt_a = jnp.zeros((k, cand.shape[1]), F32)
        zsum = jnp.zeros((1, cand.shape[1]), F32)
        top = tv1[0:1, :] + tv2[0:1, :]
        for _ in range(k):
            m = jnp.max(cand, axis=0, keepdims=True)
            pos = jnp.min(jnp.where(cand == m, pio, float(k * k)), axis=0, keepdims=True)
            cand = jnp.where(pio == pos, -jnp.inf, cand)
            cnt_a = cnt_a + jnp.where(aio == jnp.floor(pos * (1.0 / k)), 1.0, 0.0)
            zsum = zsum + jnp.exp(m - top)
        cnt = jnp.zeros(s1.shape, F32)
        for a in range(k):
            cnt = jnp.where(rank1 == float(a), cnt_a[a:a + 1, :], cnt)
        rank2_ref[h * nk:(h + 1) * nk, :] = rank2
        cnt_ref[h * nk:(h + 1) * nk, :] = cnt
        e1_ref[h * nk:(h + 1) * nk, :] = jnp.exp(s1 - tv1[0:1, :])
        e2_ref[h * nk:(h + 1) * nk, :] = jnp.exp(s2 - tv2[0:1, :]) / zsum


def peer_select(scores, tl=128):
    rows, t = scores.shape
    half = rows // 2
    spec = pl.BlockSpec((half, tl), lambda i: (0, i))
    shp = jax.ShapeDtypeStruct((half, t), F32)
    return pl.pallas_call(
        _peer_select_kernel,
        grid=(t // tl,),
        in_specs=[pl.BlockSpec((rows, tl), lambda i: (0, i))],
        out_specs=[spec, spec, spec, spec],
        out_shape=[shp, shp, shp, shp],
        compiler_params=_cparams(("parallel",)),
        name="peer_select",
    )(scores)


def _peer_dense_kernel(h_ref, u_ref, vt_ref, rank2_ref, cnt_ref, e1_ref, e2_ref, x_ref, gt_ref, o_ref,
                       acc_scr, *, nchunk, rows_per_chunk):
    e = pl.program_id(1)

    @pl.when(e == 0)
    def _():
        acc_scr[...] = jnp.zeros_like(acc_scr)

    nk = PEER_KEYS
    act = jax.nn.gelu(jnp.dot(u_ref[...], h_ref[...], preferred_element_type=F32))
    parts = []
    for ii in range(rows_per_chunk):
        w = None
        for h in range(PEER_HEADS):
            row = h * nk + e * rows_per_chunk + ii
            cnt_row = cnt_ref[pl.ds(row, 1), :]
            e1_row = e1_ref[pl.ds(row, 1), :]
            sel = rank2_ref[h * nk:(h + 1) * nk, :] < cnt_row
            term = jnp.where(sel, e2_ref[h * nk:(h + 1) * nk, :], 0.0) * e1_row
            w = term if w is None else w + term
        parts.append((w * act[ii * nk:(ii + 1) * nk, :]).astype(BF16))
    z = jnp.concatenate(parts, axis=0)
    acc_scr[...] += jnp.dot(vt_ref[...], z, preferred_element_type=F32)

    @pl.when(e == nchunk - 1)
    def _():
        o_ref[...] = x_ref[...] + gt_ref[...] * acc_scr[...].T


def peer_dense(h_t, u_tab, v_t, rank2, cnt, e1, e2, x, gt, seg_len, tm=512, et=512):
    d, t = h_t.shape
    ne = u_tab.shape[0]
    nchunk = ne // et
    spt = seg_len // tm
    map_spec = pl.BlockSpec((rank2.shape[0], tm), lambda i, e: (0, i))
    return pl.pallas_call(
        functools.partial(_peer_dense_kernel, nchunk=nchunk, rows_per_chunk=et // PEER_KEYS),
        grid=(t // tm, nchunk),
        in_specs=[pl.BlockSpec((d, tm), lambda i, e: (0, i)),
                  pl.BlockSpec((et, d), lambda i, e: (e, 0)),
                  pl.BlockSpec((d, et), lambda i, e: (0, e)),
                  map_spec, map_spec, map_spec, map_spec,
                  pl.BlockSpec((tm, d), lambda i, e: (i, 0)),
                  pl.BlockSpec((None, 1, d), lambda i, e: (i // spt, 0, 0))],
        out_specs=pl.BlockSpec((tm, d), lambda i, e: (i, 0)),
        out_shape=jax.ShapeDtypeStruct((t, d), F32),
        scratch_shapes=[pltpu.VMEM((d, tm), F32)],
        compiler_params=_cparams(("parallel", "arbitrary")),
        name="peer_dense",
    )(h_t, u_tab, v_t, rank2, cnt, e1, e2, x, gt)


def _layer(x, mod, lw, layer, nseq, seq, ctx):
    sh1, sc1, gt1, sh2, sc2, gt2 = mod
    seg_len = x.shape[0] // sh1.shape[0]
    h = norm_mod(x, lw["g_mix"], sc1, sh1, seg_len)
    qkv = matmul(h, lw["w_qkv"], 1024, 512, name="proj_qkv")
    ssm = matmul(h, lw["w_ssm"], 1024, 128 * 11, name="proj_ssm")
    mlp = matmul(h, lw["w_mlp"], 1024, 512, name="proj_mlp")
    gates = matmul(h, lw["w_gate"], 1024, 512, name="proj_gate")
    if ctx is None:
        o_na = context_attention(qkv, nseq, seq)
        init = None
    else:
        cache_k, cache_v, init = ctx
        o_na = neighbourhood_attention(qkv, cache_k, cache_v, layer, lw["na_bias"], nseq, seq)
    xc = conv_silu(ssm, lw["conv_w"], lw["conv_b"], nseq, seq)
    y, fin = ssd_scan(xc, ssm, lw["dt_bias"], lw["a_log"], init, layer)
    t = nseq * seq
    o_ssm = ssm_finish(y.reshape(2, t, SSM_INNER), xc.reshape(t, SSM_CONV_DIM), ssm, lw["d_skip"], lw["ssm_norm_g"])
    o_mlp = chunk_mlp(mlp, lw["mlp_norm_g"], lw["mlp_w_s"], lw["mlp_b_s"])
    merged = merge_branches(o_na, o_ssm, o_mlp, lw["w_br"], gates)
    x = out_proj(merged, lw["w_out"], x, gt1, seg_len)
    h2t = norm_mod(x, lw["g_ffn"], sc2, sh2, seg_len, transpose=True)
    scores = peer_scores(lw["peer_wq_t"], h2t, lw["peer_keys"])
    rank2, cnt, e1, e2 = peer_select(scores)
    x = peer_dense(h2t, lw["peer_u"], lw["peer_v_t"], rank2, cnt, e1, e2, x, gt2, seg_len)
    return x, qkv, fin


def _pad_lanes(v):
    pad = LANES - v.shape[-1]
    return jnp.pad(v, [(0, 0)] * (v.ndim - 1) + [(0, pad)])[..., None, :]


def kernel(x_prompt, x_sample, cache_k, cache_v, state_ssm, c, c_ctx, w_mod, b_mod, g_norm_mix, g_norm_ffn, g_norm_final, w_in, na_rel_bias, ssm_conv_w, ssm_conv_b, ssm_dt_bias, ssm_a_log, ssm_d, ssm_norm_g, mlp_norm_g, mlp_w_s, mlp_b_s, w_br_na, w_br_ssm, w_br_mlp, w_out, peer_w_q, peer_sub_keys, peer_u, peer_v):
    batch, seq, d = x_prompt.shape
    dec_batch, dec_seq, _ = x_sample.shape
    depth = w_in.shape[0]
    past = cache_k.shape[2]

    cvec = jnp.concatenate([c_ctx[None, :], c, jnp.zeros((8 - 1 - dec_batch, d), F32)], axis=0)
    mod_all = modulation(cvec, w_mod, b_mod)

    o_q = 0
    o_z = 3 * NA_WIDTH
    o_xbc = o_z + SSM_INNER
    o_dt = o_xbc + SSM_CONV_DIM
    o_u = o_dt + 2 * SSM_HEADS
    o_g = o_u + 2 * MLP_WIDTH
    dt_pad = jnp.zeros((depth, d, LANES - SSM_HEADS), F32)
    w_ssm = jnp.concatenate([w_in[:, :, o_z:o_dt], w_in[:, :, o_dt:o_dt + SSM_HEADS], dt_pad,
                             w_in[:, :, o_dt + SSM_HEADS:o_u], dt_pad], axis=-1).astype(BF16)
    w_qkv = w_in[:, :, o_q:o_z].astype(BF16)
    w_mlp = w_in[:, :, o_u:o_g].astype(BF16)
    w_gate = w_in[:, :, o_g:].astype(BF16)
    w_br = jnp.stack([w_br_na, w_br_ssm, w_br_mlp], axis=1).astype(BF16)
    w_out_b = w_out.astype(BF16)
    peer_wq_t = jnp.swapaxes(peer_w_q, 1, 2).astype(BF16)
    peer_keys = peer_sub_keys.reshape(depth, 2 * PEER_HEADS, PEER_KEYS, PEER_QDIM // 2).astype(BF16)
    peer_u_b = peer_u.astype(BF16)
    peer_v_t = jnp.swapaxes(peer_v, 1, 2).astype(BF16)
    dt_bias = _pad_lanes(ssm_dt_bias)
    a_log = _pad_lanes(ssm_a_log)
    d_skip = jnp.repeat(ssm_d, SSM_HEAD_DIM, axis=-1)[:, None, :]
    mlp_b = jnp.broadcast_to(mlp_b_s[..., None], mlp_b_s.shape + (MLP_CHUNK,))
    mlp_w = mlp_w_s.astype(BF16)

    cache_k4 = cache_k.reshape(dec_batch, depth, past, NA_WIDTH)
    cache_v4 = cache_v.reshape(dec_batch, depth, past, NA_WIDTH)

    xp = x_prompt.reshape(batch * seq, d)
    xs = x_sample.reshape(dec_batch * dec_seq, d)
    new_k, new_v, new_s = [], [], []
    for l in range(depth):
        lw = {
            "g_mix": g_norm_mix[l][None, :], "g_ffn": g_norm_ffn[l][None, :],
            "w_qkv": w_qkv[l], "w_ssm": w_ssm[l], "w_mlp": w_mlp[l], "w_gate": w_gate[l],
            "na_bias": _na_bias_table(na_rel_bias[l]),
            "conv_w": ssm_conv_w[l], "conv_b": ssm_conv_b[l][None, :],
            "dt_bias": dt_bias[l], "a_log": a_log[l], "d_skip": d_skip[l],
            "ssm_norm_g": ssm_norm_g[l][None, :], "mlp_norm_g": mlp_norm_g[l][None, :],
            "mlp_w_s": mlp_w[l], "mlp_b_s": mlp_b[l],
            "w_br": w_br[l], "w_out": w_out_b[l],
            "peer_wq_t": peer_wq_t[l], "peer_keys": peer_keys[l], "peer_u": peer_u_b[l], "peer_v_t": peer_v_t[l],
        }
        m = mod_all[l].reshape(8, 6, d)
        mod_ctx = [m[0:1, j][:, None, :] for j in range(6)]
        mod_lat = [m[1:1 + dec_batch, j][:, None, :] for j in range(6)]
        xp, qkv_p, fin = _layer(xp, mod_ctx, lw, l, batch, seq, None)
        xs, _, _ = _layer(xs, mod_lat, lw, l, dec_batch, dec_seq, (cache_k4, cache_v4, state_ssm))
        new_k.append(qkv_p[:, NA_WIDTH:2 * NA_WIDTH].reshape(batch, seq, NA_HEADS, NA_HEAD_DIM))
        new_v.append(qkv_p[:, 2 * NA_WIDTH:].reshape(batch, seq, NA_HEADS, NA_HEAD_DIM))
        new_s.append(fin)
    gf = g_norm_final[None, :]
    y_prompt = final_norm(xp, gf).reshape(batch, seq, d)
    y_sample = final_norm(xs, gf).reshape(dec_batch, dec_seq, d)
    return (y_prompt, y_sample, jnp.stack(new_k, axis=1), jnp.stack(new_v, axis=1), jnp.stack(new_s, axis=1))
```

```python
import functools
import math

import numpy as np
import jax
import jax.numpy as jnp
from jax import lax
from jax.experimental import pallas as pl
from jax.experimental.pallas import tpu as pltpu

F32 = jnp.float32
BF16 = jnp.bfloat16

D_MODEL = 2048
DEPTH = 4
GRID_W = 64
EPS = 1e-6
NA_HEADS = 8
NA_HEAD_DIM = 128
NA_WIDTH = NA_HEADS * NA_HEAD_DIM
NA_WIN_ROWS = 8
NA_WIN_COLS = 16
SSM_HEADS = 16
SSM_HEAD_DIM = 64
SSM_INNER = SSM_HEADS * SSM_HEAD_DIM
SSM_GROUPS = 2
SSM_STATE = 128
SSM_CHUNK = 128
SSM_BC = SSM_GROUPS * SSM_STATE
SSM_CONV_DIM = SSM_INNER + 2 * SSM_BC
MLP_GROUPS = 8
MLP_WIDTH = 1024
MLP_CHUNK = 128
PEER_HEADS = 8
PEER_KEYS = 128
PEER_EXPERTS = PEER_KEYS * PEER_KEYS
PEER_QDIM = 256
PEER_TOPK = 16
N_BRANCH = 3

LANES = 128
BF16_ROWS = 16
SUB_KEYS = 2
SSM_COLS = SSM_INNER + SSM_CONV_DIM + 2 * LANES
DT_BLOCK0 = (SSM_INNER + SSM_CONV_DIM) // LANES
VMEM_LIMIT = 60 * 1024 * 1024
NEG_BIG = -1e30


def _cparams(sem):
    return pltpu.CompilerParams(dimension_semantics=sem, vmem_limit_bytes=VMEM_LIMIT)


def _nt_dot(a, b):
    return lax.dot_general(a, b, (((1,), (1,)), ((), ())), preferred_element_type=F32)


def _tn_dot(a, b):
    return lax.dot_general(a, b, (((0,), (0,)), ((), ())), preferred_element_type=F32)


def _mod_kernel(c_ref, w_ref, b_ref, o_ref):
    c = c_ref[...]
    a = c * jax.nn.sigmoid(c)
    o_ref[...] = jnp.dot(a, w_ref[...], preferred_element_type=F32,
                         precision=lax.Precision.HIGHEST) + b_ref[...]


def modulation(cvec, w_mod, b_mod):
    depth, d, n = w_mod.shape
    tn = 1024
    return pl.pallas_call(
        _mod_kernel,
        grid=(depth, n // tn),
        in_specs=[
            pl.BlockSpec((8, d), lambda l, j: (0, 0)),
            pl.BlockSpec((None, d, tn), lambda l, j: (l, 0, j)),
            pl.BlockSpec((None, 1, tn), lambda l, j: (l, 0, j)),
        ],
        out_specs=pl.BlockSpec((None, 8, tn), lambda l, j: (l, 0, j)),
        out_shape=jax.ShapeDtypeStruct((depth, 8, n), F32),
        compiler_params=_cparams(("parallel", "parallel")),
        name="modulation",
    )(cvec, w_mod, b_mod.reshape(depth, 1, n))


def _norm_mod_kernel(x_ref, g_ref, sc_ref, sh_ref, o_ref, *, transpose):
    x = x_ref[...]
    y = x * lax.rsqrt(jnp.mean(x * x, axis=-1, keepdims=True) + EPS) * g_ref[...]
    h = y * (1.0 + sc_ref[...]) + sh_ref[...]
    if transpose:
        h = h.T
    o_ref[...] = h.astype(o_ref.dtype)


def norm_mod(x, g, sc, sh, seg_len, transpose=False, tm=512):
    t, d = x.shape
    spt = seg_len // tm
    seg_spec = pl.BlockSpec((None, 1, d), lambda i: (i // spt, 0, 0))
    if transpose:
        out_spec = pl.BlockSpec((d, tm), lambda i: (0, i))
        out_shape = jax.ShapeDtypeStruct((d, t), BF16)
    else:
        out_spec = pl.BlockSpec((tm, d), lambda i: (i, 0))
        out_shape = jax.ShapeDtypeStruct((t, d), BF16)
    return pl.pallas_call(
        functools.partial(_norm_mod_kernel, transpose=transpose),
        grid=(t // tm,),
        in_specs=[pl.BlockSpec((tm, d), lambda i: (i, 0)),
                  pl.BlockSpec((1, d), lambda i: (0, 0)), seg_spec, seg_spec],
        out_specs=out_spec,
        out_shape=out_shape,
        compiler_params=_cparams(("parallel",)),
        name="norm_mod_t" if transpose else "norm_mod",
    )(x, g, sc, sh)


def _final_norm_kernel(x_ref, g_ref, o_ref):
    x = x_ref[...]
    o_ref[...] = x * lax.rsqrt(jnp.mean(x * x, axis=-1, keepdims=True) + EPS) * g_ref[...]


def final_norm(x, g, tm=512):
    t, d = x.shape
    return pl.pallas_call(
        _final_norm_kernel,
        grid=(t // tm,),
        in_specs=[pl.BlockSpec((tm, d), lambda i: (i, 0)), pl.BlockSpec((1, d), lambda i: (0, 0))],
        out_specs=pl.BlockSpec((tm, d), lambda i: (i, 0)),
        out_shape=jax.ShapeDtypeStruct((t, d), F32),
        compiler_params=_cparams(("parallel",)),
        name="final_norm",
    )(x, g)


def _mm_kernel(a_ref, b_ref, o_ref):
    o_ref[...] = jnp.dot(a_ref[...], b_ref[...], preferred_element_type=F32).astype(o_ref.dtype)


def matmul(a, b, tm, tn, out_dtype=F32, name="matmul"):
    m, k = a.shape
    n = b.shape[1]
    tm = min(tm, m)
    return pl.pallas_call(
        _mm_kernel,
        grid=(m // tm, n // tn),
        in_specs=[pl.BlockSpec((tm, k), lambda i, j: (i, 0)),
                  pl.BlockSpec((k, tn), lambda i, j: (0, j))],
        out_specs=pl.BlockSpec((tm, tn), lambda i, j: (i, j)),
        out_shape=jax.ShapeDtypeStruct((m, n), out_dtype),
        compiler_params=_cparams(("parallel", "parallel")),
        name=name,
    )(a, b)


def _ctx_attn_kernel(q_ref, k_ref, v_ref, o_ref):
    scale = NA_HEAD_DIM ** -0.5
    q = q_ref[...].astype(BF16)
    k = k_ref[...].astype(BF16)
    v = v_ref[...].astype(BF16)
    s = _nt_dot(q, k) * scale
    m = jnp.max(s, axis=-1, keepdims=True)
    p = jnp.exp(s - m)
    p = p / jnp.sum(p, axis=-1, keepdims=True)
    o_ref[...] = jnp.dot(p.astype(BF16), v, preferred_element_type=F32).astype(o_ref.dtype)


def context_attention(qkv, nseq, seq):
    hd = NA_HEAD_DIM
    return pl.pallas_call(
        _ctx_attn_kernel,
        grid=(nseq, NA_HEADS),
        in_specs=[pl.BlockSpec((seq, hd), lambda b, h: (b, h)),
                  pl.BlockSpec((seq, hd), lambda b, h: (b, NA_HEADS + h)),
                  pl.BlockSpec((seq, hd), lambda b, h: (b, 2 * NA_HEADS + h))],
        out_specs=pl.BlockSpec((seq, hd), lambda b, h: (b, h)),
        out_shape=jax.ShapeDtypeStruct((nseq * seq, NA_WIDTH), BF16),
        compiler_params=_cparams(("parallel", "parallel")),
        name="ctx_attention",
    )(qkv, qkv, qkv)


def _na_kernel(q_ref, k_ref, v_ref, kc_ref, vc_ref, bias_ref, o_ref, kb_scr, vb_scr, *, rows):
    scale = NA_HEAD_DIM ** -0.5
    band = NA_WIN_ROWS * GRID_W
    kb_scr[...] = k_ref[...].astype(BF16)
    vb_scr[...] = v_ref[...].astype(BF16)
    kc = kc_ref[...].astype(BF16)
    vc = vc_ref[...].astype(BF16)

    def row_step(r, carry):
        rs = jnp.clip(r - NA_WIN_ROWS // 2, 0, rows - NA_WIN_ROWS)
        q = q_ref[pl.ds(pl.multiple_of(r * GRID_W, GRID_W), GRID_W), :].astype(BF16)
        k0 = pl.multiple_of(rs * GRID_W, GRID_W)
        kb = kb_scr[pl.ds(k0, band), :]
        vb = vb_scr[pl.ds(k0, band), :]
        s_win = _nt_dot(q, kb) * scale + bias_ref[r - rs]
        s_ctx = _nt_dot(q, kc) * scale
        m = jnp.maximum(jnp.max(s_win, axis=-1, keepdims=True), jnp.max(s_ctx, axis=-1, keepdims=True))
        p_win = jnp.exp(s_win - m)
        p_ctx = jnp.exp(s_ctx - m)
        den = jnp.sum(p_win, axis=-1, keepdims=True) + jnp.sum(p_ctx, axis=-1, keepdims=True)
        o = (jnp.dot((p_win / den).astype(BF16), vb, preferred_element_type=F32)
             + jnp.dot((p_ctx / den).astype(BF16), vc, preferred_element_type=F32))
        o_ref[pl.ds(pl.multiple_of(r * GRID_W, GRID_W), GRID_W), :] = o.astype(o_ref.dtype)
        return carry

    lax.fori_loop(0, rows, row_step, 0)


def _na_bias_table(rel_bias):
    col = np.arange(GRID_W)
    cstart = np.clip(col - NA_WIN_COLS // 2, 0, GRID_W - NA_WIN_COLS)
    valid = (col[None, :] >= cstart[:, None]) & (col[None, :] < cstart[:, None] + NA_WIN_COLS)
    dcol = col[None, :] - col[:, None] + NA_WIN_COLS - 1
    onehot = ((np.arange(2 * NA_WIN_COLS - 1)[:, None, None] == dcol[None]) & valid[None]).astype(np.float32)
    tabc = jnp.einsum("hrk,kcd->hrcd", rel_bias, jnp.asarray(onehot), precision=lax.Precision.HIGHEST)
    tabc = tabc + jnp.asarray(np.where(valid, 0.0, NEG_BIG).astype(np.float32))
    wr = NA_WIN_ROWS
    tab = jnp.stack([tabc[:, wr - 1 - o:2 * wr - 1 - o].transpose(0, 2, 1, 3) for o in range(wr)], axis=1)
    return tab.reshape(rel_bias.shape[0], wr, GRID_W, wr * GRID_W).astype(F32)


def neighbourhood_attention(qkv, cache_k, cache_v, layer, bias_tab, nseq, seq):
    hd = NA_HEAD_DIM
    past = cache_k.shape[2]
    rows = seq // GRID_W
    band = NA_WIN_ROWS * GRID_W
    return pl.pallas_call(
        functools.partial(_na_kernel, rows=rows),
        grid=(nseq, NA_HEADS),
        in_specs=[pl.BlockSpec((seq, hd), lambda b, h: (b, h)),
                  pl.BlockSpec((seq, hd), lambda b, h: (b, NA_HEADS + h)),
                  pl.BlockSpec((seq, hd), lambda b, h: (b, 2 * NA_HEADS + h)),
                  pl.BlockSpec((None, None, past, hd), lambda b, h: (b, layer, 0, h)),
                  pl.BlockSpec((None, None, past, hd), lambda b, h: (b, layer, 0, h)),
                  pl.BlockSpec((None, NA_WIN_ROWS, GRID_W, band), lambda b, h: (h, 0, 0, 0))],
        out_specs=pl.BlockSpec((seq, hd), lambda b, h: (b, h)),
        out_shape=jax.ShapeDtypeStruct((nseq * seq, NA_WIDTH), BF16),
        scratch_shapes=[pltpu.VMEM((seq, hd), BF16), pltpu.VMEM((seq, hd), BF16)],
        compiler_params=_cparams(("parallel", "parallel")),
        name="nbr_attention",
    )(qkv, qkv, qkv, cache_k, cache_v, bias_tab)


def _conv_kernel(x_ref, w_ref, b_ref, o_ref):
    x = x_ref[...]
    n = x.shape[0]
    row = lax.broadcasted_iota(jnp.int32, x.shape, 0)
    xm2 = jnp.where(row >= 2, pltpu.roll(x, 2, 0), 0.0)
    xm1 = jnp.where(row >= 1, pltpu.roll(x, 1, 0), 0.0)
    xp1 = jnp.where(row < n - 1, pltpu.roll(x, n - 1, 0), 0.0)
    w = w_ref[...]
    y = w[0:1] * xm2 + w[1:2] * xm1 + w[2:3] * x + w[3:4] * xp1 + b_ref[...]
    o_ref[...] = y * jax.nn.sigmoid(y)


def conv_silu(ssm, conv_w, conv_b, nseq, seq):
    tc = 256
    c0 = SSM_INNER // tc
    return pl.pallas_call(
        _conv_kernel,
        grid=(nseq, SSM_CONV_DIM // tc),
        in_specs=[pl.BlockSpec((None, seq, tc), lambda s, j: (s, 0, c0 + j)),
                  pl.BlockSpec((4, tc), lambda s, j: (0, j)),
                  pl.BlockSpec((1, tc), lambda s, j: (0, j))],
        out_specs=pl.BlockSpec((None, seq, tc), lambda s, j: (s, 0, j)),
        out_shape=jax.ShapeDtypeStruct((nseq, seq, SSM_CONV_DIM), F32),
        compiler_params=_cparams(("parallel", "parallel")),
        name="conv_silu",
    )(ssm.reshape(nseq, seq, SSM_COLS), conv_w, conv_b)


def _ssd_kernel(*refs, has_init, nchunk):
    if has_init:
        xc_ref, dt_ref, tri_ref, dtb_ref, alog_ref, init_ref, y_ref, fin_ref, st_scr = refs
    else:
        xc_ref, dt_ref, tri_ref, dtb_ref, alog_ref, y_ref, fin_ref, st_scr = refs
        init_ref = None
    c = pl.program_id(2)

    @pl.when(c == 0)
    def _():
        if has_init:
            st_scr[...] = init_ref[...]
        else:
            st_scr[...] = jnp.zeros_like(st_scr)

    p = SSM_HEAD_DIM
    n = SSM_STATE
    xc = xc_ref[...]
    dt = jax.nn.softplus(dt_ref[...] + dtb_ref[...])
    a = dt * (-jnp.exp(alog_ref[...]))
    tri = tri_ref[...]
    mask = tri > 0.5
    cum = jnp.dot(tri, a, preferred_element_type=F32, precision=lax.Precision.HIGHEST)
    cum_t = cum.T
    tot = jnp.min(cum, axis=0, keepdims=True)
    e_cum = jnp.exp(cum)
    e_end = jnp.exp(tot - cum)
    e_tot = jnp.exp(tot)
    rep = SSM_HEADS // SSM_GROUPS
    gmat = []
    bmat = []
    cmat = []
    for g in range(SSM_GROUPS):
        b_g = xc[:, SSM_INNER + g * n:SSM_INNER + (g + 1) * n]
        c_g = xc[:, SSM_INNER + SSM_BC + g * n:SSM_INNER + SSM_BC + (g + 1) * n]
        bmat.append(b_g)
        cmat.append(c_g.astype(BF16))
        gmat.append(_nt_dot(c_g.astype(BF16), b_g.astype(BF16)))
    for h in range(SSM_HEADS):
        g = h // rep
        seg = cum[:, h:h + 1] - cum_t[h:h + 1, :]
        decay = jnp.exp(jnp.where(mask, seg, -jnp.inf))
        xd = (xc[:, h * p:(h + 1) * p] * dt[:, h:h + 1]).astype(BF16)
        y_diag = jnp.dot((gmat[g] * decay).astype(BF16), xd, preferred_element_type=F32)
        b_dec = (bmat[g] * e_end[:, h:h + 1]).astype(BF16)
        chunk_state = _tn_dot(xd, b_dec)
        prev = st_scr[h]
        y_off = _nt_dot(cmat[g], prev.astype(BF16)) * e_cum[:, h:h + 1]
        y_ref[:, h * p:(h + 1) * p] = y_diag + y_off
        st_scr[h] = prev * e_tot[:, h:h + 1] + chunk_state

    @pl.when(c == nchunk - 1)
    def _():
        fin_ref[...] = st_scr[...]


def ssd_scan(xc, ssm, dt_bias, a_log, init, layer):
    nseq, seq, _ = xc.shape
    q = SSM_CHUNK
    nc = seq // q
    idx = np.arange(q)
    tri = jnp.asarray(np.stack([idx[None, :] <= idx[:, None], idx[None, :] >= idx[:, None]]).astype(np.float32))

    def cc(d, c):
        return c + d * (nc - 1 - 2 * c)

    in_specs = [pl.BlockSpec((None, q, SSM_CONV_DIM), lambda d, s, c: (s, cc(d, c), 0)),
                pl.BlockSpec((None, q, LANES), lambda d, s, c: (s, cc(d, c), DT_BLOCK0 + d)),
                pl.BlockSpec((None, q, q), lambda d, s, c: (d, 0, 0)),
                pl.BlockSpec((None, 1, LANES), lambda d, s, c: (d, 0, 0)),
                pl.BlockSpec((None, 1, LANES), lambda d, s, c: (d, 0, 0))]
    args = [xc, ssm.reshape(nseq, seq, SSM_COLS), tri, dt_bias, a_log]
    if init is not None:
        in_specs.append(pl.BlockSpec((None, None, None, SSM_HEADS, SSM_HEAD_DIM, SSM_STATE),
                                     lambda d, s, c: (s, layer, d, 0, 0, 0)))
        args.append(init)
    return pl.pallas_call(
        functools.partial(_ssd_kernel, has_init=init is not None, nchunk=nc),
        grid=(2, nseq, nc),
        in_specs=in_specs,
        out_specs=[pl.BlockSpec((None, None, q, SSM_INNER), lambda d, s, c: (d, s, cc(d, c), 0)),
                   pl.BlockSpec((None, None, SSM_HEADS, SSM_HEAD_DIM, SSM_STATE), lambda d, s, c: (s, d, 0, 0, 0))],
        out_shape=[jax.ShapeDtypeStruct((2, nseq, seq, SSM_INNER), F32),
                   jax.ShapeDtypeStruct((nseq, 2, SSM_HEADS, SSM_HEAD_DIM, SSM_STATE), F32)],
        scratch_shapes=[pltpu.VMEM((SSM_HEADS, SSM_HEAD_DIM, SSM_STATE), F32)],
        compiler_params=_cparams(("parallel", "parallel", "arbitrary")),
        name="ssd_scan",
    )(*args)


def _ssm_finish_kernel(y_ref, xs_ref, z_ref, dsk_ref, g_ref, o_ref):
    z = z_ref[...]
    y = (y_ref[0] + y_ref[1] + dsk_ref[...] * xs_ref[...]) * (z * jax.nn.sigmoid(z))
    y = y * lax.rsqrt(jnp.mean(y * y, axis=-1, keepdims=True) + EPS) * g_ref[...]
    o_ref[...] = y.astype(o_ref.dtype)


def ssm_finish(y, xc, ssm, d_skip, g, tm=512):
    t = ssm.shape[0]
    w = SSM_INNER
    return pl.pallas_call(
        _ssm_finish_kernel,
        grid=(t // tm,),
        in_specs=[pl.BlockSpec((2, tm, w), lambda i: (0, i, 0)),
                  pl.BlockSpec((tm, w), lambda i: (i, 0)),
                  pl.BlockSpec((tm, w), lambda i: (i, 0)),
                  pl.BlockSpec((1, w), lambda i: (0, 0)),
                  pl.BlockSpec((1, w), lambda i: (0, 0))],
        out_specs=pl.BlockSpec((tm, w), lambda i: (i, 0)),
        out_shape=jax.ShapeDtypeStruct((t, w), BF16),
        compiler_params=_cparams(("parallel",)),
        name="ssm_finish",
    )(y, xc, ssm, d_skip, g)


def _gmlp_kernel(u_ref, v_ref, g_ref, w_ref, b_ref, o_ref):
    v = jax.nn.gelu(v_ref[...])
    v = (v * lax.rsqrt(jnp.mean(v * v, axis=-1, keepdims=True) + EPS) * g_ref[...]).astype(BF16)
    q = MLP_CHUNK
    gw = MLP_WIDTH // MLP_GROUPS
    for ch in range(v.shape[0] // q):
        for g in range(MLP_GROUPS):
            mixed = jnp.dot(w_ref[g], v[ch * q:(ch + 1) * q, g * gw:(g + 1) * gw],
                            preferred_element_type=F32) + b_ref[g]
            u = jax.nn.gelu(u_ref[ch * q:(ch + 1) * q, g * gw:(g + 1) * gw])
            o_ref[ch * q:(ch + 1) * q, g * gw:(g + 1) * gw] = (u * mixed).astype(o_ref.dtype)


def chunk_mlp(mlp, g, w_s, b_s, tm=512):
    t = mlp.shape[0]
    w = MLP_WIDTH
    return pl.pallas_call(
        _gmlp_kernel,
        grid=(t // tm,),
        in_specs=[pl.BlockSpec((tm, w), lambda i: (i, 0)),
                  pl.BlockSpec((tm, w), lambda i: (i, 1)),
                  pl.BlockSpec((1, w), lambda i: (0, 0)),
                  pl.BlockSpec(w_s.shape, lambda i: (0, 0, 0)),
                  pl.BlockSpec(b_s.shape, lambda i: (0, 0, 0))],
        out_specs=pl.BlockSpec((tm, w), lambda i: (i, 0)),
        out_shape=jax.ShapeDtypeStruct((t, w), BF16),
        compiler_params=_cparams(("parallel",)),
        name="chunk_mlp",
    )(mlp, mlp, g, w_s, b_s)


def _merge_kernel(a0_ref, a1_ref, a2_ref, w_ref, g0_ref, g1_ref, g2_ref, o_ref):
    acc = jax.nn.sigmoid(g0_ref[...]) * jnp.dot(a0_ref[...], w_ref[0], preferred_element_type=F32)
    acc += jax.nn.sigmoid(g1_ref[...]) * jnp.dot(a1_ref[...], w_ref[1], preferred_element_type=F32)
    acc += jax.nn.sigmoid(g2_ref[...]) * jnp.dot(a2_ref[...], w_ref[2], preferred_element_type=F32)
    o_ref[...] = acc.astype(o_ref.dtype)


def merge_branches(o_na, o_ssm, o_mlp, w_br, gates, tm=512, tn=512):
    t, k = o_na.shape
    d = w_br.shape[2]
    nb = d // tn
    a_spec = pl.BlockSpec((tm, k), lambda i, j: (i, 0))
    return pl.pallas_call(
        _merge_kernel,
        grid=(t // tm, nb),
        in_specs=[a_spec, a_spec, a_spec,
                  pl.BlockSpec((N_BRANCH, k, tn), lambda i, j: (0, 0, j)),
                  pl.BlockSpec((tm, tn), lambda i, j: (i, j)),
                  pl.BlockSpec((tm, tn), lambda i, j: (i, nb + j)),
                  pl.BlockSpec((tm, tn), lambda i, j: (i, 2 * nb + j))],
        out_specs=pl.BlockSpec((tm, tn), lambda i, j: (i, j)),
        out_shape=jax.ShapeDtypeStruct((t, d), BF16),
        compiler_params=_cparams(("parallel", "parallel")),
        name="merge_branches",
    )(o_na, o_ssm, o_mlp, w_br, gates, gates, gates)


def _out_proj_kernel(a_ref, w_ref, x_ref, gt_ref, o_ref):
    o_ref[...] = x_ref[...] + gt_ref[...] * jnp.dot(a_ref[...], w_ref[...], preferred_element_type=F32)


def out_proj(merged, w_out, x, gt, seg_len, tm=512, tn=512):
    t, k = merged.shape
    d = w_out.shape[1]
    spt = seg_len // tm
    return pl.pallas_call(
        _out_proj_kernel,
        grid=(t // tm, d // tn),
        in_specs=[pl.BlockSpec((tm, k), lambda i, j: (i, 0)),
                  pl.BlockSpec((k, tn), lambda i, j: (0, j)),
                  pl.BlockSpec((tm, tn), lambda i, j: (i, j)),
                  pl.BlockSpec((None, 1, tn), lambda i, j: (i // spt, 0, j))],
        out_specs=pl.BlockSpec((tm, tn), lambda i, j: (i, j)),
        out_shape=jax.ShapeDtypeStruct((t, d), F32),
        compiler_params=_cparams(("parallel", "parallel")),
        name="out_proj",
    )(merged, w_out, x, gt)


def _peer_scores_kernel(wq_ref, h_ref, key_ref, o_ref):
    qv = jnp.dot(wq_ref[...], h_ref[...], preferred_element_type=F32).astype(BF16)
    dq = PEER_QDIM // 2
    for hk in range(2 * PEER_HEADS):
        o_ref[hk * PEER_KEYS:(hk + 1) * PEER_KEYS, :] = jnp.dot(
            key_ref[hk], qv[hk * dq:(hk + 1) * dq, :], preferred_element_type=F32)


def peer_scores(wq_t, h_t, keys, tl=512):
    d, t = h_t.shape
    rows = 2 * PEER_HEADS * PEER_KEYS
    return pl.pallas_call(
        _peer_scores_kernel,
        grid=(t // tl,),
        in_specs=[pl.BlockSpec(wq_t.shape, lambda i: (0, 0)),
                  pl.BlockSpec((d, tl), lambda i: (0, i)),
                  pl.BlockSpec(keys.shape, lambda i: (0, 0, 0))],
        out_specs=pl.BlockSpec((rows, tl), lambda i: (0, i)),
        out_shape=jax.ShapeDtypeStruct((rows, t), F32),
        compiler_params=_cparams(("parallel",)),
        name="peer_scores",
    )(wq_t, h_t, keys)


def _top16(s):
    nk = s.shape[0]
    kio = lax.broadcasted_iota(jnp.int32, s.shape, 0).astype(F32)
    rank = jnp.full(s.shape, float(PEER_TOPK), F32)
    vals = []
    for a in range(PEER_TOPK):
        m = jnp.max(s, axis=0, keepdims=True)
        idx = jnp.min(jnp.where(s == m, kio, float(nk)), axis=0, keepdims=True)
        hit = kio == idx
        rank = jnp.where(hit, float(a), rank)
        s = jnp.where(hit, -jnp.inf, s)
        vals.append(m)
    return jnp.concatenate(vals, axis=0), rank


def _peer_select_kernel(s_ref, rank2_ref, cnt_ref, e1_ref, e2_ref):
    nk = PEER_KEYS
    k = PEER_TOPK
    for h in range(PEER_HEADS):
        s1 = s_ref[(2 * h) * nk:(2 * h + 1) * nk, :]
        s2 = s_ref[(2 * h + 1) * nk:(2 * h + 2) * nk, :]
        tv1, rank1 = _top16(s1)
        tv2, rank2 = _top16(s2)
        sub = lax.broadcasted_iota(jnp.int32, (8, s1.shape[1]), 0).astype(F32)
        pieces = []
        pos_pieces = []
        for a in range(k // 2):
            nb = k // (a + 1)
            for b0 in range(0, nb, 8):
                vals = tv1[a:a + 1, :] + tv2[b0:b0 + 8, :]
                pieces.append(vals if nb - b0 >= 8 else jnp.where(sub < float(nb - b0), vals, -jnp.inf))
                pos_pieces.append(sub + float(a * k + b0))
        pieces.append(tv1[k // 2:k, :] + tv2[0:1, :])
        pos_pieces.append((sub + float(k // 2)) * float(k))
        cand = jnp.concatenate(pieces, axis=0)
        pio = jnp.concatenate(pos_pieces, axis=0)
        aio = lax.broadcasted_iota(jnp.int32, (k, cand.shape[1]), 0).astype(F32)
        cnt_a = jnp.zeros((k, cand.shape[1]), F32)
        zsum = jnp.zeros((1, cand.shape[1]), F32)
        top = tv1[0:1, :] + tv2[0:1, :]
        for _ in range(k):
            m = jnp.max(cand, axis=0, keepdims=True)
            pos = jnp.min(jnp.where(cand == m, pio, float(k * k)), axis=0, keepdims=True)
            cand = jnp.where(pio == pos, -jnp.inf, cand)
            cnt_a = cnt_a + jnp.where(aio == jnp.floor(pos * (1.0 / k)), 1.0, 0.0)
            zsum = zsum + jnp.exp(m - top)
        cnt = jnp.zeros(s1.shape, F32)
        for a in range(k):
            cnt = jnp.where(rank1 == float(a), cnt_a[a:a + 1, :], cnt)
        rank2_ref[h * nk:(h + 1) * nk, :] = rank2.astype(rank2_ref.dtype)
        cnt_ref[h * nk:(h + 1) * nk, :] = cnt
        e1_ref[h * nk:(h + 1) * nk, :] = jnp.exp(s1 - tv1[0:1, :])
        e2_ref[h * nk:(h + 1) * nk, :] = (jnp.exp(s2 - tv2[0:1, :]) / zsum).astype(e2_ref.dtype)


def peer_select(scores, tl=128):
    rows, t = scores.shape
    half = rows // 2
    spec = pl.BlockSpec((half, tl), lambda i: (0, i))
    shp = jax.ShapeDtypeStruct((half, t), F32)
    shp_b = jax.ShapeDtypeStruct((half, t), BF16)
    return pl.pallas_call(
        _peer_select_kernel,
        grid=(t // tl,),
        in_specs=[pl.BlockSpec((rows, tl), lambda i: (0, i))],
        out_specs=[spec, spec, spec, spec],
        out_shape=[shp_b, shp, shp, shp_b],
        compiler_params=_cparams(("parallel",)),
        name="peer_select",
    )(scores)


def _peer_dense_kernel(h_ref, u_ref, vta_ref, vtb_ref, rank2_ref, cnt_ref, e1_ref, e2_ref, o_ref, za_scr, zb_scr,
                       *, nstep, rows_per_half, lane_tile):
    e = pl.program_id(1)

    @pl.when(e == 0)
    def _():
        o_ref[...] = jnp.zeros_like(o_ref)
        zb_scr[...] = jnp.zeros_like(zb_scr)

    e_cur = jnp.minimum(e, nstep - 1)
    live = jnp.where(e < nstep, 1.0, 0.0)
    nk = PEER_KEYS
    nsub = rows_per_half // SUB_KEYS
    mrows = o_ref.shape[0] // nsub
    lanes = [slice(lt * lane_tile, (lt + 1) * lane_tile) for lt in range(o_ref.shape[1] // lane_tile)]
    for half, (z_new, z_old, vt_ref) in enumerate(((za_scr, zb_scr, vta_ref), (zb_scr, za_scr, vtb_ref))):
        for sc in range(nsub):
            i0 = half * rows_per_half + sc * SUB_KEYS
            acts = [jax.nn.gelu(jnp.dot(u_ref[i0 * nk:(i0 + SUB_KEYS) * nk, :], h_ref[:, ls],
                                        preferred_element_type=F32)) for ls in lanes]
            for ls in lanes:
                o_ref[sc * mrows:(sc + 1) * mrows, ls] += jnp.dot(
                    vt_ref[sc * mrows:(sc + 1) * mrows, :], z_old[:, ls], preferred_element_type=F32)
            for ls, act in zip(lanes, acts):
                for k in range(SUB_KEYS):
                    w = None
                    for h in range(PEER_HEADS):
                        row = h * nk + e_cur * (2 * rows_per_half) + i0 + k
                        cnt_b = jnp.broadcast_to(cnt_ref[pl.ds(row, 1), ls] * live, (BF16_ROWS, lane_tile))
                        e1_b = jnp.broadcast_to(e1_ref[pl.ds(row, 1), ls], (BF16_ROWS, lane_tile))
                        cnt_b = jnp.concatenate([cnt_b.astype(BF16)] * (nk // BF16_ROWS), axis=0)
                        e1_b = jnp.concatenate([e1_b.astype(BF16)] * (nk // BF16_ROWS), axis=0)
                        sel = rank2_ref[h * nk:(h + 1) * nk, ls] < cnt_b
                        term = jnp.where(sel, e2_ref[h * nk:(h + 1) * nk, ls], jnp.zeros((), BF16)) * e1_b
                        w = term if w is None else w + term
                    r0 = (sc * SUB_KEYS + k) * nk
                    z_new[r0:r0 + nk, ls] = w * act[k * nk:(k + 1) * nk, :].astype(BF16)


def peer_dense(h_t, u_tab, v_t, rank2, cnt, e1, e2, tm=512, et=1024, lane_tile=256):
    d, t = h_t.shape
    ne = u_tab.shape[0]
    nstep = ne // et
    eh = et // 2
    map_spec = pl.BlockSpec((rank2.shape[0], tm), lambda i, e: (0, i))
    return pl.pallas_call(
        functools.partial(_peer_dense_kernel, nstep=nstep, rows_per_half=eh // PEER_KEYS, lane_tile=lane_tile),
        grid=(t // tm, nstep + 1),
        in_specs=[pl.BlockSpec((d, tm), lambda i, e: (0, i)),
                  pl.BlockSpec((et, d), lambda i, e: (jnp.minimum(e, nstep - 1), 0)),
                  pl.BlockSpec((d, eh), lambda i, e: (0, jnp.maximum(2 * e - 1, 0))),
                  pl.BlockSpec((d, eh), lambda i, e: (0, jnp.minimum(2 * e, 2 * nstep - 1))),
                  map_spec, map_spec, map_spec, map_spec],
        out_specs=pl.BlockSpec((d, tm), lambda i, e: (0, i)),
        out_shape=jax.ShapeDtypeStruct((d, t), F32),
        scratch_shapes=[pltpu.VMEM((eh, tm), BF16), pltpu.VMEM((eh, tm), BF16)],
        compiler_params=_cparams(("parallel", "arbitrary")),
        name="peer_dense",
    )(h_t, u_tab, v_t, v_t, rank2, cnt, e1, e2)


def _peer_resid_kernel(x_ref, p_ref, gt_ref, o_ref):
    o_ref[...] = x_ref[...] + gt_ref[...] * p_ref[...].T


def peer_residual(x, peer_t, gt, seg_len, tm=512):
    t, d = x.shape
    spt = seg_len // tm
    return pl.pallas_call(
        _peer_resid_kernel,
        grid=(t // tm,),
        in_specs=[pl.BlockSpec((tm, d), lambda i: (i, 0)),
                  pl.BlockSpec((d, tm), lambda i: (0, i)),
                  pl.BlockSpec((None, 1, d), lambda i: (i // spt, 0, 0))],
        out_specs=pl.BlockSpec((tm, d), lambda i: (i, 0)),
        out_shape=jax.ShapeDtypeStruct((t, d), F32),
        compiler_params=_cparams(("parallel",)),
        name="peer_residual",
    )(x, peer_t, gt)


def _layer(x, mod, lw, layer, nseq, seq, ctx):
    sh1, sc1, gt1, sh2, sc2, gt2 = mod
    seg_len = x.shape[0] // sh1.shape[0]
    h = norm_mod(x, lw["g_mix"], sc1, sh1, seg_len)
    qkv = matmul(h, lw["w_qkv"], 1024, 512, name="proj_qkv")
    ssm = matmul(h, lw["w_ssm"], 1024, 128 * 11, name="proj_ssm")
    mlp = matmul(h, lw["w_mlp"], 1024, 512, name="proj_mlp")
    gates = matmul(h, lw["w_gate"], 1024, 512, name="proj_gate")
    if ctx is None:
        o_na = context_attention(qkv, nseq, seq)
        init = None
    else:
        cache_k, cache_v, init = ctx
        o_na = neighbourhood_attention(qkv, cache_k, cache_v, layer, lw["na_bias"], nseq, seq)
    xc = conv_silu(ssm, lw["conv_w"], lw["conv_b"], nseq, seq)
    y, fin = ssd_scan(xc, ssm, lw["dt_bias"], lw["a_log"], init, layer)
    t = nseq * seq
    o_ssm = ssm_finish(y.reshape(2, t, SSM_INNER), xc.reshape(t, SSM_CONV_DIM), ssm, lw["d_skip"], lw["ssm_norm_g"])
    o_mlp = chunk_mlp(mlp, lw["mlp_norm_g"], lw["mlp_w_s"], lw["mlp_b_s"])
    merged = merge_branches(o_na, o_ssm, o_mlp, lw["w_br"], gates)
    x = out_proj(merged, lw["w_out"], x, gt1, seg_len)
    h2t = norm_mod(x, lw["g_ffn"], sc2, sh2, seg_len, transpose=True)
    scores = peer_scores(lw["peer_wq_t"], h2t, lw["peer_keys"])
    rank2, cnt, e1, e2 = peer_select(scores)
    peer_t = peer_dense(h2t, lw["peer_u"], lw["peer_v_t"], rank2, cnt, e1, e2)
    x = peer_residual(x, peer_t, gt2, seg_len)
    return x, qkv, fin


def _pad_lanes(v):
    pad = LANES - v.shape[-1]
    return jnp.pad(v, [(0, 0)] * (v.ndim - 1) + [(0, pad)])[..., None, :]


def kernel(x_prompt, x_sample, cache_k, cache_v, state_ssm, c, c_ctx, w_mod, b_mod, g_norm_mix, g_norm_ffn, g_norm_final, w_in, na_rel_bias, ssm_conv_w, ssm_conv_b, ssm_dt_bias, ssm_a_log, ssm_d, ssm_norm_g, mlp_norm_g, mlp_w_s, mlp_b_s, w_br_na, w_br_ssm, w_br_mlp, w_out, peer_w_q, peer_sub_keys, peer_u, peer_v):
    batch, seq, d = x_prompt.shape
    dec_batch, dec_seq, _ = x_sample.shape
    depth = w_in.shape[0]
    past = cache_k.shape[2]

    cvec = jnp.concatenate([c_ctx[None, :], c, jnp.zeros((8 - 1 - dec_batch, d), F32)], axis=0)
    mod_all = modulation(cvec, w_mod, b_mod)

    o_q = 0
    o_z = 3 * NA_WIDTH
    o_xbc = o_z + SSM_INNER
    o_dt = o_xbc + SSM_CONV_DIM
    o_u = o_dt + 2 * SSM_HEADS
    o_g = o_u + 2 * MLP_WIDTH
    dt_pad = jnp.zeros((depth, d, LANES - SSM_HEADS), F32)
    w_ssm = jnp.concatenate([w_in[:, :, o_z:o_dt], w_in[:, :, o_dt:o_dt + SSM_HEADS], dt_pad,
                             w_in[:, :, o_dt + SSM_HEADS:o_u], dt_pad], axis=-1).astype(BF16)
    w_qkv = w_in[:, :, o_q:o_z].astype(BF16)
    w_mlp = w_in[:, :, o_u:o_g].astype(BF16)
    w_gate = w_in[:, :, o_g:].astype(BF16)
    w_br = jnp.stack([w_br_na, w_br_ssm, w_br_mlp], axis=1).astype(BF16)
    w_out_b = w_out.astype(BF16)
    peer_wq_t = jnp.swapaxes(peer_w_q, 1, 2).astype(BF16)
    peer_keys = peer_sub_keys.reshape(depth, 2 * PEER_HEADS, PEER_KEYS, PEER_QDIM // 2).astype(BF16)
    peer_u_b = peer_u.astype(BF16)
    peer_v_t = jnp.swapaxes(peer_v, 1, 2).astype(BF16)
    dt_bias = _pad_lanes(ssm_dt_bias)
    a_log = _pad_lanes(ssm_a_log)
    d_skip = jnp.repeat(ssm_d, SSM_HEAD_DIM, axis=-1)[:, None, :]
    mlp_b = jnp.broadcast_to(mlp_b_s[..., None], mlp_b_s.shape + (MLP_CHUNK,))
    mlp_w = mlp_w_s.astype(BF16)

    cache_k4 = cache_k.reshape(dec_batch, depth, past, NA_WIDTH)
    cache_v4 = cache_v.reshape(dec_batch, depth, past, NA_WIDTH)

    xp = x_prompt.reshape(batch * seq, d)
    xs = x_sample.reshape(dec_batch * dec_seq, d)
    new_k, new_v, new_s = [], [], []
    for l in range(depth):
        lw = {
            "g_mix": g_norm_mix[l][None, :], "g_ffn": g_norm_ffn[l][None, :],
            "w_qkv": w_qkv[l], "w_ssm": w_ssm[l], "w_mlp": w_mlp[l], "w_gate": w_gate[l],
            "na_bias": _na_bias_table(na_rel_bias[l]),
            "conv_w": ssm_conv_w[l], "conv_b": ssm_conv_b[l][None, :],
            "dt_bias": dt_bias[l], "a_log": a_log[l], "d_skip": d_skip[l],
            "ssm_norm_g": ssm_norm_g[l][None, :], "mlp_norm_g": mlp_norm_g[l][None, :],
            "mlp_w_s": mlp_w[l], "mlp_b_s": mlp_b[l],
            "w_br": w_br[l], "w_out": w_out_b[l],
            "peer_wq_t": peer_wq_t[l], "peer_keys": peer_keys[l], "peer_u": peer_u_b[l], "peer_v_t": peer_v_t[l],
        }
        m = mod_all[l].reshape(8, 6, d)
        mod_ctx = [m[0:1, j][:, None, :] for j in range(6)]
        mod_lat = [m[1:1 + dec_batch, j][:, None, :] for j in range(6)]
        xp, qkv_p, fin = _layer(xp, mod_ctx, lw, l, batch, seq, None)
        xs, _, _ = _layer(xs, mod_lat, lw, l, dec_batch, dec_seq, (cache_k4, cache_v4, state_ssm))
        new_k.append(qkv_p[:, NA_WIDTH:2 * NA_WIDTH].reshape(batch, seq, NA_HEADS, NA_HEAD_DIM))
        new_v.append(qkv_p[:, 2 * NA_WIDTH:].reshape(batch, seq, NA_HEADS, NA_HEAD_DIM))
        new_s.append(fin)
    gf = g_norm_final[None, :]
    y_prompt = final_norm(xp, gf).reshape(batch, seq, d)
    y_sample = final_norm(xs, gf).reshape(dec_batch, dec_seq, d)
    return (y_prompt, y_sample, jnp.stack(new_k, axis=1), jnp.stack(new_v, axis=1), jnp.stack(new_s, axis=1))
```

```python
import functools
import math

import numpy as np
import jax
import jax.numpy as jnp
from jax import lax
from jax.experimental import pallas as pl
from jax.experimental.pallas import tpu as pltpu

F32 = jnp.float32
BF16 = jnp.bfloat16

D_MODEL = 2048
DEPTH = 4
GRID_W = 64
EPS = 1e-6
NA_HEADS = 8
NA_HEAD_DIM = 128
NA_WIDTH = NA_HEADS * NA_HEAD_DIM
NA_WIN_ROWS = 8
NA_WIN_COLS = 16
NA_ROW_GROUP = 4
SSM_HEADS = 16
SSM_HEAD_DIM = 64
SSM_INNER = SSM_HEADS * SSM_HEAD_DIM
SSM_GROUPS = 2
SSM_STATE = 128
SSM_CHUNK = 128
SSM_BC = SSM_GROUPS * SSM_STATE
SSM_CONV_DIM = SSM_INNER + 2 * SSM_BC
MLP_GROUPS = 8
MLP_WIDTH = 1024
MLP_CHUNK = 128
PEER_HEADS = 8
PEER_KEYS = 128
PEER_EXPERTS = PEER_KEYS * PEER_KEYS
PEER_QDIM = 256
PEER_TOPK = 16
N_BRANCH = 3

LANES = 128
BF16_ROWS = 16
SUB_KEYS = 2
SSM_COLS = SSM_INNER + SSM_CONV_DIM + 2 * LANES
DT_BLOCK0 = (SSM_INNER + SSM_CONV_DIM) // LANES
VMEM_LIMIT = 60 * 1024 * 1024
NEG_BIG = -1e30


def _cparams(sem):
    return pltpu.CompilerParams(dimension_semantics=sem, vmem_limit_bytes=VMEM_LIMIT)


def _nt_dot(a, b):
    return lax.dot_general(a, b, (((1,), (1,)), ((), ())), preferred_element_type=F32)


def _tn_dot(a, b):
    return lax.dot_general(a, b, (((0,), (0,)), ((), ())), preferred_element_type=F32)


def _mod_kernel(c_ref, w_ref, b_ref, o_ref):
    c = c_ref[...]
    a = c * jax.nn.sigmoid(c)
    o_ref[...] = jnp.dot(a, w_ref[...], preferred_element_type=F32,
                         precision=lax.Precision.HIGHEST) + b_ref[...]


def modulation(cvec, w_mod, b_mod):
    depth, d, n = w_mod.shape
    tn = 1024
    return pl.pallas_call(
        _mod_kernel,
        grid=(depth, n // tn),
        in_specs=[
            pl.BlockSpec((8, d), lambda l, j: (0, 0)),
            pl.BlockSpec((None, d, tn), lambda l, j: (l, 0, j)),
            pl.BlockSpec((None, 1, tn), lambda l, j: (l, 0, j)),
        ],
        out_specs=pl.BlockSpec((None, 8, tn), lambda l, j: (l, 0, j)),
        out_shape=jax.ShapeDtypeStruct((depth, 8, n), F32),
        compiler_params=_cparams(("parallel", "parallel")),
        name="modulation",
    )(cvec, w_mod, b_mod.reshape(depth, 1, n))


def _norm_mod_kernel(x_ref, g_ref, sc_ref, sh_ref, o_ref, *, transpose):
    x = x_ref[...]
    y = x * lax.rsqrt(jnp.mean(x * x, axis=-1, keepdims=True) + EPS) * g_ref[...]
    h = y * (1.0 + sc_ref[...]) + sh_ref[...]
    if transpose:
        h = h.T
    o_ref[...] = h.astype(o_ref.dtype)


def norm_mod(x, g, sc, sh, seg_len, transpose=False, tm=512):
    t, d = x.shape
    spt = seg_len // tm
    seg_spec = pl.BlockSpec((None, 1, d), lambda i: (i // spt, 0, 0))
    if transpose:
        out_spec = pl.BlockSpec((d, tm), lambda i: (0, i))
        out_shape = jax.ShapeDtypeStruct((d, t), BF16)
    else:
        out_spec = pl.BlockSpec((tm, d), lambda i: (i, 0))
        out_shape = jax.ShapeDtypeStruct((t, d), BF16)
    return pl.pallas_call(
        functools.partial(_norm_mod_kernel, transpose=transpose),
        grid=(t // tm,),
        in_specs=[pl.BlockSpec((tm, d), lambda i: (i, 0)),
                  pl.BlockSpec((1, d), lambda i: (0, 0)), seg_spec, seg_spec],
        out_specs=out_spec,
        out_shape=out_shape,
        compiler_params=_cparams(("parallel",)),
        name="norm_mod_t" if transpose else "norm_mod",
    )(x, g, sc, sh)


def _mm_kernel(a_ref, b_ref, o_ref):
    o_ref[...] = jnp.dot(a_ref[...], b_ref[...], preferred_element_type=F32).astype(o_ref.dtype)


def matmul(a, b, tm, tn, out_dtype=F32, name="matmul"):
    m, k = a.shape
    n = b.shape[1]
    tm = min(tm, m)
    return pl.pallas_call(
        _mm_kernel,
        grid=(m // tm, n // tn),
        in_specs=[pl.BlockSpec((tm, k), lambda i, j: (i, 0)),
                  pl.BlockSpec((k, tn), lambda i, j: (0, j))],
        out_specs=pl.BlockSpec((tm, tn), lambda i, j: (i, j)),
        out_shape=jax.ShapeDtypeStruct((m, n), out_dtype),
        compiler_params=_cparams(("parallel", "parallel")),
        name=name,
    )(a, b)


def _ctx_attn_kernel(q_ref, k_ref, v_ref, o_ref):
    scale = NA_HEAD_DIM ** -0.5
    hd = NA_HEAD_DIM
    for h in range(NA_HEADS):
        hs = slice(h * hd, (h + 1) * hd)
        q = q_ref[:, hs].astype(BF16)
        k = k_ref[:, hs].astype(BF16)
        v = v_ref[:, hs].astype(BF16)
        s = _nt_dot(q, k) * scale
        m = jnp.max(s, axis=-1, keepdims=True)
        p = jnp.exp(s - m)
        p = p / jnp.sum(p, axis=-1, keepdims=True)
        o_ref[:, hs] = jnp.dot(p.astype(BF16), v, preferred_element_type=F32).astype(o_ref.dtype)


def context_attention(qkv, nseq, seq):
    w = NA_WIDTH
    return pl.pallas_call(
        _ctx_attn_kernel,
        grid=(nseq,),
        in_specs=[pl.BlockSpec((seq, w), lambda b: (b, 0)),
                  pl.BlockSpec((seq, w), lambda b: (b, 1)),
                  pl.BlockSpec((seq, w), lambda b: (b, 2))],
        out_specs=pl.BlockSpec((seq, w), lambda b: (b, 0)),
        out_shape=jax.ShapeDtypeStruct((nseq * seq, w), BF16),
        compiler_params=_cparams(("parallel",)),
        name="ctx_attention",
    )(qkv, qkv, qkv)


def _na_kernel(q_ref, k_ref, v_ref, kc_ref, vc_ref, bias_ref, o_ref, kb_scr, vb_scr, *, rows):
    scale = NA_HEAD_DIM ** -0.5
    band = NA_WIN_ROWS * GRID_W
    kb_scr[...] = k_ref[...].astype(BF16)
    vb_scr[...] = v_ref[...].astype(BF16)
    kc = kc_ref[...].astype(BF16)
    vc = vc_ref[...].astype(BF16)

    def rows_step(it, carry):
        r0 = it * NA_ROW_GROUP
        q0 = pl.multiple_of(r0 * GRID_W, NA_ROW_GROUP * GRID_W)
        q = q_ref[pl.ds(q0, NA_ROW_GROUP * GRID_W), :].astype(BF16)
        k0s = []
        s_rows = []
        for j in range(NA_ROW_GROUP):
            r = r0 + j
            rs = jnp.clip(r - NA_WIN_ROWS // 2, 0, rows - NA_WIN_ROWS)
            k0 = pl.multiple_of(rs * GRID_W, GRID_W)
            k0s.append(k0)
            s_rows.append(_nt_dot(q[j * GRID_W:(j + 1) * GRID_W], kb_scr[pl.ds(k0, band), :]) * scale
                          + bias_ref[r - rs])
        s_win = jnp.concatenate(s_rows, axis=0)
        s_ctx = _nt_dot(q, kc) * scale
        m = jnp.maximum(jnp.max(s_win, axis=-1, keepdims=True), jnp.max(s_ctx, axis=-1, keepdims=True))
        p_win = jnp.exp(s_win - m)
        p_ctx = jnp.exp(s_ctx - m)
        den = jnp.sum(p_win, axis=-1, keepdims=True) + jnp.sum(p_ctx, axis=-1, keepdims=True)
        p_win = (p_win / den).astype(BF16)
        o_ctx = jnp.dot((p_ctx / den).astype(BF16), vc, preferred_element_type=F32)
        for j in range(NA_ROW_GROUP):
            o = jnp.dot(p_win[j * GRID_W:(j + 1) * GRID_W], vb_scr[pl.ds(k0s[j], band), :],
                        preferred_element_type=F32) + o_ctx[j * GRID_W:(j + 1) * GRID_W]
            o_ref[pl.ds(pl.multiple_of((r0 + j) * GRID_W, GRID_W), GRID_W), :] = o.astype(o_ref.dtype)
        return carry

    lax.fori_loop(0, rows // NA_ROW_GROUP, rows_step, 0)


def _na_bias_table(rel_bias):
    col = np.arange(GRID_W)
    cstart = np.clip(col - NA_WIN_COLS // 2, 0, GRID_W - NA_WIN_COLS)
    valid = (col[None, :] >= cstart[:, None]) & (col[None, :] < cstart[:, None] + NA_WIN_COLS)
    dcol = col[None, :] - col[:, None] + NA_WIN_COLS - 1
    onehot = ((np.arange(2 * NA_WIN_COLS - 1)[:, None, None] == dcol[None]) & valid[None]).astype(np.float32)
    tabc = jnp.einsum("hrk,kcd->hrcd", rel_bias, jnp.asarray(onehot), precision=lax.Precision.HIGHEST)
    tabc = tabc + jnp.asarray(np.where(valid, 0.0, NEG_BIG).astype(np.float32))
    wr = NA_WIN_ROWS
    tab = jnp.stack([tabc[:, wr - 1 - o:2 * wr - 1 - o].transpose(0, 2, 1, 3) for o in range(wr)], axis=1)
    return tab.reshape(rel_bias.shape[0], wr, GRID_W, wr * GRID_W).astype(F32)


def neighbourhood_attention(qkv, cache_k, cache_v, layer, bias_tab, nseq, seq):
    hd = NA_HEAD_DIM
    past = cache_k.shape[2]
    rows = seq // GRID_W
    band = NA_WIN_ROWS * GRID_W
    return pl.pallas_call(
        functools.partial(_na_kernel, rows=rows),
        grid=(nseq, NA_HEADS),
        in_specs=[pl.BlockSpec((seq, hd), lambda b, h: (b, h)),
                  pl.BlockSpec((seq, hd), lambda b, h: (b, NA_HEADS + h)),
                  pl.BlockSpec((seq, hd), lambda b, h: (b, 2 * NA_HEADS + h)),
                  pl.BlockSpec((None, None, past, hd), lambda b, h: (b, layer, 0, h)),
                  pl.BlockSpec((None, None, past, hd), lambda b, h: (b, layer, 0, h)),
                  pl.BlockSpec((None, NA_WIN_ROWS, GRID_W, band), lambda b, h: (h, 0, 0, 0))],
        out_specs=pl.BlockSpec((seq, hd), lambda b, h: (b, h)),
        out_shape=jax.ShapeDtypeStruct((nseq * seq, NA_WIDTH), BF16),
        scratch_shapes=[pltpu.VMEM((seq, hd), BF16), pltpu.VMEM((seq, hd), BF16)],
        compiler_params=_cparams(("parallel", "parallel")),
        name="nbr_attention",
    )(qkv, qkv, qkv, cache_k, cache_v, bias_tab)


def _conv_kernel(x_ref, w_ref, b_ref, o_ref):
    x = x_ref[...]
    n = x.shape[0]
    row = lax.broadcasted_iota(jnp.int32, x.shape, 0)
    xm2 = jnp.where(row >= 2, pltpu.roll(x, 2, 0), 0.0)
    xm1 = jnp.where(row >= 1, pltpu.roll(x, 1, 0), 0.0)
    xp1 = jnp.where(row < n - 1, pltpu.roll(x, n - 1, 0), 0.0)
    w = w_ref[...]
    y = w[0:1] * xm2 + w[1:2] * xm1 + w[2:3] * x + w[3:4] * xp1 + b_ref[...]
    o_ref[...] = y * jax.nn.sigmoid(y)


def conv_silu(ssm, conv_w, conv_b, nseq, seq):
    tc = 256
    c0 = SSM_INNER // tc
    return pl.pallas_call(
        _conv_kernel,
        grid=(nseq, SSM_CONV_DIM // tc),
        in_specs=[pl.BlockSpec((None, seq, tc), lambda s, j: (s, 0, c0 + j)),
                  pl.BlockSpec((4, tc), lambda s, j: (0, j)),
                  pl.BlockSpec((1, tc), lambda s, j: (0, j))],
        out_specs=pl.BlockSpec((None, seq, tc), lambda s, j: (s, 0, j)),
        out_shape=jax.ShapeDtypeStruct((nseq, seq, SSM_CONV_DIM), F32),
        compiler_params=_cparams(("parallel", "parallel")),
        name="conv_silu",
    )(ssm.reshape(nseq, seq, SSM_COLS), conv_w, conv_b)


def _select_columns(x, sel):
    hi = x.astype(BF16)
    r = x - hi.astype(F32)
    mid = r.astype(BF16)
    lo = (r - mid.astype(F32)).astype(BF16)
    return (jnp.dot(hi, sel, preferred_element_type=F32) + jnp.dot(mid, sel, preferred_element_type=F32)
            + jnp.dot(lo, sel, preferred_element_type=F32))


def _ssd_kernel(*refs, has_init, nchunk):
    if has_init:
        xc_ref, dt_ref, tri_ref, dtb_ref, alog_ref, sel_n_ref, sel_p_ref, init_ref, y_ref, fin_ref, st_scr = refs
    else:
        xc_ref, dt_ref, tri_ref, dtb_ref, alog_ref, sel_n_ref, sel_p_ref, y_ref, fin_ref, st_scr = refs
        init_ref = None
    c = pl.program_id(2)

    @pl.when(c == 0)
    def _():
        if has_init:
            st_scr[...] = init_ref[...]
        else:
            st_scr[...] = jnp.zeros_like(st_scr)

    p = SSM_HEAD_DIM
    n = SSM_STATE
    xc = xc_ref[...]
    dt = jax.nn.softplus(dt_ref[...] + dtb_ref[...])
    a = dt * (-jnp.exp(alog_ref[...]))
    tri = tri_ref[...]
    mask = tri > 0.5
    cum = jnp.dot(tri, a, preferred_element_type=F32, precision=lax.Precision.HIGHEST)
    cum_t = cum.T
    cum_x = _select_columns(cum, sel_n_ref[...])
    dt_x = _select_columns(dt, sel_p_ref[...])
    xd_all = xc[:, :SSM_INNER] * dt_x
    rep = SSM_HEADS // SSM_GROUPS
    gmat = []
    bmat = []
    cmat = []
    for g in range(SSM_GROUPS):
        b_g = xc[:, SSM_INNER + g * n:SSM_INNER + (g + 1) * n].astype(BF16)
        c_g = xc[:, SSM_INNER + SSM_BC + g * n:SSM_INNER + SSM_BC + (g + 1) * n].astype(BF16)
        bmat.append(b_g)
        cmat.append(c_g)
        gmat.append(_nt_dot(c_g, b_g))
    for h in range(SSM_HEADS):
        g = h // rep
        cum_h = cum_x[:, h * n:(h + 1) * n]
        decay = jnp.exp(jnp.where(mask, cum_h - cum_t[h:h + 1, :], -jnp.inf))
        xd = xd_all[:, h * p:(h + 1) * p]
        y_diag = jnp.dot((gmat[g] * decay).astype(BF16), xd.astype(BF16), preferred_element_type=F32)
        tot = jnp.min(cum_h, axis=0, keepdims=True)
        e_end = jnp.exp(tot[:, :p] - cum_h[:, :p])
        chunk_state = _tn_dot((xd * e_end).astype(BF16), bmat[g])
        prev = st_scr[h]
        y_off = _nt_dot(cmat[g], prev.astype(BF16)) * jnp.exp(cum_h[:, :p])
        y_ref[:, h * p:(h + 1) * p] = y_diag + y_off
        st_scr[h] = prev * jnp.exp(tot) + chunk_state

    @pl.when(c == nchunk - 1)
    def _():
        fin_ref[...] = st_scr[...]


def ssd_scan(xc, ssm, dt_bias, a_log, init, layer):
    nseq, seq, _ = xc.shape
    q = SSM_CHUNK
    nc = seq // q
    idx = np.arange(q)
    tri = jnp.asarray(np.stack([idx[None, :] <= idx[:, None], idx[None, :] >= idx[:, None]]).astype(np.float32))

    def cc(d, c):
        return c + d * (nc - 1 - 2 * c)

    in_specs = [pl.BlockSpec((None, q, SSM_CONV_DIM), lambda d, s, c: (s, cc(d, c), 0)),
                pl.BlockSpec((None, q, LANES), lambda d, s, c: (s, cc(d, c), DT_BLOCK0 + d)),
                pl.BlockSpec((None, q, q), lambda d, s, c: (d, 0, 0)),
                pl.BlockSpec((None, 1, LANES), lambda d, s, c: (d, 0, 0)),
                pl.BlockSpec((None, 1, LANES), lambda d, s, c: (d, 0, 0)),
                pl.BlockSpec((LANES, SSM_HEADS * SSM_STATE), lambda d, s, c: (0, 0)),
                pl.BlockSpec((LANES, SSM_INNER), lambda d, s, c: (0, 0))]
    head = np.arange(LANES)[:, None]
    sel_n = jnp.asarray(head == np.arange(SSM_HEADS * SSM_STATE)[None, :] // SSM_STATE, BF16)
    sel_p = jnp.asarray(head == np.arange(SSM_INNER)[None, :] // SSM_HEAD_DIM, BF16)
    args = [xc, ssm.reshape(nseq, seq, SSM_COLS), tri, dt_bias, a_log, sel_n, sel_p]
    if init is not None:
        in_specs.append(pl.BlockSpec((None, None, None, SSM_HEADS, SSM_HEAD_DIM, SSM_STATE),
                                     lambda d, s, c: (s, layer, d, 0, 0, 0)))
        args.append(init)
    return pl.pallas_call(
        functools.partial(_ssd_kernel, has_init=init is not None, nchunk=nc),
        grid=(2, nseq, nc),
        in_specs=in_specs,
        out_specs=[pl.BlockSpec((None, None, q, SSM_INNER), lambda d, s, c: (d, s, cc(d, c), 0)),
                   pl.BlockSpec((None, None, SSM_HEADS, SSM_HEAD_DIM, SSM_STATE), lambda d, s, c: (s, d, 0, 0, 0))],
        out_shape=[jax.ShapeDtypeStruct((2, nseq, seq, SSM_INNER), F32),
                   jax.ShapeDtypeStruct((nseq, 2, SSM_HEADS, SSM_HEAD_DIM, SSM_STATE), F32)],
        scratch_shapes=[pltpu.VMEM((SSM_HEADS, SSM_HEAD_DIM, SSM_STATE), F32)],
        compiler_params=_cparams(("parallel", "parallel", "arbitrary")),
        name="ssd_scan",
    )(*args)


def _ssm_finish_kernel(y_ref, xs_ref, z_ref, dsk_ref, g_ref, o_ref):
    z = z_ref[...]
    y = (y_ref[0] + y_ref[1] + dsk_ref[...] * xs_ref[...]) * (z * jax.nn.sigmoid(z))
    y = y * lax.rsqrt(jnp.mean(y * y, axis=-1, keepdims=True) + EPS) * g_ref[...]
    o_ref[...] = y.astype(o_ref.dtype)


def ssm_finish(y, xc, ssm, d_skip, g, tm=512):
    t = ssm.shape[0]
    w = SSM_INNER
    return pl.pallas_call(
        _ssm_finish_kernel,
        grid=(t // tm,),
        in_specs=[pl.BlockSpec((2, tm, w), lambda i: (0, i, 0)),
                  pl.BlockSpec((tm, w), lambda i: (i, 0)),
                  pl.BlockSpec((tm, w), lambda i: (i, 0)),
                  pl.BlockSpec((1, w), lambda i: (0, 0)),
                  pl.BlockSpec((1, w), lambda i: (0, 0))],
        out_specs=pl.BlockSpec((tm, w), lambda i: (i, 0)),
        out_shape=jax.ShapeDtypeStruct((t, w), BF16),
        compiler_params=_cparams(("parallel",)),
        name="ssm_finish",
    )(y, xc, ssm, d_skip, g)


def _gmlp_kernel(u_ref, v_ref, g_ref, w_ref, b_ref, o_ref):
    v = jax.nn.gelu(v_ref[...])
    v = (v * lax.rsqrt(jnp.mean(v * v, axis=-1, keepdims=True) + EPS) * g_ref[...]).astype(BF16)
    q = MLP_CHUNK
    gw = MLP_WIDTH // MLP_GROUPS
    for ch in range(v.shape[0] // q):
        for g in range(MLP_GROUPS):
            mixed = jnp.dot(w_ref[g], v[ch * q:(ch + 1) * q, g * gw:(g + 1) * gw],
                            preferred_element_type=F32) + b_ref[g]
            u = jax.nn.gelu(u_ref[ch * q:(ch + 1) * q, g * gw:(g + 1) * gw])
            o_ref[ch * q:(ch + 1) * q, g * gw:(g + 1) * gw] = (u * mixed).astype(o_ref.dtype)


def chunk_mlp(mlp, g, w_s, b_s, tm=512):
    t = mlp.shape[0]
    w = MLP_WIDTH
    return pl.pallas_call(
        _gmlp_kernel,
        grid=(t // tm,),
        in_specs=[pl.BlockSpec((tm, w), lambda i: (i, 0)),
                  pl.BlockSpec((tm, w), lambda i: (i, 1)),
                  pl.BlockSpec((1, w), lambda i: (0, 0)),
                  pl.BlockSpec(w_s.shape, lambda i: (0, 0, 0)),
                  pl.BlockSpec(b_s.shape, lambda i: (0, 0, 0))],
        out_specs=pl.BlockSpec((tm, w), lambda i: (i, 0)),
        out_shape=jax.ShapeDtypeStruct((t, w), BF16),
        compiler_params=_cparams(("parallel",)),
        name="chunk_mlp",
    )(mlp, mlp, g, w_s, b_s)


def _merge_kernel(a0_ref, a1_ref, a2_ref, w_ref, g0_ref, g1_ref, g2_ref, o_ref):
    acc = jax.nn.sigmoid(g0_ref[...]) * jnp.dot(a0_ref[...], w_ref[0], preferred_element_type=F32)
    acc += jax.nn.sigmoid(g1_ref[...]) * jnp.dot(a1_ref[...], w_ref[1], preferred_element_type=F32)
    acc += jax.nn.sigmoid(g2_ref[...]) * jnp.dot(a2_ref[...], w_ref[2], preferred_element_type=F32)
    o_ref[...] = acc.astype(o_ref.dtype)


def merge_branches(o_na, o_ssm, o_mlp, w_br, gates, tm=512, tn=512):
    t, k = o_na.shape
    d = w_br.shape[2]
    nb = d // tn
    a_spec = pl.BlockSpec((tm, k), lambda i, j: (i, 0))
    return pl.pallas_call(
        _merge_kernel,
        grid=(t // tm, nb),
        in_specs=[a_spec, a_spec, a_spec,
                  pl.BlockSpec((N_BRANCH, k, tn), lambda i, j: (0, 0, j)),
                  pl.BlockSpec((tm, tn), lambda i, j: (i, j)),
                  pl.BlockSpec((tm, tn), lambda i, j: (i, nb + j)),
                  pl.BlockSpec((tm, tn), lambda i, j: (i, 2 * nb + j))],
        out_specs=pl.BlockSpec((tm, tn), lambda i, j: (i, j)),
        out_shape=jax.ShapeDtypeStruct((t, d), BF16),
        compiler_params=_cparams(("parallel", "parallel")),
        name="merge_branches",
    )(o_na, o_ssm, o_mlp, w_br, gates, gates, gates)


def _out_proj_kernel(a_ref, w_ref, x_ref, gt_ref, o_ref):
    o_ref[...] = x_ref[...] + gt_ref[...] * jnp.dot(a_ref[...], w_ref[...], preferred_element_type=F32)


def out_proj(merged, w_out, x, gt, seg_len, tm=512, tn=512):
    t, k = merged.shape
    d = w_out.shape[1]
    spt = seg_len // tm
    return pl.pallas_call(
        _out_proj_kernel,
        grid=(t // tm, d // tn),
        in_specs=[pl.BlockSpec((tm, k), lambda i, j: (i, 0)),
                  pl.BlockSpec((k, tn), lambda i, j: (0, j)),
                  pl.BlockSpec((tm, tn), lambda i, j: (i, j)),
                  pl.BlockSpec((None, 1, tn), lambda i, j: (i // spt, 0, j))],
        out_specs=pl.BlockSpec((tm, tn), lambda i, j: (i, j)),
        out_shape=jax.ShapeDtypeStruct((t, d), F32),
        compiler_params=_cparams(("parallel", "parallel")),
        name="out_proj",
    )(merged, w_out, x, gt)


def _peer_scores_kernel(wq_ref, h_ref, key_ref, o_ref):
    qv = jnp.dot(wq_ref[...], h_ref[...], preferred_element_type=F32).astype(BF16)
    dq = PEER_QDIM // 2
    for hk in range(2 * PEER_HEADS):
        o_ref[hk * PEER_KEYS:(hk + 1) * PEER_KEYS, :] = jnp.dot(
            key_ref[hk], qv[hk * dq:(hk + 1) * dq, :], preferred_element_type=F32)


def peer_scores(wq_t, h_t, keys, tl=512):
    d, t = h_t.shape
    rows = 2 * PEER_HEADS * PEER_KEYS
    return pl.pallas_call(
        _peer_scores_kernel,
        grid=(t // tl,),
        in_specs=[pl.BlockSpec(wq_t.shape, lambda i: (0, 0)),
                  pl.BlockSpec((d, tl), lambda i: (0, i)),
                  pl.BlockSpec(keys.shape, lambda i: (0, 0, 0))],
        out_specs=pl.BlockSpec((rows, tl), lambda i: (0, i)),
        out_shape=jax.ShapeDtypeStruct((rows, t), F32),
        compiler_params=_cparams(("parallel",)),
        name="peer_scores",
    )(wq_t, h_t, keys)


def _top16(s):
    nk = s.shape[0]
    kio = lax.broadcasted_iota(jnp.int32, s.shape, 0).astype(F32)
    rank = jnp.full(s.shape, float(PEER_TOPK), F32)
    vals = []
    for a in range(PEER_TOPK):
        m = jnp.max(s, axis=0, keepdims=True)
        idx = jnp.min(jnp.where(s == m, kio, float(nk)), axis=0, keepdims=True)
        hit = kio == idx
        rank = jnp.where(hit, float(a), rank)
        s = jnp.where(hit, -jnp.inf, s)
        vals.append(m)
    return jnp.concatenate(vals, axis=0), rank


def _peer_select_kernel(s_ref, rank2_ref, cnt_ref, e1_ref, e2_ref):
    nk = PEER_KEYS
    k = PEER_TOPK
    for h in range(PEER_HEADS):
        s1 = s_ref[(2 * h) * nk:(2 * h + 1) * nk, :]
        s2 = s_ref[(2 * h + 1) * nk:(2 * h + 2) * nk, :]
        tv1, rank1 = _top16(s1)
        tv2, rank2 = _top16(s2)
        sub = lax.broadcasted_iota(jnp.int32, (8, s1.shape[1]), 0).astype(F32)
        pieces = []
        pos_pieces = []
        for a in range(k // 2):
            nb = k // (a + 1)
            for b0 in range(0, nb, 8):
                vals = tv1[a:a + 1, :] + tv2[b0:b0 + 8, :]
                pieces.append(vals if nb - b0 >= 8 else jnp.where(sub < float(nb - b0), vals, -jnp.inf))
                pos_pieces.append(sub + float(a * k + b0))
        pieces.append(tv1[k // 2:k, :] + tv2[0:1, :])
        pos_pieces.append((sub + float(k // 2)) * float(k))
        cand = jnp.concatenate(pieces, axis=0)
        pio = jnp.concatenate(pos_pieces, axis=0)
        aio = lax.broadcasted_iota(jnp.int32, (k, cand.shape[1]), 0).astype(F32)
        cnt_a = jnp.zeros((k, cand.shape[1]), F32)
        zsum = jnp.zeros((1, cand.shape[1]), F32)
        top = tv1[0:1, :] + tv2[0:1, :]
        for _ in range(k):
            m = jnp.max(cand, axis=0, keepdims=True)
            pos = jnp.min(jnp.where(cand == m, pio, float(k * k)), axis=0, keepdims=True)
            cand = jnp.where(pio == pos, -jnp.inf, cand)
            cnt_a = cnt_a + jnp.where(aio == jnp.floor(pos * (1.0 / k)), 1.0, 0.0)
            zsum = zsum + jnp.exp(m - top)
        cnt = jnp.zeros(s1.shape, F32)
        for a in range(k):
            cnt = jnp.where(rank1 == float(a), cnt_a[a:a + 1, :], cnt)
        rank2_ref[h * nk:(h + 1) * nk, :] = rank2.astype(rank2_ref.dtype)
        cnt_ref[h * nk:(h + 1) * nk, :] = cnt
        e1_ref[h * nk:(h + 1) * nk, :] = jnp.exp(s1 - tv1[0:1, :])
        e2_ref[h * nk:(h + 1) * nk, :] = (jnp.exp(s2 - tv2[0:1, :]) / zsum).astype(e2_ref.dtype)


def peer_select(scores, tl=128):
    rows, t = scores.shape
    half = rows // 2
    spec = pl.BlockSpec((half, tl), lambda i: (0, i))
    shp = jax.ShapeDtypeStruct((half, t), F32)
    shp_b = jax.ShapeDtypeStruct((half, t), BF16)
    return pl.pallas_call(
        _peer_select_kernel,
        grid=(t // tl,),
        in_specs=[pl.BlockSpec((rows, tl), lambda i: (0, i))],
        out_specs=[spec, spec, spec, spec],
        out_shape=[shp_b, shp, shp, shp_b],
        compiler_params=_cparams(("parallel",)),
        name="peer_select",
    )(scores)


def _peer_dense_kernel(h_ref, u_ref, vta_ref, vtb_ref, rank2_ref, cnt_ref, e1_ref, e2_ref, o_ref, za_scr, zb_scr,
                       *, nstep, rows_per_half, lane_tile):
    e = pl.program_id(1)

    @pl.when(e == 0)
    def _():
        o_ref[...] = jnp.zeros_like(o_ref)
        zb_scr[...] = jnp.zeros_like(zb_scr)

    e_cur = jnp.minimum(e, nstep - 1)
    live = jnp.where(e < nstep, 1.0, 0.0)
    nk = PEER_KEYS
    nsub = rows_per_half // SUB_KEYS
    mrows = o_ref.shape[0] // nsub
    lanes = [slice(lt * lane_tile, (lt + 1) * lane_tile) for lt in range(o_ref.shape[1] // lane_tile)]
    for half, (z_new, z_old, vt_ref) in enumerate(((za_scr, zb_scr, vta_ref), (zb_scr, za_scr, vtb_ref))):
        for sc in range(nsub):
            i0 = half * rows_per_half + sc * SUB_KEYS
            acts = [jax.nn.gelu(jnp.dot(u_ref[i0 * nk:(i0 + SUB_KEYS) * nk, :], h_ref[:, ls],
                                        preferred_element_type=F32)) for ls in lanes]
            for ls in lanes:
                o_ref[sc * mrows:(sc + 1) * mrows, ls] += jnp.dot(
                    vt_ref[sc * mrows:(sc + 1) * mrows, :], z_old[:, ls], preferred_element_type=F32)
            for ls, act in zip(lanes, acts):
                for k in range(SUB_KEYS):
                    w = None
                    for h in range(PEER_HEADS):
                        row = h * nk + e_cur * (2 * rows_per_half) + i0 + k
                        cnt_b = jnp.broadcast_to(cnt_ref[pl.ds(row, 1), ls] * live, (BF16_ROWS, lane_tile))
                        e1_b = jnp.broadcast_to(e1_ref[pl.ds(row, 1), ls], (BF16_ROWS, lane_tile))
                        cnt_b = jnp.concatenate([cnt_b.astype(BF16)] * (nk // BF16_ROWS), axis=0)
                        e1_b = jnp.concatenate([e1_b.astype(BF16)] * (nk // BF16_ROWS), axis=0)
                        sel = rank2_ref[h * nk:(h + 1) * nk, ls] < cnt_b
                        term = jnp.where(sel, e2_ref[h * nk:(h + 1) * nk, ls], jnp.zeros((), BF16)) * e1_b
                        w = term if w is None else w + term
                    r0 = (sc * SUB_KEYS + k) * nk
                    z_new[r0:r0 + nk, ls] = w * act[k * nk:(k + 1) * nk, :].astype(BF16)


def peer_dense(h_t, u_tab, v_t, rank2, cnt, e1, e2, tm=512, et=1024, lane_tile=256):
    d, t = h_t.shape
    ne = u_tab.shape[0]
    nstep = ne // et
    eh = et // 2
    map_spec = pl.BlockSpec((rank2.shape[0], tm), lambda i, e: (0, i))
    return pl.pallas_call(
        functools.partial(_peer_dense_kernel, nstep=nstep, rows_per_half=eh // PEER_KEYS, lane_tile=lane_tile),
        grid=(t // tm, nstep + 1),
        in_specs=[pl.BlockSpec((d, tm), lambda i, e: (0, i)),
                  pl.BlockSpec((et, d), lambda i, e: (jnp.minimum(e, nstep - 1), 0)),
                  pl.BlockSpec((d, eh), lambda i, e: (0, jnp.maximum(2 * e - 1, 0))),
                  pl.BlockSpec((d, eh), lambda i, e: (0, jnp.minimum(2 * e, 2 * nstep - 1))),
                  map_spec, map_spec, map_spec, map_spec],
        out_specs=pl.BlockSpec((d, tm), lambda i, e: (0, i)),
        out_shape=jax.ShapeDtypeStruct((d, t), F32),
        scratch_shapes=[pltpu.VMEM((eh, tm), BF16), pltpu.VMEM((eh, tm), BF16)],
        compiler_params=_cparams(("parallel", "arbitrary")),
        name="peer_dense",
    )(h_t, u_tab, v_t, v_t, rank2, cnt, e1, e2)


def _peer_resid_norm_kernel(x_ref, p_ref, gt_ref, g_ref, sc_ref, sh_ref, x_out_ref, h_ref):
    x = x_ref[...] + gt_ref[...] * p_ref[...].T
    x_out_ref[...] = x
    y = x * lax.rsqrt(jnp.mean(x * x, axis=-1, keepdims=True) + EPS) * g_ref[...]
    h_ref[...] = (y * (1.0 + sc_ref[...]) + sh_ref[...]).astype(h_ref.dtype)


def _peer_resid_final_kernel(x_ref, p_ref, gt_ref, g_ref, o_ref):
    x = x_ref[...] + gt_ref[...] * p_ref[...].T
    o_ref[...] = x * lax.rsqrt(jnp.mean(x * x, axis=-1, keepdims=True) + EPS) * g_ref[...]


def peer_residual_norm(x, peer_t, gt, g, sc, sh, seg_len, tm=512):
    t, d = x.shape
    spt = seg_len // tm
    row_spec = pl.BlockSpec((tm, d), lambda i: (i, 0))
    seg_spec = pl.BlockSpec((None, 1, d), lambda i: (i // spt, 0, 0))
    in_specs = [row_spec, pl.BlockSpec((d, tm), lambda i: (0, i)), seg_spec, pl.BlockSpec((1, d), lambda i: (0, 0))]
    if sc is None:
        return pl.pallas_call(
            _peer_resid_final_kernel,
            grid=(t // tm,),
            in_specs=in_specs,
            out_specs=row_spec,
            out_shape=jax.ShapeDtypeStruct((t, d), F32),
            compiler_params=_cparams(("parallel",)),
            name="peer_resid_final",
        )(x, peer_t, gt, g)
    return pl.pallas_call(
        _peer_resid_norm_kernel,
        grid=(t // tm,),
        in_specs=in_specs + [seg_spec, seg_spec],
        out_specs=[row_spec, row_spec],
        out_shape=[jax.ShapeDtypeStruct((t, d), F32), jax.ShapeDtypeStruct((t, d), BF16)],
        compiler_params=_cparams(("parallel",)),
        name="peer_resid_norm",
    )(x, peer_t, gt, g, sc, sh)


def _layer(x, h, mod, lw, layer, nseq, seq, ctx):
    sh1, sc1, gt1, sh2, sc2, gt2 = mod
    seg_len = x.shape[0] // sh1.shape[0]
    qkv = matmul(h, lw["w_qkv"], 1024, 512, name="proj_qkv")
    ssm = matmul(h, lw["w_ssm"], 1024, 128 * 11, name="proj_ssm")
    mlp = matmul(h, lw["w_mlp"], 1024, 512, name="proj_mlp")
    gates = matmul(h, lw["w_gate"], 1024, 512, name="proj_gate")
    if ctx is None:
        o_na = context_attention(qkv, nseq, seq)
        init = None
    else:
        cache_k, cache_v, init = ctx
        o_na = neighbourhood_attention(qkv, cache_k, cache_v, layer, lw["na_bias"], nseq, seq)
    xc = conv_silu(ssm, lw["conv_w"], lw["conv_b"], nseq, seq)
    y, fin = ssd_scan(xc, ssm, lw["dt_bias"], lw["a_log"], init, layer)
    t = nseq * seq
    o_ssm = ssm_finish(y.reshape(2, t, SSM_INNER), xc.reshape(t, SSM_CONV_DIM), ssm, lw["d_skip"], lw["ssm_norm_g"])
    o_mlp = chunk_mlp(mlp, lw["mlp_norm_g"], lw["mlp_w_s"], lw["mlp_b_s"])
    merged = merge_branches(o_na, o_ssm, o_mlp, lw["w_br"], gates)
    x = out_proj(merged, lw["w_out"], x, gt1, seg_len)
    h2t = norm_mod(x, lw["g_ffn"], sc2, sh2, seg_len, transpose=True)
    scores = peer_scores(lw["peer_wq_t"], h2t, lw["peer_keys"])
    rank2, cnt, e1, e2 = peer_select(scores)
    peer_t = peer_dense(h2t, lw["peer_u"], lw["peer_v_t"], rank2, cnt, e1, e2)
    return x, peer_t, qkv, fin


def _pad_lanes(v):
    pad = LANES - v.shape[-1]
    return jnp.pad(v, [(0, 0)] * (v.ndim - 1) + [(0, pad)])[..., None, :]


def kernel(x_prompt, x_sample, cache_k, cache_v, state_ssm, c, c_ctx, w_mod, b_mod, g_norm_mix, g_norm_ffn, g_norm_final, w_in, na_rel_bias, ssm_conv_w, ssm_conv_b, ssm_dt_bias, ssm_a_log, ssm_d, ssm_norm_g, mlp_norm_g, mlp_w_s, mlp_b_s, w_br_na, w_br_ssm, w_br_mlp, w_out, peer_w_q, peer_sub_keys, peer_u, peer_v):
    batch, seq, d = x_prompt.shape
    dec_batch, dec_seq, _ = x_sample.shape
    depth = w_in.shape[0]
    past = cache_k.shape[2]

    cvec = jnp.concatenate([c_ctx[None, :], c, jnp.zeros((8 - 1 - dec_batch, d), F32)], axis=0)
    mod_all = modulation(cvec, w_mod, b_mod)

    o_q = 0
    o_z = 3 * NA_WIDTH
    o_xbc = o_z + SSM_INNER
    o_dt = o_xbc + SSM_CONV_DIM
    o_u = o_dt + 2 * SSM_HEADS
    o_g = o_u + 2 * MLP_WIDTH
    dt_pad = jnp.zeros((depth, d, LANES - SSM_HEADS), F32)
    w_ssm = jnp.concatenate([w_in[:, :, o_z:o_dt], w_in[:, :, o_dt:o_dt + SSM_HEADS], dt_pad,
                             w_in[:, :, o_dt + SSM_HEADS:o_u], dt_pad], axis=-1).astype(BF16)
    w_qkv = w_in[:, :, o_q:o_z].astype(BF16)
    w_mlp = w_in[:, :, o_u:o_g].astype(BF16)
    w_gate = w_in[:, :, o_g:].astype(BF16)
    w_br = jnp.stack([w_br_na, w_br_ssm, w_br_mlp], axis=1).astype(BF16)
    w_out_b = w_out.astype(BF16)
    peer_wq_t = jnp.swapaxes(peer_w_q, 1, 2).astype(BF16)
    peer_keys = peer_sub_keys.reshape(depth, 2 * PEER_HEADS, PEER_KEYS, PEER_QDIM // 2).astype(BF16)
    peer_u_b = peer_u.astype(BF16)
    peer_v_t = jnp.swapaxes(peer_v, 1, 2).astype(BF16)
    dt_bias = _pad_lanes(ssm_dt_bias)
    a_log = _pad_lanes(ssm_a_log)
    d_skip = jnp.repeat(ssm_d, SSM_HEAD_DIM, axis=-1)[:, None, :]
    mlp_b = jnp.broadcast_to(mlp_b_s[..., None], mlp_b_s.shape + (MLP_CHUNK,))
    mlp_w = mlp_w_s.astype(BF16)

    cache_k4 = cache_k.reshape(dec_batch, depth, past, NA_WIDTH)
    cache_v4 = cache_v.reshape(dec_batch, depth, past, NA_WIDTH)

    xp = x_prompt.reshape(batch * seq, d)
    xs = x_sample.reshape(dec_batch * dec_seq, d)
    mods = []
    for l in range(depth):
        m = mod_all[l].reshape(8, 6, d)
        mods.append(([m[0:1, j][:, None, :] for j in range(6)],
                     [m[1:1 + dec_batch, j][:, None, :] for j in range(6)]))
    new_k, new_v, new_s = [], [], []
    for l in range(depth):
        lw = {
            "g_mix": g_norm_mix[l][None, :], "g_ffn": g_norm_ffn[l][None, :],
            "w_qkv": w_qkv[l], "w_ssm": w_ssm[l], "w_mlp": w_mlp[l], "w_gate": w_gate[l],
            "na_bias": _na_bias_table(na_rel_bias[l]),
            "conv_w": ssm_conv_w[l], "conv_b": ssm_conv_b[l][None, :],
            "dt_bias": dt_bias[l], "a_log": a_log[l], "d_skip": d_skip[l],
            "ssm_norm_g": ssm_norm_g[l][None, :], "mlp_norm_g": mlp_norm_g[l][None, :],
            "mlp_w_s": mlp_w[l], "mlp_b_s": mlp_b[l],
            "w_br": w_br[l], "w_out": w_out_b[l],
            "peer_wq_t": peer_wq_t[l], "peer_keys": peer_keys[l], "peer_u": peer_u_b[l], "peer_v_t": peer_v_t[l],
        }
        mod_ctx, mod_lat = mods[l]
        if l == 0:
            hp = norm_mod(xp, lw["g_mix"], mod_ctx[1], mod_ctx[0], batch * seq)
            hs = norm_mod(xs, lw["g_mix"], mod_lat[1], mod_lat[0], dec_seq)
        xp, peer_p, qkv_p, fin = _layer(xp, hp, mod_ctx, lw, l, batch, seq, None)
        xs, peer_s, _, _ = _layer(xs, hs, mod_lat, lw, l, dec_batch, dec_seq, (cache_k4, cache_v4, state_ssm))
        if l + 1 < depth:
            nxt_ctx, nxt_lat = mods[l + 1]
            g_next = g_norm_mix[l + 1][None, :]
            xp, hp = peer_residual_norm(xp, peer_p, mod_ctx[5], g_next, nxt_ctx[1], nxt_ctx[0], batch * seq)
            xs, hs = peer_residual_norm(xs, peer_s, mod_lat[5], g_next, nxt_lat[1], nxt_lat[0], dec_seq)
        else:
            gf = g_norm_final[None, :]
            y_prompt = peer_residual_norm(xp, peer_p, mod_ctx[5], gf, None, None, batch * seq)
            y_sample = peer_residual_norm(xs, peer_s, mod_lat[5], gf, None, None, dec_seq)
        new_k.append(qkv_p[:, NA_WIDTH:2 * NA_WIDTH].reshape(batch, seq, NA_HEADS, NA_HEAD_DIM))
        new_v.append(qkv_p[:, 2 * NA_WIDTH:].reshape(batch, seq, NA_HEADS, NA_HEAD_DIM))
        new_s.append(fin)
    return (y_prompt.reshape(batch, seq, d), y_sample.reshape(dec_batch, dec_seq, d),
            jnp.stack(new_k, axis=1), jnp.stack(new_v, axis=1), jnp.stack(new_s, axis=1))
```

```python
import functools
import math

import numpy as np
import jax
import jax.numpy as jnp
from jax import lax
from jax.experimental import pallas as pl
from jax.experimental.pallas import tpu as pltpu

F32 = jnp.float32
BF16 = jnp.bfloat16

D_MODEL = 2048
DEPTH = 4
GRID_W = 64
EPS = 1e-6
NA_HEADS = 8
NA_HEAD_DIM = 128
NA_WIDTH = NA_HEADS * NA_HEAD_DIM
NA_WIN_ROWS = 8
NA_WIN_COLS = 16
NA_ROW_GROUP = 4
SSM_HEADS = 16
SSM_HEAD_DIM = 64
SSM_INNER = SSM_HEADS * SSM_HEAD_DIM
SSM_GROUPS = 2
SSM_STATE = 128
SSM_CHUNK = 128
SSM_BC = SSM_GROUPS * SSM_STATE
SSM_CONV_DIM = SSM_INNER + 2 * SSM_BC
MLP_GROUPS = 8
MLP_WIDTH = 1024
MLP_CHUNK = 128
PEER_HEADS = 8
PEER_KEYS = 128
PEER_EXPERTS = PEER_KEYS * PEER_KEYS
PEER_QDIM = 256
PEER_TOPK = 16
N_BRANCH = 3

LANES = 128
BF16_ROWS = 16
SUB_KEYS = 2
SSM_COLS = SSM_INNER + SSM_CONV_DIM + 2 * LANES
DT_BLOCK0 = (SSM_INNER + SSM_CONV_DIM) // LANES
VMEM_LIMIT = 60 * 1024 * 1024
NEG_BIG = -1e30


def _cparams(sem):
    return pltpu.CompilerParams(dimension_semantics=sem, vmem_limit_bytes=VMEM_LIMIT)


def _nt_dot(a, b):
    return lax.dot_general(a, b, (((1,), (1,)), ((), ())), preferred_element_type=F32)


def _tn_dot(a, b):
    return lax.dot_general(a, b, (((0,), (0,)), ((), ())), preferred_element_type=F32)


def _mod_kernel(c_ref, w_ref, b_ref, o_ref):
    c = c_ref[...]
    a = c * jax.nn.sigmoid(c)
    o_ref[...] = jnp.dot(a, w_ref[...], preferred_element_type=F32,
                         precision=lax.Precision.HIGHEST) + b_ref[...]


def modulation(cvec, w_mod, b_mod):
    depth, d, n = w_mod.shape
    tn = 1024
    return pl.pallas_call(
        _mod_kernel,
        grid=(depth, n // tn),
        in_specs=[
            pl.BlockSpec((8, d), lambda l, j: (0, 0)),
            pl.BlockSpec((None, d, tn), lambda l, j: (l, 0, j)),
            pl.BlockSpec((None, 1, tn), lambda l, j: (l, 0, j)),
        ],
        out_specs=pl.BlockSpec((None, 8, tn), lambda l, j: (l, 0, j)),
        out_shape=jax.ShapeDtypeStruct((depth, 8, n), F32),
        compiler_params=_cparams(("parallel", "parallel")),
        name="modulation",
    )(cvec, w_mod, b_mod.reshape(depth, 1, n))


def _norm_mod_kernel(x_ref, g_ref, sc_ref, sh_ref, o_ref, *, transpose):
    x = x_ref[...]
    y = x * lax.rsqrt(jnp.mean(x * x, axis=-1, keepdims=True) + EPS) * g_ref[...]
    h = y * (1.0 + sc_ref[...]) + sh_ref[...]
    if transpose:
        h = h.T
    o_ref[...] = h.astype(o_ref.dtype)


def norm_mod(x, g, sc, sh, seg_len, transpose=False, tm=512):
    t, d = x.shape
    spt = seg_len // tm
    seg_spec = pl.BlockSpec((None, 1, d), lambda i: (i // spt, 0, 0))
    if transpose:
        out_spec = pl.BlockSpec((d, tm), lambda i: (0, i))
        out_shape = jax.ShapeDtypeStruct((d, t), BF16)
    else:
        out_spec = pl.BlockSpec((tm, d), lambda i: (i, 0))
        out_shape = jax.ShapeDtypeStruct((t, d), BF16)
    return pl.pallas_call(
        functools.partial(_norm_mod_kernel, transpose=transpose),
        grid=(t // tm,),
        in_specs=[pl.BlockSpec((tm, d), lambda i: (i, 0)),
                  pl.BlockSpec((1, d), lambda i: (0, 0)), seg_spec, seg_spec],
        out_specs=out_spec,
        out_shape=out_shape,
        compiler_params=_cparams(("parallel",)),
        name="norm_mod_t" if transpose else "norm_mod",
    )(x, g, sc, sh)


def _mm_kernel(a_ref, b_ref, o_ref):
    o_ref[...] = jnp.dot(a_ref[...], b_ref[...], preferred_element_type=F32).astype(o_ref.dtype)


def matmul(a, b, tm, tn, out_dtype=F32, name="matmul", col0=0, n=None):
    m, k = a.shape
    n = b.shape[1] if n is None else n
    tm = min(tm, m)
    c0 = col0 // tn
    return pl.pallas_call(
        _mm_kernel,
        grid=(m // tm, n // tn),
        in_specs=[pl.BlockSpec((tm, k), lambda i, j: (i, 0)),
                  pl.BlockSpec((k, tn), lambda i, j: (0, c0 + j))],
        out_specs=pl.BlockSpec((tm, tn), lambda i, j: (i, j)),
        out_shape=jax.ShapeDtypeStruct((m, n), out_dtype),
        compiler_params=_cparams(("parallel", "parallel")),
        name=name,
    )(a, b)


def _mm_layer_kernel(a_ref, b_ref, *refs):
    o_ref = refs[-1]
    o_ref[...] = jnp.dot(a_ref[...], b_ref[...], preferred_element_type=F32).reshape(o_ref.shape)


def matmul_into_layer(a, b, col0, n, buf, layer, depth, seq, tm, tn, name):
    m, k = a.shape
    tm = min(tm, m)
    sp = tm // seq
    c0 = col0 // tn
    in_specs = [pl.BlockSpec((tm, k), lambda i, j: (i, 0)),
                pl.BlockSpec((k, tn), lambda i, j: (0, c0 + j))]
    args = [a, b]
    aliases = {}
    if buf is not None:
        in_specs.append(pl.BlockSpec(memory_space=pl.ANY))
        args.append(buf)
        aliases = {2: 0}
    return pl.pallas_call(
        _mm_layer_kernel,
        grid=(m // tm, n // tn),
        in_specs=in_specs,
        out_specs=pl.BlockSpec((sp, None, seq, tn), lambda i, j: (i, layer, 0, j)),
        out_shape=jax.ShapeDtypeStruct((m // seq, depth, seq, n), F32),
        input_output_aliases=aliases,
        compiler_params=_cparams(("parallel", "parallel")),
        name=name,
    )(*args)


def _ctx_attn_kernel(q_ref, k_ref, v_ref, o_ref):
    scale = NA_HEAD_DIM ** -0.5
    hd = NA_HEAD_DIM
    for h in range(NA_HEADS):
        hs = slice(h * hd, (h + 1) * hd)
        q = q_ref[:, hs].astype(BF16)
        k = k_ref[:, hs].astype(BF16)
        v = v_ref[:, hs].astype(BF16)
        s = _nt_dot(q, k) * scale
        m = jnp.max(s, axis=-1, keepdims=True)
        p = jnp.exp(s - m)
        p = p / jnp.sum(p, axis=-1, keepdims=True)
        o_ref[:, hs] = jnp.dot(p.astype(BF16), v, preferred_element_type=F32).astype(o_ref.dtype)


def context_attention(q, kbuf, vbuf, layer, nseq, seq):
    w = NA_WIDTH
    kv_spec = pl.BlockSpec((None, None, seq, w), lambda b: (b, layer, 0, 0))
    return pl.pallas_call(
        _ctx_attn_kernel,
        grid=(nseq,),
        in_specs=[pl.BlockSpec((seq, w), lambda b: (b, 0)), kv_spec, kv_spec],
        out_specs=pl.BlockSpec((seq, w), lambda b: (b, 0)),
        out_shape=jax.ShapeDtypeStruct((nseq * seq, w), BF16),
        compiler_params=_cparams(("parallel",)),
        name="ctx_attention",
    )(q, kbuf, vbuf)


def _na_kernel(q_ref, k_ref, v_ref, kc_ref, vc_ref, bias_ref, o_ref, kb_scr, vb_scr, *, rows):
    scale = NA_HEAD_DIM ** -0.5
    band = NA_WIN_ROWS * GRID_W
    kb_scr[...] = k_ref[...].astype(BF16)
    vb_scr[...] = v_ref[...].astype(BF16)
    kc = kc_ref[...].astype(BF16)
    vc = vc_ref[...].astype(BF16)

    def rows_step(it, carry):
        r0 = it * NA_ROW_GROUP
        q0 = pl.multiple_of(r0 * GRID_W, NA_ROW_GROUP * GRID_W)
        q = q_ref[pl.ds(q0, NA_ROW_GROUP * GRID_W), :].astype(BF16)
        k0s = []
        s_rows = []
        for j in range(NA_ROW_GROUP):
            r = r0 + j
            rs = jnp.clip(r - NA_WIN_ROWS // 2, 0, rows - NA_WIN_ROWS)
            k0 = pl.multiple_of(rs * GRID_W, GRID_W)
            k0s.append(k0)
            s_rows.append(_nt_dot(q[j * GRID_W:(j + 1) * GRID_W], kb_scr[pl.ds(k0, band), :]) * scale
                          + bias_ref[r - rs])
        s_win = jnp.concatenate(s_rows, axis=0)
        s_ctx = _nt_dot(q, kc) * scale
        m = jnp.maximum(jnp.max(s_win, axis=-1, keepdims=True), jnp.max(s_ctx, axis=-1, keepdims=True))
        p_win = jnp.exp(s_win - m)
        p_ctx = jnp.exp(s_ctx - m)
        den = jnp.sum(p_win, axis=-1, keepdims=True) + jnp.sum(p_ctx, axis=-1, keepdims=True)
        p_win = (p_win / den).astype(BF16)
        o_ctx = jnp.dot((p_ctx / den).astype(BF16), vc, preferred_element_type=F32)
        for j in range(NA_ROW_GROUP):
            o = jnp.dot(p_win[j * GRID_W:(j + 1) * GRID_W], vb_scr[pl.ds(k0s[j], band), :],
                        preferred_element_type=F32) + o_ctx[j * GRID_W:(j + 1) * GRID_W]
            o_ref[pl.ds(pl.multiple_of((r0 + j) * GRID_W, GRID_W), GRID_W), :] = o.astype(o_ref.dtype)
        return carry

    lax.fori_loop(0, rows // NA_ROW_GROUP, rows_step, 0)


def _na_bias_table(rel_bias):
    col = np.arange(GRID_W)
    cstart = np.clip(col - NA_WIN_COLS // 2, 0, GRID_W - NA_WIN_COLS)
    valid = (col[None, :] >= cstart[:, None]) & (col[None, :] < cstart[:, None] + NA_WIN_COLS)
    dcol = col[None, :] - col[:, None] + NA_WIN_COLS - 1
    onehot = ((np.arange(2 * NA_WIN_COLS - 1)[:, None, None] == dcol[None]) & valid[None]).astype(np.float32)
    tabc = jnp.einsum("hrk,kcd->hrcd", rel_bias, jnp.asarray(onehot), precision=lax.Precision.HIGHEST)
    tabc = tabc + jnp.asarray(np.where(valid, 0.0, NEG_BIG).astype(np.float32))
    wr = NA_WIN_ROWS
    tab = jnp.stack([tabc[:, wr - 1 - o:2 * wr - 1 - o].transpose(0, 2, 1, 3) for o in range(wr)], axis=1)
    return tab.reshape(rel_bias.shape[0], wr, GRID_W, wr * GRID_W).astype(F32)


def neighbourhood_attention(qkv, cache_k, cache_v, layer, bias_tab, nseq, seq):
    hd = NA_HEAD_DIM
    past = cache_k.shape[2]
    rows = seq // GRID_W
    band = NA_WIN_ROWS * GRID_W
    return pl.pallas_call(
        functools.partial(_na_kernel, rows=rows),
        grid=(nseq, NA_HEADS),
        in_specs=[pl.BlockSpec((seq, hd), lambda b, h: (b, h)),
                  pl.BlockSpec((seq, hd), lambda b, h: (b, NA_HEADS + h)),
                  pl.BlockSpec((seq, hd), lambda b, h: (b, 2 * NA_HEADS + h)),
                  pl.BlockSpec((None, None, past, hd), lambda b, h: (b, layer, 0, h)),
                  pl.BlockSpec((None, None, past, hd), lambda b, h: (b, layer, 0, h)),
                  pl.BlockSpec((None, NA_WIN_ROWS, GRID_W, band), lambda b, h: (h, 0, 0, 0))],
        out_specs=pl.BlockSpec((seq, hd), lambda b, h: (b, h)),
        out_shape=jax.ShapeDtypeStruct((nseq * seq, NA_WIDTH), BF16),
        scratch_shapes=[pltpu.VMEM((seq, hd), BF16), pltpu.VMEM((seq, hd), BF16)],
        compiler_params=_cparams(("parallel", "parallel")),
        name="nbr_attention",
    )(qkv, qkv, qkv, cache_k, cache_v, bias_tab)


def _conv_kernel(x_ref, w_ref, b_ref, o_ref):
    x = x_ref[...]
    n = x.shape[0]
    row = lax.broadcasted_iota(jnp.int32, x.shape, 0)
    xm2 = jnp.where(row >= 2, pltpu.roll(x, 2, 0), 0.0)
    xm1 = jnp.where(row >= 1, pltpu.roll(x, 1, 0), 0.0)
    xp1 = jnp.where(row < n - 1, pltpu.roll(x, n - 1, 0), 0.0)
    w = w_ref[...]
    y = w[0:1] * xm2 + w[1:2] * xm1 + w[2:3] * x + w[3:4] * xp1 + b_ref[...]
    o_ref[...] = y * jax.nn.sigmoid(y)


def conv_silu(ssm, conv_w, conv_b, nseq, seq):
    tc = 256
    c0 = SSM_INNER // tc
    return pl.pallas_call(
        _conv_kernel,
        grid=(nseq, SSM_CONV_DIM // tc),
        in_specs=[pl.BlockSpec((None, seq, tc), lambda s, j: (s, 0, c0 + j)),
                  pl.BlockSpec((4, tc), lambda s, j: (0, j)),
                  pl.BlockSpec((1, tc), lambda s, j: (0, j))],
        out_specs=pl.BlockSpec((None, seq, tc), lambda s, j: (s, 0, j)),
        out_shape=jax.ShapeDtypeStruct((nseq, seq, SSM_CONV_DIM), F32),
        compiler_params=_cparams(("parallel", "parallel")),
        name="conv_silu",
    )(ssm.reshape(nseq, seq, SSM_COLS), conv_w, conv_b)


def _select_columns(x, sel):
    hi = x.astype(BF16)
    r = x - hi.astype(F32)
    mid = r.astype(BF16)
    lo = (r - mid.astype(F32)).astype(BF16)
    return (jnp.dot(hi, sel, preferred_element_type=F32) + jnp.dot(mid, sel, preferred_element_type=F32)
            + jnp.dot(lo, sel, preferred_element_type=F32))


def _ssd_kernel(*refs, has_init, nchunk):
    if has_init:
        xc_ref, dt_ref, tri_ref, dtb_ref, alog_ref, sel_n_ref, sel_p_ref, init_ref, y_ref, fin_ref, st_scr = refs
    else:
        xc_ref, dt_ref, tri_ref, dtb_ref, alog_ref, sel_n_ref, sel_p_ref, y_ref, fin_ref, st_scr = refs
        init_ref = None
    c = pl.program_id(2)

    @pl.when(c == 0)
    def _():
        if has_init:
            st_scr[...] = init_ref[...]
        else:
            st_scr[...] = jnp.zeros_like(st_scr)

    p = SSM_HEAD_DIM
    n = SSM_STATE
    xc = xc_ref[...]
    dt = jax.nn.softplus(dt_ref[...] + dtb_ref[...])
    a = dt * (-jnp.exp(alog_ref[...]))
    tri = tri_ref[...]
    mask = tri > 0.5
    cum = jnp.dot(tri, a, preferred_element_type=F32, precision=lax.Precision.HIGHEST)
    cum_t = cum.T
    cum_x = _select_columns(cum, sel_n_ref[...])
    dt_x = _select_columns(dt, sel_p_ref[...])
    xd_all = xc[:, :SSM_INNER] * dt_x
    rep = SSM_HEADS // SSM_GROUPS
    gmat = []
    bmat = []
    cmat = []
    for g in range(SSM_GROUPS):
        b_g = xc[:, SSM_INNER + g * n:SSM_INNER + (g + 1) * n].astype(BF16)
        c_g = xc[:, SSM_INNER + SSM_BC + g * n:SSM_INNER + SSM_BC + (g + 1) * n].astype(BF16)
        bmat.append(b_g)
        cmat.append(c_g)
        gmat.append(_nt_dot(c_g, b_g))
    for h in range(SSM_HEADS):
        g = h // rep
        cum_h = cum_x[:, h * n:(h + 1) * n]
        decay = jnp.exp(jnp.where(mask, cum_h - cum_t[h:h + 1, :], -jnp.inf))
        xd = xd_all[:, h * p:(h + 1) * p]
        y_diag = jnp.dot((gmat[g] * decay).astype(BF16), xd.astype(BF16), preferred_element_type=F32)
        tot = jnp.min(cum_h, axis=0, keepdims=True)
        e_end = jnp.exp(tot[:, :p] - cum_h[:, :p])
        chunk_state = _tn_dot((xd * e_end).astype(BF16), bmat[g])
        prev = st_scr[h]
        y_off = _nt_dot(cmat[g], prev.astype(BF16)) * jnp.exp(cum_h[:, :p])
        y_ref[:, h * p:(h + 1) * p] = y_diag + y_off
        st_scr[h] = prev * jnp.exp(tot) + chunk_state

    @pl.when(c == nchunk - 1)
    def _():
        fin_ref[...] = st_scr[...]


def ssd_scan(xc, ssm, dt_bias, a_log, init, layer):
    nseq, seq, _ = xc.shape
    q = SSM_CHUNK
    nc = seq // q
    idx = np.arange(q)
    tri = jnp.asarray(np.stack([idx[None, :] <= idx[:, None], idx[None, :] >= idx[:, None]]).astype(np.float32))

    def cc(d, c):
        return c + d * (nc - 1 - 2 * c)

    in_specs = [pl.BlockSpec((None, q, SSM_CONV_DIM), lambda d, s, c: (s, cc(d, c), 0)),
                pl.BlockSpec((None, q, LANES), lambda d, s, c: (s, cc(d, c), DT_BLOCK0 + d)),
                pl.BlockSpec((None, q, q), lambda d, s, c: (d, 0, 0)),
                pl.BlockSpec((None, 1, LANES), lambda d, s, c: (d, 0, 0)),
                pl.BlockSpec((None, 1, LANES), lambda d, s, c: (d, 0, 0)),
                pl.BlockSpec((LANES, SSM_HEADS * SSM_STATE), lambda d, s, c: (0, 0)),
                pl.BlockSpec((LANES, SSM_INNER), lambda d, s, c: (0, 0))]
    head = np.arange(LANES)[:, None]
    sel_n = jnp.asarray(head == np.arange(SSM_HEADS * SSM_STATE)[None, :] // SSM_STATE, BF16)
    sel_p = jnp.asarray(head == np.arange(SSM_INNER)[None, :] // SSM_HEAD_DIM, BF16)
    args = [xc, ssm.reshape(nseq, seq, SSM_COLS), tri, dt_bias, a_log, sel_n, sel_p]
    if init is not None:
        in_specs.append(pl.BlockSpec((None, None, None, SSM_HEADS, SSM_HEAD_DIM, SSM_STATE),
                                     lambda d, s, c: (s, layer, d, 0, 0, 0)))
        args.append(init)
    return pl.pallas_call(
        functools.partial(_ssd_kernel, has_init=init is not None, nchunk=nc),
        grid=(2, nseq, nc),
        in_specs=in_specs,
        out_specs=[pl.BlockSpec((None, None, q, SSM_INNER), lambda d, s, c: (d, s, cc(d, c), 0)),
                   pl.BlockSpec((None, None, SSM_HEADS, SSM_HEAD_DIM, SSM_STATE), lambda d, s, c: (s, d, 0, 0, 0))],
        out_shape=[jax.ShapeDtypeStruct((2, nseq, seq, SSM_INNER), F32),
                   jax.ShapeDtypeStruct((nseq, 2, SSM_HEADS, SSM_HEAD_DIM, SSM_STATE), F32)],
        scratch_shapes=[pltpu.VMEM((SSM_HEADS, SSM_HEAD_DIM, SSM_STATE), F32)],
        compiler_params=_cparams(("parallel", "parallel", "arbitrary")),
        name="ssd_scan",
    )(*args)


def _ssm_finish_kernel(y_ref, xs_ref, z_ref, dsk_ref, g_ref, o_ref):
    z = z_ref[...]
    y = (y_ref[0] + y_ref[1] + dsk_ref[...] * xs_ref[...]) * (z * jax.nn.sigmoid(z))
    y = y * lax.rsqrt(jnp.mean(y * y, axis=-1, keepdims=True) + EPS) * g_ref[...]
    o_ref[...] = y.astype(o_ref.dtype)


def ssm_finish(y, xc, ssm, d_skip, g, tm=512):
    t = ssm.shape[0]
    w = SSM_INNER
    return pl.pallas_call(
        _ssm_finish_kernel,
        grid=(t // tm,),
        in_specs=[pl.BlockSpec((2, tm, w), lambda i: (0, i, 0)),
                  pl.BlockSpec((tm, w), lambda i: (i, 0)),
                  pl.BlockSpec((tm, w), lambda i: (i, 0)),
                  pl.BlockSpec((1, w), lambda i: (0, 0)),
                  pl.BlockSpec((1, w), lambda i: (0, 0))],
        out_specs=pl.BlockSpec((tm, w), lambda i: (i, 0)),
        out_shape=jax.ShapeDtypeStruct((t, w), BF16),
        compiler_params=_cparams(("parallel",)),
        name="ssm_finish",
    )(y, xc, ssm, d_skip, g)


def _gmlp_kernel(u_ref, v_ref, g_ref, w_ref, b_ref, o_ref):
    v = jax.nn.gelu(v_ref[...].astype(F32))
    v = (v * lax.rsqrt(jnp.mean(v * v, axis=-1, keepdims=True) + EPS) * g_ref[...]).astype(BF16)
    q = MLP_CHUNK
    gw = MLP_WIDTH // MLP_GROUPS
    for ch in range(v.shape[0] // q):
        for g in range(MLP_GROUPS):
            mixed = jnp.dot(w_ref[g], v[ch * q:(ch + 1) * q, g * gw:(g + 1) * gw],
                            preferred_element_type=F32) + b_ref[g]
            u = jax.nn.gelu(u_ref[ch * q:(ch + 1) * q, g * gw:(g + 1) * gw].astype(F32))
            o_ref[ch * q:(ch + 1) * q, g * gw:(g + 1) * gw] = (u * mixed).astype(o_ref.dtype)


def chunk_mlp(mlp, g, w_s, b_s, tm=512):
    t = mlp.shape[0]
    w = MLP_WIDTH
    return pl.pallas_call(
        _gmlp_kernel,
        grid=(t // tm,),
        in_specs=[pl.BlockSpec((tm, w), lambda i: (i, 0)),
                  pl.BlockSpec((tm, w), lambda i: (i, 1)),
                  pl.BlockSpec((1, w), lambda i: (0, 0)),
                  pl.BlockSpec(w_s.shape, lambda i: (0, 0, 0)),
                  pl.BlockSpec(b_s.shape, lambda i: (0, 0, 0))],
        out_specs=pl.BlockSpec((tm, w), lambda i: (i, 0)),
        out_shape=jax.ShapeDtypeStruct((t, w), BF16),
        compiler_params=_cparams(("parallel",)),
        name="chunk_mlp",
    )(mlp, mlp, g, w_s, b_s)


def _merge_kernel(a0_ref, a1_ref, a2_ref, w_ref, g0_ref, g1_ref, g2_ref, o_ref):
    acc = jax.nn.sigmoid(g0_ref[...].astype(F32)) * jnp.dot(a0_ref[...], w_ref[0], preferred_element_type=F32)
    acc += jax.nn.sigmoid(g1_ref[...].astype(F32)) * jnp.dot(a1_ref[...], w_ref[1], preferred_element_type=F32)
    acc += jax.nn.sigmoid(g2_ref[...].astype(F32)) * jnp.dot(a2_ref[...], w_ref[2], preferred_element_type=F32)
    o_ref[...] = acc.astype(o_ref.dtype)


def merge_branches(o_na, o_ssm, o_mlp, w_br, gates, tm=512, tn=512):
    t, k = o_na.shape
    d = w_br.shape[2]
    nb = d // tn
    a_spec = pl.BlockSpec((tm, k), lambda i, j: (i, 0))
    return pl.pallas_call(
        _merge_kernel,
        grid=(t // tm, nb),
        in_specs=[a_spec, a_spec, a_spec,
                  pl.BlockSpec((N_BRANCH, k, tn), lambda i, j: (0, 0, j)),
                  pl.BlockSpec((tm, tn), lambda i, j: (i, j)),
                  pl.BlockSpec((tm, tn), lambda i, j: (i, nb + j)),
                  pl.BlockSpec((tm, tn), lambda i, j: (i, 2 * nb + j))],
        out_specs=pl.BlockSpec((tm, tn), lambda i, j: (i, j)),
        out_shape=jax.ShapeDtypeStruct((t, d), BF16),
        compiler_params=_cparams(("parallel", "parallel")),
        name="merge_branches",
    )(o_na, o_ssm, o_mlp, w_br, gates, gates, gates)


def _out_proj_kernel(a_ref, w_ref, x_ref, gt_ref, o_ref):
    o_ref[...] = x_ref[...] + gt_ref[...] * jnp.dot(a_ref[...], w_ref[...], preferred_element_type=F32)


def out_proj(merged, w_out, x, gt, seg_len, tm=512, tn=512):
    t, k = merged.shape
    d = w_out.shape[1]
    spt = seg_len // tm
    return pl.pallas_call(
        _out_proj_kernel,
        grid=(t // tm, d // tn),
        in_specs=[pl.BlockSpec((tm, k), lambda i, j: (i, 0)),
                  pl.BlockSpec((k, tn), lambda i, j: (0, j)),
                  pl.BlockSpec((tm, tn), lambda i, j: (i, j)),
                  pl.BlockSpec((None, 1, tn), lambda i, j: (i // spt, 0, j))],
        out_specs=pl.BlockSpec((tm, tn), lambda i, j: (i, j)),
        out_shape=jax.ShapeDtypeStruct((t, d), F32),
        compiler_params=_cparams(("parallel", "parallel")),
        name="out_proj",
    )(merged, w_out, x, gt)


def _peer_scores_kernel(wq_ref, h_ref, key_ref, o_ref):
    qv = jnp.dot(wq_ref[...], h_ref[...], preferred_element_type=F32).astype(BF16)
    dq = PEER_QDIM // 2
    for hk in range(2 * PEER_HEADS):
        o_ref[hk * PEER_KEYS:(hk + 1) * PEER_KEYS, :] = jnp.dot(
            key_ref[hk], qv[hk * dq:(hk + 1) * dq, :], preferred_element_type=F32)


def peer_scores(wq_t, h_t, keys, tl=512):
    d, t = h_t.shape
    rows = 2 * PEER_HEADS * PEER_KEYS
    return pl.pallas_call(
        _peer_scores_kernel,
        grid=(t // tl,),
        in_specs=[pl.BlockSpec(wq_t.shape, lambda i: (0, 0)),
                  pl.BlockSpec((d, tl), lambda i: (0, i)),
                  pl.BlockSpec(keys.shape, lambda i: (0, 0, 0))],
        out_specs=pl.BlockSpec((rows, tl), lambda i: (0, i)),
        out_shape=jax.ShapeDtypeStruct((rows, t), F32),
        compiler_params=_cparams(("parallel",)),
        name="peer_scores",
    )(wq_t, h_t, keys)


def _top16(s, exact):
    nk = s.shape[0]
    kio = lax.broadcasted_iota(jnp.int32, s.shape, 0).astype(F32)
    rank = jnp.full(s.shape, float(PEER_TOPK), F32)
    vals = []
    for a in range(PEER_TOPK):
        m = jnp.max(s, axis=0, keepdims=True)
        hit = s == m
        if exact:
            hit = kio == jnp.min(jnp.where(hit, kio, float(nk)), axis=0, keepdims=True)
        rank = jnp.where(hit, float(a), rank)
        s = jnp.where(hit, -jnp.inf, s)
        vals.append(m)
    return jnp.concatenate(vals, axis=0), rank


def _peer_select_heads(s_ref, rank2_ref, cnt_ref, e1_ref, e2_ref, exact):
    nk = PEER_KEYS
    k = PEER_TOPK
    bad = jnp.zeros((1, s_ref.shape[1]), F32)
    for h in range(PEER_HEADS):
        s1 = s_ref[(2 * h) * nk:(2 * h + 1) * nk, :]
        s2 = s_ref[(2 * h + 1) * nk:(2 * h + 2) * nk, :]
        tv1, rank1 = _top16(s1, exact)
        tv2, rank2 = _top16(s2, exact)
        sub = lax.broadcasted_iota(jnp.int32, (8, s1.shape[1]), 0).astype(F32)
        pieces = []
        pos_pieces = []
        for a in range(k // 2):
            nb = k // (a + 1)
            for b0 in range(0, nb, 8):
                vals = tv1[a:a + 1, :] + tv2[b0:b0 + 8, :]
                pieces.append(vals if nb - b0 >= 8 else jnp.where(sub < float(nb - b0), vals, -jnp.inf))
                pos_pieces.append(sub + float(a * k + b0))
        pieces.append(tv1[k // 2:k, :] + tv2[0:1, :])
        pos_pieces.append((sub + float(k // 2)) * float(k))
        cand = jnp.concatenate(pieces, axis=0)
        pio = jnp.concatenate(pos_pieces, axis=0)
        cnt_cells = jnp.zeros(cand.shape, F32)
        zsum = jnp.zeros((1, cand.shape[1]), F32)
        top = tv1[0:1, :] + tv2[0:1, :]
        for _ in range(k):
            m = jnp.max(cand, axis=0, keepdims=True)
            hit = cand == m
            if exact:
                hit = pio == jnp.min(jnp.where(hit, pio, float(k * k)), axis=0, keepdims=True)
            cand = jnp.where(hit, -jnp.inf, cand)
            cnt_cells = jnp.where(hit, 1.0, cnt_cells)
            zsum = zsum + jnp.exp(m - top)
        cnt = jnp.zeros(s1.shape, F32)
        for a in range(k):
            if a < k // 2:
                lo = 0 if a == 0 else 8 * (a + 1)
                cnt_a = jnp.sum(cnt_cells[lo:8 * (a + 2), :], axis=0, keepdims=True)
            else:
                r = 8 * (k // 2 + 1) + a - k // 2
                cnt_a = cnt_cells[r:r + 1, :]
            cnt = jnp.where(rank1 == float(a), cnt_a, cnt)
        if not exact:
            n1 = jnp.sum(jnp.where(rank1 < float(k), 1.0, 0.0), axis=0, keepdims=True)
            n2 = jnp.sum(jnp.where(rank2 < float(k), 1.0, 0.0), axis=0, keepdims=True)
            nc = jnp.sum(cnt_cells, axis=0, keepdims=True)
            bad = bad + jnp.abs(n1 - float(k)) + jnp.abs(n2 - float(k)) + jnp.abs(nc - float(k))
        rank2_ref[h * nk:(h + 1) * nk, :] = rank2.astype(rank2_ref.dtype)
        cnt_ref[h * nk:(h + 1) * nk, :] = cnt
        e1_ref[h * nk:(h + 1) * nk, :] = jnp.exp(s1 - tv1[0:1, :])
        e2_ref[h * nk:(h + 1) * nk, :] = (jnp.exp(s2 - tv2[0:1, :]) / zsum).astype(e2_ref.dtype)
    return bad


def _peer_select_kernel(s_ref, rank2_ref, cnt_ref, e1_ref, e2_ref):
    bad = _peer_select_heads(s_ref, rank2_ref, cnt_ref, e1_ref, e2_ref, exact=False)

    @pl.when(jnp.max(bad) > 0.0)
    def _():
        _peer_select_heads(s_ref, rank2_ref, cnt_ref, e1_ref, e2_ref, exact=True)


def peer_select(scores, tl=128):
    rows, t = scores.shape
    half = rows // 2
    spec = pl.BlockSpec((half, tl), lambda i: (0, i))
    shp = jax.ShapeDtypeStruct((half, t), F32)
    shp_b = jax.ShapeDtypeStruct((half, t), BF16)
    return pl.pallas_call(
        _peer_select_kernel,
        grid=(t // tl,),
        in_specs=[pl.BlockSpec((rows, tl), lambda i: (0, i))],
        out_specs=[spec, spec, spec, spec],
        out_shape=[shp_b, shp, shp, shp_b],
        compiler_params=_cparams(("parallel",)),
        name="peer_select",
    )(scores)


def _peer_dense_kernel(h_ref, u_ref, vta_ref, vtb_ref, rank2_ref, cnt_ref, e1_ref, e2_ref, o_ref, za_scr, zb_scr,
                       *, nstep, rows_per_half, lane_tile):
    e = pl.program_id(1)

    @pl.when(e == 0)
    def _():
        o_ref[...] = jnp.zeros_like(o_ref)
        zb_scr[...] = jnp.zeros_like(zb_scr)

    nk = PEER_KEYS
    eh = rows_per_half * nk
    nsub = rows_per_half // SUB_KEYS
    mrows = o_ref.shape[0] // nsub
    lanes = [slice(lt * lane_tile, (lt + 1) * lane_tile) for lt in range(o_ref.shape[1] // lane_tile)]
    for half, (z_new, z_old, vt_ref) in enumerate(((za_scr, zb_scr, vta_ref), (zb_scr, za_scr, vtb_ref))):
        for sc in range(nsub):
            i0 = half * rows_per_half + sc * SUB_KEYS
            acts = [jax.nn.gelu(jnp.dot(u_ref[i0 * nk:(i0 + SUB_KEYS) * nk, :], h_ref[:, ls],
                                        preferred_element_type=F32)) for ls in lanes]
            for ls in lanes:
                o_ref[sc * mrows:(sc + 1) * mrows, ls] += jnp.dot(
                    vt_ref[sc * mrows:(sc + 1) * mrows, :eh], z_old[:, ls], preferred_element_type=F32)
            for ls, act in zip(lanes, acts):
                for k in range(SUB_KEYS):
                    w = None
                    for h in range(PEER_HEADS):
                        row = h * nk + e * (2 * rows_per_half) + i0 + k
                        cnt_b = jnp.broadcast_to(cnt_ref[pl.ds(row, 1), ls], (BF16_ROWS, lane_tile))
                        e1_b = jnp.broadcast_to(e1_ref[pl.ds(row, 1), ls], (BF16_ROWS, lane_tile))
                        cnt_b = jnp.concatenate([cnt_b.astype(BF16)] * (nk // BF16_ROWS), axis=0)
                        e1_b = jnp.concatenate([e1_b.astype(BF16)] * (nk // BF16_ROWS), axis=0)
                        sel = rank2_ref[h * nk:(h + 1) * nk, ls] < cnt_b
                        term = jnp.where(sel, e2_ref[h * nk:(h + 1) * nk, ls], jnp.zeros((), BF16)) * e1_b
                        w = term if w is None else w + term
                    r0 = (sc * SUB_KEYS + k) * nk
                    z_new[r0:r0 + nk, ls] = w * act[k * nk:(k + 1) * nk, :].astype(BF16)

    @pl.when(e == nstep - 1)
    def _():
        for ls in lanes:
            o_ref[:, ls] += jnp.dot(vtb_ref[:, eh:], zb_scr[:, ls], preferred_element_type=F32)


def peer_dense(h_t, u_tab, v_t, rank2, cnt, e1, e2, tm=512, et=1024, lane_tile=256):
    d, t = h_t.shape
    ne = u_tab.shape[0]
    nstep = ne // et
    eh = et // 2
    map_spec = pl.BlockSpec((rank2.shape[0], tm), lambda i, e: (0, i))
    return pl.pallas_call(
        functools.partial(_peer_dense_kernel, nstep=nstep, rows_per_half=eh // PEER_KEYS, lane_tile=lane_tile),
        grid=(t // tm, nstep),
        in_specs=[pl.BlockSpec((d, tm), lambda i, e: (0, i)),
                  pl.BlockSpec((et, d), lambda i, e: (e, 0)),
                  pl.BlockSpec((d, eh), lambda i, e: (0, jnp.maximum(2 * e - 1, 0))),
                  pl.BlockSpec((d, et), lambda i, e: (0, e)),
                  map_spec, map_spec, map_spec, map_spec],
        out_specs=pl.BlockSpec((d, tm), lambda i, e: (0, i)),
        out_shape=jax.ShapeDtypeStruct((d, t), F32),
        scratch_shapes=[pltpu.VMEM((eh, tm), BF16), pltpu.VMEM((eh, tm), BF16)],
        compiler_params=_cparams(("parallel", "arbitrary")),
        name="peer_dense",
    )(h_t, u_tab, v_t, v_t, rank2, cnt, e1, e2)


def _peer_resid_norm_kernel(x_ref, p_ref, gt_ref, g_ref, sc_ref, sh_ref, x_out_ref, h_ref):
    x = x_ref[...] + gt_ref[...] * p_ref[...].T
    x_out_ref[...] = x
    y = x * lax.rsqrt(jnp.mean(x * x, axis=-1, keepdims=True) + EPS) * g_ref[...]
    h_ref[...] = (y * (1.0 + sc_ref[...]) + sh_ref[...]).astype(h_ref.dtype)


def _peer_resid_final_kernel(x_ref, p_ref, gt_ref, g_ref, o_ref):
    x = x_ref[...] + gt_ref[...] * p_ref[...].T
    o_ref[...] = x * lax.rsqrt(jnp.mean(x * x, axis=-1, keepdims=True) + EPS) * g_ref[...]


def peer_residual_norm(x, peer_t, gt, g, sc, sh, seg_len, tm=512):
    t, d = x.shape
    spt = seg_len // tm
    row_spec = pl.BlockSpec((tm, d), lambda i: (i, 0))
    seg_spec = pl.BlockSpec((None, 1, d), lambda i: (i // spt, 0, 0))
    in_specs = [row_spec, pl.BlockSpec((d, tm), lambda i: (0, i)), seg_spec, pl.BlockSpec((1, d), lambda i: (0, 0))]
    if sc is None:
        return pl.pallas_call(
            _peer_resid_final_kernel,
            grid=(t // tm,),
            in_specs=in_specs,
            out_specs=row_spec,
            out_shape=jax.ShapeDtypeStruct((t, d), F32),
            compiler_params=_cparams(("parallel",)),
            name="peer_resid_final",
        )(x, peer_t, gt, g)
    return pl.pallas_call(
        _peer_resid_norm_kernel,
        grid=(t // tm,),
        in_specs=in_specs + [seg_spec, seg_spec],
        out_specs=[row_spec, row_spec],
        out_shape=[jax.ShapeDtypeStruct((t, d), F32), jax.ShapeDtypeStruct((t, d), BF16)],
        compiler_params=_cparams(("parallel",)),
        name="peer_resid_norm",
    )(x, peer_t, gt, g, sc, sh)


def _layer(x, h, mod, lw, layer, depth, nseq, seq, ctx):
    sh1, sc1, gt1, sh2, sc2, gt2 = mod
    seg_len = x.shape[0] // sh1.shape[0]
    ssm = matmul(h, lw["w_ssm"], 1024, 128 * 11, name="proj_ssm")
    mlp = matmul(h, lw["w_mlp"], 1024, 512, out_dtype=BF16, name="proj_mlp")
    gates = matmul(h, lw["w_gate"], 1024, 512, out_dtype=BF16, name="proj_gate")
    if ctx[0] == "context":
        q = matmul(h, lw["w_qkv"], 1024, 512, out_dtype=BF16, name="proj_q", col0=0, n=NA_WIDTH)
        kbuf = matmul_into_layer(h, lw["w_qkv"], NA_WIDTH, NA_WIDTH, ctx[1], layer, depth, seq, 1024, 512, "proj_k")
        vbuf = matmul_into_layer(h, lw["w_qkv"], 2 * NA_WIDTH, NA_WIDTH, ctx[2], layer, depth, seq, 1024, 512, "proj_v")
        o_na = context_attention(q, kbuf, vbuf, layer, nseq, seq)
        init = None
    else:
        _, cache_k, cache_v, init = ctx
        qkv = matmul(h, lw["w_qkv"], 1024, 512, out_dtype=BF16, name="proj_qkv")
        o_na = neighbourhood_attention(qkv, cache_k, cache_v, layer, lw["na_bias"], nseq, seq)
    xc = conv_silu(ssm, lw["conv_w"], lw["conv_b"], nseq, seq)
    y, fin = ssd_scan(xc, ssm, lw["dt_bias"], lw["a_log"], init, layer)
    t = nseq * seq
    o_ssm = ssm_finish(y.reshape(2, t, SSM_INNER), xc.reshape(t, SSM_CONV_DIM), ssm, lw["d_skip"], lw["ssm_norm_g"])
    o_mlp = chunk_mlp(mlp, lw["mlp_norm_g"], lw["mlp_w_s"], lw["mlp_b_s"])
    merged = merge_branches(o_na, o_ssm, o_mlp, lw["w_br"], gates)
    x = out_proj(merged, lw["w_out"], x, gt1, seg_len)
    h2t = norm_mod(x, lw["g_ffn"], sc2, sh2, seg_len, transpose=True)
    scores = peer_scores(lw["peer_wq_t"], h2t, lw["peer_keys"])
    rank2, cnt, e1, e2 = peer_select(scores)
    peer_t = peer_dense(h2t, lw["peer_u"], lw["peer_v_t"], rank2, cnt, e1, e2)
    extras = (kbuf, vbuf, fin) if ctx[0] == "context" else ()
    return x, peer_t, extras


def _pad_lanes(v):
    pad = LANES - v.shape[-1]
    return jnp.pad(v, [(0, 0)] * (v.ndim - 1) + [(0, pad)])[..., None, :]


def kernel(x_prompt, x_sample, cache_k, cache_v, state_ssm, c, c_ctx, w_mod, b_mod, g_norm_mix, g_norm_ffn, g_norm_final, w_in, na_rel_bias, ssm_conv_w, ssm_conv_b, ssm_dt_bias, ssm_a_log, ssm_d, ssm_norm_g, mlp_norm_g, mlp_w_s, mlp_b_s, w_br_na, w_br_ssm, w_br_mlp, w_out, peer_w_q, peer_sub_keys, peer_u, peer_v):
    batch, seq, d = x_prompt.shape
    dec_batch, dec_seq, _ = x_sample.shape
    depth = w_in.shape[0]
    past = cache_k.shape[2]

    cvec = jnp.concatenate([c_ctx[None, :], c, jnp.zeros((8 - 1 - dec_batch, d), F32)], axis=0)
    mod_all = modulation(cvec, w_mod, b_mod)

    o_q = 0
    o_z = 3 * NA_WIDTH
    o_xbc = o_z + SSM_INNER
    o_dt = o_xbc + SSM_CONV_DIM
    o_u = o_dt + 2 * SSM_HEADS
    o_g = o_u + 2 * MLP_WIDTH
    dt_pad = jnp.zeros((depth, d, LANES - SSM_HEADS), F32)
    w_ssm = jnp.concatenate([w_in[:, :, o_z:o_dt], w_in[:, :, o_dt:o_dt + SSM_HEADS], dt_pad,
                             w_in[:, :, o_dt + SSM_HEADS:o_u], dt_pad], axis=-1).astype(BF16)
    w_qkv = w_in[:, :, o_q:o_z].astype(BF16)
    w_mlp = w_in[:, :, o_u:o_g].astype(BF16)
    w_gate = w_in[:, :, o_g:].astype(BF16)
    w_br = jnp.stack([w_br_na, w_br_ssm, w_br_mlp], axis=1).astype(BF16)
    w_out_b = w_out.astype(BF16)
    peer_wq_t = jnp.swapaxes(peer_w_q, 1, 2).astype(BF16)
    peer_keys = peer_sub_keys.reshape(depth, 2 * PEER_HEADS, PEER_KEYS, PEER_QDIM // 2).astype(BF16)
    peer_u_b = peer_u.astype(BF16)
    peer_v_t = jnp.swapaxes(peer_v, 1, 2).astype(BF16)
    dt_bias = _pad_lanes(ssm_dt_bias)
    a_log = _pad_lanes(ssm_a_log)
    d_skip = jnp.repeat(ssm_d, SSM_HEAD_DIM, axis=-1)[:, None, :]
    mlp_b = jnp.broadcast_to(mlp_b_s[..., None], mlp_b_s.shape + (MLP_CHUNK,))
    mlp_w = mlp_w_s.astype(BF16)

    cache_k4 = cache_k.reshape(dec_batch, depth, past, NA_WIDTH)
    cache_v4 = cache_v.reshape(dec_batch, depth, past, NA_WIDTH)

    xp = x_prompt.reshape(batch * seq, d)
    xs = x_sample.reshape(dec_batch * dec_seq, d)
    mods = []
    for l in range(depth):
        m = mod_all[l].reshape(8, 6, d)
        mods.append(([m[0:1, j][:, None, :] for j in range(6)],
                     [m[1:1 + dec_batch, j][:, None, :] for j in range(6)]))
    kbuf = vbuf = None
    new_s = []
    for l in range(depth):
        lw = {
            "g_mix": g_norm_mix[l][None, :], "g_ffn": g_norm_ffn[l][None, :],
            "w_qkv": w_qkv[l], "w_ssm": w_ssm[l], "w_mlp": w_mlp[l], "w_gate": w_gate[l],
            "na_bias": _na_bias_table(na_rel_bias[l]),
            "conv_w": ssm_conv_w[l], "conv_b": ssm_conv_b[l][None, :],
            "dt_bias": dt_bias[l], "a_log": a_log[l], "d_skip": d_skip[l],
            "ssm_norm_g": ssm_norm_g[l][None, :], "mlp_norm_g": mlp_norm_g[l][None, :],
            "mlp_w_s": mlp_w[l], "mlp_b_s": mlp_b[l],
            "w_br": w_br[l], "w_out": w_out_b[l],
            "peer_wq_t": peer_wq_t[l], "peer_keys": peer_keys[l], "peer_u": peer_u_b[l], "peer_v_t": peer_v_t[l],
        }
        mod_ctx, mod_lat = mods[l]
        if l == 0:
            hp = norm_mod(xp, lw["g_mix"], mod_ctx[1], mod_ctx[0], batch * seq)
            hs = norm_mod(xs, lw["g_mix"], mod_lat[1], mod_lat[0], dec_seq)
        xp, peer_p, (kbuf, vbuf, fin) = _layer(xp, hp, mod_ctx, lw, l, depth, batch, seq, ("context", kbuf, vbuf))
        xs, peer_s, _ = _layer(xs, hs, mod_lat, lw, l, depth, dec_batch, dec_seq,
                               ("latent", cache_k4, cache_v4, state_ssm))
        if l + 1 < depth:
            nxt_ctx, nxt_lat = mods[l + 1]
            g_next = g_norm_mix[l + 1][None, :]
            xp, hp = peer_residual_norm(xp, peer_p, mod_ctx[5], g_next, nxt_ctx[1], nxt_ctx[0], batch * seq)
            xs, hs = peer_residual_norm(xs, peer_s, mod_lat[5], g_next, nxt_lat[1], nxt_lat[0], dec_seq)
        else:
            gf = g_norm_final[None, :]
            y_prompt = peer_residual_norm(xp, peer_p, mod_ctx[5], gf, None, None, batch * seq)
            y_sample = peer_residual_norm(xs, peer_s, mod_lat[5], gf, None, None, dec_seq)
        new_s.append(fin)
    cache_shape = (batch, depth, seq, NA_HEADS, NA_HEAD_DIM)
    return (y_prompt.reshape(batch, seq, d), y_sample.reshape(dec_batch, dec_seq, d),
            kbuf.reshape(cache_shape), vbuf.reshape(cache_shape), jnp.stack(new_s, axis=1))
```

```python
import functools
import math

import numpy as np
import jax
import jax.numpy as jnp
from jax import lax
from jax.experimental import pallas as pl
from jax.experimental.pallas import tpu as pltpu

F32 = jnp.float32
BF16 = jnp.bfloat16

D_MODEL = 2048
DEPTH = 4
GRID_W = 64
EPS = 1e-6
NA_HEADS = 8
NA_HEAD_DIM = 128
NA_WIDTH = NA_HEADS * NA_HEAD_DIM
NA_WIN_ROWS = 8
NA_WIN_COLS = 16
NA_ROW_GROUP = 16
SSM_HEADS = 16
SSM_HEAD_DIM = 64
SSM_INNER = SSM_HEADS * SSM_HEAD_DIM
SSM_GROUPS = 2
SSM_STATE = 128
SSM_CHUNK = 128
SSM_BC = SSM_GROUPS * SSM_STATE
SSM_CONV_DIM = SSM_INNER + 2 * SSM_BC
MLP_GROUPS = 8
MLP_WIDTH = 1024
MLP_CHUNK = 128
PEER_HEADS = 8
PEER_KEYS = 128
PEER_EXPERTS = PEER_KEYS * PEER_KEYS
PEER_QDIM = 256
PEER_TOPK = 16
N_BRANCH = 3

LANES = 128
BF16_ROWS = 16
SUB_KEYS = 2
SSM_COLS = SSM_INNER + SSM_CONV_DIM + 2 * LANES
DT_BLOCK0 = (SSM_INNER + SSM_CONV_DIM) // LANES
VMEM_LIMIT = 60 * 1024 * 1024
NEG_BIG = -1e30


def _cparams(sem):
    return pltpu.CompilerParams(dimension_semantics=sem, vmem_limit_bytes=VMEM_LIMIT)


def _nt_dot(a, b):
    return lax.dot_general(a, b, (((1,), (1,)), ((), ())), preferred_element_type=F32)


def _tn_dot(a, b):
    return lax.dot_general(a, b, (((0,), (0,)), ((), ())), preferred_element_type=F32)


def _mod_kernel(c_ref, w_ref, b_ref, o_ref):
    c = c_ref[...]
    a = c * jax.nn.sigmoid(c)
    o_ref[...] = jnp.dot(a, w_ref[...], preferred_element_type=F32,
                         precision=lax.Precision.HIGHEST) + b_ref[...]


def modulation(cvec, w_mod, b_mod):
    depth, d, n = w_mod.shape
    tn = 1024
    return pl.pallas_call(
        _mod_kernel,
        grid=(depth, n // tn),
        in_specs=[
            pl.BlockSpec((8, d), lambda l, j: (0, 0)),
            pl.BlockSpec((None, d, tn), lambda l, j: (l, 0, j)),
            pl.BlockSpec((None, 1, tn), lambda l, j: (l, 0, j)),
        ],
        out_specs=pl.BlockSpec((None, 8, tn), lambda l, j: (l, 0, j)),
        out_shape=jax.ShapeDtypeStruct((depth, 8, n), F32),
        compiler_params=_cparams(("parallel", "parallel")),
        name="modulation",
    )(cvec, w_mod, b_mod.reshape(depth, 1, n))


def _norm_mod_kernel(x_ref, g_ref, sc_ref, sh_ref, o_ref, *, transpose):
    x = x_ref[...]
    y = x * lax.rsqrt(jnp.mean(x * x, axis=-1, keepdims=True) + EPS) * g_ref[...]
    h = y * (1.0 + sc_ref[...]) + sh_ref[...]
    if transpose:
        h = h.T
    o_ref[...] = h.astype(o_ref.dtype)


def norm_mod(x, g, sc, sh, seg_len, transpose=False, tm=512):
    t, d = x.shape
    spt = seg_len // tm
    seg_spec = pl.BlockSpec((None, 1, d), lambda i: (i // spt, 0, 0))
    if transpose:
        out_spec = pl.BlockSpec((d, tm), lambda i: (0, i))
        out_shape = jax.ShapeDtypeStruct((d, t), BF16)
    else:
        out_spec = pl.BlockSpec((tm, d), lambda i: (i, 0))
        out_shape = jax.ShapeDtypeStruct((t, d), BF16)
    return pl.pallas_call(
        functools.partial(_norm_mod_kernel, transpose=transpose),
        grid=(t // tm,),
        in_specs=[pl.BlockSpec((tm, d), lambda i: (i, 0)),
                  pl.BlockSpec((1, d), lambda i: (0, 0)), seg_spec, seg_spec],
        out_specs=out_spec,
        out_shape=out_shape,
        compiler_params=_cparams(("parallel",)),
        name="norm_mod_t" if transpose else "norm_mod",
    )(x, g, sc, sh)


def _mm_kernel(a_ref, b_ref, o_ref):
    o_ref[...] = jnp.dot(a_ref[...], b_ref[...], preferred_element_type=F32).astype(o_ref.dtype)


def matmul(a, b, tm, tn, out_dtype=F32, name="matmul", col0=0, n=None):
    m, k = a.shape
    n = b.shape[1] if n is None else n
    tm = min(tm, m)
    c0 = col0 // tn
    return pl.pallas_call(
        _mm_kernel,
        grid=(m // tm, n // tn),
        in_specs=[pl.BlockSpec((tm, k), lambda i, j: (i, 0)),
                  pl.BlockSpec((k, tn), lambda i, j: (0, c0 + j))],
        out_specs=pl.BlockSpec((tm, tn), lambda i, j: (i, j)),
        out_shape=jax.ShapeDtypeStruct((m, n), out_dtype),
        compiler_params=_cparams(("parallel", "parallel")),
        name=name,
    )(a, b)


def _mm_layer_kernel(a_ref, b_ref, *refs):
    o_ref = refs[-1]
    o_ref[...] = jnp.dot(a_ref[...], b_ref[...], preferred_element_type=F32).reshape(o_ref.shape)


def matmul_into_layer(a, b, col0, n, buf, layer, depth, seq, tm, tn, name):
    m, k = a.shape
    tm = min(tm, m)
    sp = tm // seq
    c0 = col0 // tn
    in_specs = [pl.BlockSpec((tm, k), lambda i, j: (i, 0)),
                pl.BlockSpec((k, tn), lambda i, j: (0, c0 + j))]
    args = [a, b]
    aliases = {}
    if buf is not None:
        in_specs.append(pl.BlockSpec(memory_space=pl.ANY))
        args.append(buf)
        aliases = {2: 0}
    return pl.pallas_call(
        _mm_layer_kernel,
        grid=(m // tm, n // tn),
        in_specs=in_specs,
        out_specs=pl.BlockSpec((sp, None, seq, tn), lambda i, j: (i, layer, 0, j)),
        out_shape=jax.ShapeDtypeStruct((m // seq, depth, seq, n), F32),
        input_output_aliases=aliases,
        compiler_params=_cparams(("parallel", "parallel")),
        name=name,
    )(*args)


def _ctx_attn_kernel(q_ref, k_ref, v_ref, o_ref):
    scale = NA_HEAD_DIM ** -0.5
    hd = NA_HEAD_DIM
    for h in range(NA_HEADS):
        hs = slice(h * hd, (h + 1) * hd)
        q = q_ref[:, hs].astype(BF16)
        k = k_ref[:, hs].astype(BF16)
        v = v_ref[:, hs].astype(BF16)
        s = _nt_dot(q, k) * scale
        m = jnp.max(s, axis=-1, keepdims=True)
        p = jnp.exp(s - m)
        p = p / jnp.sum(p, axis=-1, keepdims=True)
        o_ref[:, hs] = jnp.dot(p.astype(BF16), v, preferred_element_type=F32).astype(o_ref.dtype)


def context_attention(q, kbuf, vbuf, layer, nseq, seq):
    w = NA_WIDTH
    kv_spec = pl.BlockSpec((None, None, seq, w), lambda b: (b, layer, 0, 0))
    return pl.pallas_call(
        _ctx_attn_kernel,
        grid=(nseq,),
        in_specs=[pl.BlockSpec((seq, w), lambda b: (b, 0)), kv_spec, kv_spec],
        out_specs=pl.BlockSpec((seq, w), lambda b: (b, 0)),
        out_shape=jax.ShapeDtypeStruct((nseq * seq, w), BF16),
        compiler_params=_cparams(("parallel",)),
        name="ctx_attention",
    )(q, kbuf, vbuf)


def _na_kernel(q_ref, k_ref, v_ref, kc_ref, vc_ref, bias_ref, o_ref, kb_scr, vb_scr, *, rows):
    scale = NA_HEAD_DIM ** -0.5
    band = NA_WIN_ROWS * GRID_W
    kb_scr[...] = k_ref[...].astype(BF16)
    vb_scr[...] = v_ref[...].astype(BF16)
    kc = kc_ref[...].astype(BF16)
    vc = vc_ref[...].astype(BF16)

    def rows_step(it, carry):
        r0 = it * NA_ROW_GROUP
        q0 = pl.multiple_of(r0 * GRID_W, NA_ROW_GROUP * GRID_W)
        q = q_ref[pl.ds(q0, NA_ROW_GROUP * GRID_W), :].astype(BF16)
        k0s = []
        s_rows = []
        for j in range(NA_ROW_GROUP):
            r = r0 + j
            rs = jnp.clip(r - NA_WIN_ROWS // 2, 0, rows - NA_WIN_ROWS)
            k0 = pl.multiple_of(rs * GRID_W, GRID_W)
            k0s.append(k0)
            s_rows.append(_nt_dot(q[j * GRID_W:(j + 1) * GRID_W], kb_scr[pl.ds(k0, band), :]) * scale
                          + bias_ref[r - rs])
        s_win = jnp.concatenate(s_rows, axis=0)
        s_ctx = _nt_dot(q, kc) * scale
        m = jnp.maximum(jnp.max(s_win, axis=-1, keepdims=True), jnp.max(s_ctx, axis=-1, keepdims=True))
        p_win = jnp.exp(s_win - m)
        p_ctx = jnp.exp(s_ctx - m)
        den = jnp.sum(p_win, axis=-1, keepdims=True) + jnp.sum(p_ctx, axis=-1, keepdims=True)
        p_win = (p_win / den).astype(BF16)
        o_ctx = jnp.dot((p_ctx / den).astype(BF16), vc, preferred_element_type=F32)
        for j in range(NA_ROW_GROUP):
            o = jnp.dot(p_win[j * GRID_W:(j + 1) * GRID_W], vb_scr[pl.ds(k0s[j], band), :],
                        preferred_element_type=F32) + o_ctx[j * GRID_W:(j + 1) * GRID_W]
            o_ref[pl.ds(pl.multiple_of((r0 + j) * GRID_W, GRID_W), GRID_W), :] = o.astype(o_ref.dtype)
        return carry

    lax.fori_loop(0, rows // NA_ROW_GROUP, rows_step, 0)


def _na_bias_table(rel_bias):
    col = np.arange(GRID_W)
    cstart = np.clip(col - NA_WIN_COLS // 2, 0, GRID_W - NA_WIN_COLS)
    valid = (col[None, :] >= cstart[:, None]) & (col[None, :] < cstart[:, None] + NA_WIN_COLS)
    dcol = col[None, :] - col[:, None] + NA_WIN_COLS - 1
    onehot = ((np.arange(2 * NA_WIN_COLS - 1)[:, None, None] == dcol[None]) & valid[None]).astype(np.float32)
    tabc = jnp.einsum("hrk,kcd->hrcd", rel_bias, jnp.asarray(onehot), precision=lax.Precision.HIGHEST)
    tabc = tabc + jnp.asarray(np.where(valid, 0.0, NEG_BIG).astype(np.float32))
    wr = NA_WIN_ROWS
    tab = jnp.stack([tabc[:, wr - 1 - o:2 * wr - 1 - o].transpose(0, 2, 1, 3) for o in range(wr)], axis=1)
    return tab.reshape(rel_bias.shape[0], wr, GRID_W, wr * GRID_W).astype(F32)


def neighbourhood_attention(qkv, cache_k, cache_v, layer, bias_tab, nseq, seq):
    hd = NA_HEAD_DIM
    past = cache_k.shape[2]
    rows = seq // GRID_W
    band = NA_WIN_ROWS * GRID_W
    return pl.pallas_call(
        functools.partial(_na_kernel, rows=rows),
        grid=(nseq, NA_HEADS),
        in_specs=[pl.BlockSpec((seq, hd), lambda b, h: (b, h)),
                  pl.BlockSpec((seq, hd), lambda b, h: (b, NA_HEADS + h)),
                  pl.BlockSpec((seq, hd), lambda b, h: (b, 2 * NA_HEADS + h)),
                  pl.BlockSpec((None, None, past, hd), lambda b, h: (b, layer, 0, h)),
                  pl.BlockSpec((None, None, past, hd), lambda b, h: (b, layer, 0, h)),
                  pl.BlockSpec((None, NA_WIN_ROWS, GRID_W, band), lambda b, h: (h, 0, 0, 0))],
        out_specs=pl.BlockSpec((seq, hd), lambda b, h: (b, h)),
        out_shape=jax.ShapeDtypeStruct((nseq * seq, NA_WIDTH), BF16),
        scratch_shapes=[pltpu.VMEM((seq, hd), BF16), pltpu.VMEM((seq, hd), BF16)],
        compiler_params=_cparams(("parallel", "parallel")),
        name="nbr_attention",
    )(qkv, qkv, qkv, cache_k, cache_v, bias_tab)


def _conv_kernel(x_ref, w_ref, b_ref, o_ref, *, seq):
    x = x_ref[...]
    n = x.shape[0]
    pos = lax.rem(lax.broadcasted_iota(jnp.int32, x.shape, 0), seq)
    xm2 = jnp.where(pos >= 2, pltpu.roll(x, 2, 0), 0.0)
    xm1 = jnp.where(pos >= 1, pltpu.roll(x, 1, 0), 0.0)
    xp1 = jnp.where(pos < seq - 1, pltpu.roll(x, n - 1, 0), 0.0)
    w = w_ref[...]
    y = w[0:1] * xm2 + w[1:2] * xm1 + w[2:3] * x + w[3:4] * xp1 + b_ref[...]
    o_ref[...] = y * jax.nn.sigmoid(y)


def conv_silu(ssm, conv_w, conv_b, nseq, seq, rows=4096):
    tc = 256
    c0 = SSM_INNER // tc
    t = nseq * seq
    tr = min(max(rows, seq), t)
    out = pl.pallas_call(
        functools.partial(_conv_kernel, seq=seq),
        grid=(t // tr, SSM_CONV_DIM // tc),
        in_specs=[pl.BlockSpec((tr, tc), lambda s, j: (s, c0 + j)),
                  pl.BlockSpec((4, tc), lambda s, j: (0, j)),
                  pl.BlockSpec((1, tc), lambda s, j: (0, j))],
        out_specs=pl.BlockSpec((tr, tc), lambda s, j: (s, j)),
        out_shape=jax.ShapeDtypeStruct((t, SSM_CONV_DIM), F32),
        compiler_params=_cparams(("parallel", "parallel")),
        name="conv_silu",
    )(ssm, conv_w, conv_b)
    return out.reshape(nseq, seq, SSM_CONV_DIM)


def _select_columns(x, sel):
    hi = x.astype(BF16)
    r = x - hi.astype(F32)
    mid = r.astype(BF16)
    lo = (r - mid.astype(F32)).astype(BF16)
    return (jnp.dot(hi, sel, preferred_element_type=F32) + jnp.dot(mid, sel, preferred_element_type=F32)
            + jnp.dot(lo, sel, preferred_element_type=F32))


def _ssd_kernel(*refs, has_init, nchunk):
    if has_init:
        xc_ref, dt_ref, tri_ref, dtb_ref, alog_ref, sel_n_ref, sel_p_ref, init_ref, y_ref, fin_ref, st_scr = refs
    else:
        xc_ref, dt_ref, tri_ref, dtb_ref, alog_ref, sel_n_ref, sel_p_ref, y_ref, fin_ref, st_scr = refs
        init_ref = None
    c = pl.program_id(2)

    @pl.when(c == 0)
    def _():
        if has_init:
            st_scr[...] = init_ref[...]
        else:
            st_scr[...] = jnp.zeros_like(st_scr)

    p = SSM_HEAD_DIM
    n = SSM_STATE
    xc = xc_ref[...]
    dt = jax.nn.softplus(dt_ref[...] + dtb_ref[...])
    a = dt * (-jnp.exp(alog_ref[...]))
    tri = tri_ref[...]
    mask = tri > 0.5
    cum = jnp.dot(tri, a, preferred_element_type=F32, precision=lax.Precision.HIGHEST)
    cum_t = cum.T
    cum_x = _select_columns(cum, sel_n_ref[...])
    dt_x = _select_columns(dt, sel_p_ref[...])
    xd_all = xc[:, :SSM_INNER] * dt_x
    rep = SSM_HEADS // SSM_GROUPS
    gmat = []
    bmat = []
    cmat = []
    for g in range(SSM_GROUPS):
        b_g = xc[:, SSM_INNER + g * n:SSM_INNER + (g + 1) * n].astype(BF16)
        c_g = xc[:, SSM_INNER + SSM_BC + g * n:SSM_INNER + SSM_BC + (g + 1) * n].astype(BF16)
        bmat.append(b_g)
        cmat.append(c_g)
        gmat.append(_nt_dot(c_g, b_g))
    for h in range(SSM_HEADS):
        g = h // rep
        cum_h = cum_x[:, h * n:(h + 1) * n]
        decay = jnp.exp(jnp.where(mask, cum_h - cum_t[h:h + 1, :], -jnp.inf))
        xd = xd_all[:, h * p:(h + 1) * p]
        y_diag = jnp.dot((gmat[g] * decay).astype(BF16), xd.astype(BF16), preferred_element_type=F32)
        tot = jnp.min(cum_h, axis=0, keepdims=True)
        e_end = jnp.exp(tot[:, :p] - cum_h[:, :p])
        chunk_state = _tn_dot((xd * e_end).astype(BF16), bmat[g])
        prev = st_scr[h]
        y_off = _nt_dot(cmat[g], prev.astype(BF16)) * jnp.exp(cum_h[:, :p])
        y_ref[:, h * p:(h + 1) * p] = y_diag + y_off
        st_scr[h] = prev * jnp.exp(tot) + chunk_state

    @pl.when(c == nchunk - 1)
    def _():
        fin_ref[...] = st_scr[...]


def ssd_scan(xc, ssm, dt_bias, a_log, init, layer):
    nseq, seq, _ = xc.shape
    q = SSM_CHUNK
    nc = seq // q
    idx = np.arange(q)
    tri = jnp.asarray(np.stack([idx[None, :] <= idx[:, None], idx[None, :] >= idx[:, None]]).astype(np.float32))

    def cc(d, c):
        return c + d * (nc - 1 - 2 * c)

    in_specs = [pl.BlockSpec((None, q, SSM_CONV_DIM), lambda d, s, c: (s, cc(d, c), 0)),
                pl.BlockSpec((None, q, LANES), lambda d, s, c: (s, cc(d, c), DT_BLOCK0 + d)),
                pl.BlockSpec((None, q, q), lambda d, s, c: (d, 0, 0)),
                pl.BlockSpec((None, 1, LANES), lambda d, s, c: (d, 0, 0)),
                pl.BlockSpec((None, 1, LANES), lambda d, s, c: (d, 0, 0)),
                pl.BlockSpec((LANES, SSM_HEADS * SSM_STATE), lambda d, s, c: (0, 0)),
                pl.BlockSpec((LANES, SSM_INNER), lambda d, s, c: (0, 0))]
    head = np.arange(LANES)[:, None]
    sel_n = jnp.asarray(head == np.arange(SSM_HEADS * SSM_STATE)[None, :] // SSM_STATE, BF16)
    sel_p = jnp.asarray(head == np.arange(SSM_INNER)[None, :] // SSM_HEAD_DIM, BF16)
    args = [xc, ssm.reshape(nseq, seq, SSM_COLS), tri, dt_bias, a_log, sel_n, sel_p]
    if init is not None:
        in_specs.append(pl.BlockSpec((None, None, None, SSM_HEADS, SSM_HEAD_DIM, SSM_STATE),
                                     lambda d, s, c: (s, layer, d, 0, 0, 0)))
        args.append(init)
    return pl.pallas_call(
        functools.partial(_ssd_kernel, has_init=init is not None, nchunk=nc),
        grid=(2, nseq, nc),
        in_specs=in_specs,
        out_specs=[pl.BlockSpec((None, None, q, SSM_INNER), lambda d, s, c: (d, s, cc(d, c), 0)),
                   pl.BlockSpec((None, None, SSM_HEADS, SSM_HEAD_DIM, SSM_STATE), lambda d, s, c: (s, d, 0, 0, 0))],
        out_shape=[jax.ShapeDtypeStruct((2, nseq, seq, SSM_INNER), F32),
                   jax.ShapeDtypeStruct((nseq, 2, SSM_HEADS, SSM_HEAD_DIM, SSM_STATE), F32)],
        scratch_shapes=[pltpu.VMEM((SSM_HEADS, SSM_HEAD_DIM, SSM_STATE), F32)],
        compiler_params=_cparams(("parallel", "parallel", "arbitrary")),
        name="ssd_scan",
    )(*args)


def _ssm_finish_kernel(y_ref, xs_ref, z_ref, dsk_ref, g_ref, o_ref):
    z = z_ref[...]
    y = (y_ref[0] + y_ref[1] + dsk_ref[...] * xs_ref[...]) * (z * jax.nn.sigmoid(z))
    y = y * lax.rsqrt(jnp.mean(y * y, axis=-1, keepdims=True) + EPS) * g_ref[...]
    o_ref[...] = y.astype(o_ref.dtype)


def ssm_finish(y, xc, ssm, d_skip, g, tm=512):
    t = ssm.shape[0]
    w = SSM_INNER
    return pl.pallas_call(
        _ssm_finish_kernel,
        grid=(t // tm,),
        in_specs=[pl.BlockSpec((2, tm, w), lambda i: (0, i, 0)),
                  pl.BlockSpec((tm, w), lambda i: (i, 0)),
                  pl.BlockSpec((tm, w), lambda i: (i, 0)),
                  pl.BlockSpec((1, w), lambda i: (0, 0)),
                  pl.BlockSpec((1, w), lambda i: (0, 0))],
        out_specs=pl.BlockSpec((tm, w), lambda i: (i, 0)),
        out_shape=jax.ShapeDtypeStruct((t, w), BF16),
        compiler_params=_cparams(("parallel",)),
        name="ssm_finish",
    )(y, xc, ssm, d_skip, g)


def _gmlp_kernel(u_ref, v_ref, g_ref, w_ref, b_ref, o_ref):
    v = jax.nn.gelu(v_ref[...].astype(F32))
    v = (v * lax.rsqrt(jnp.mean(v * v, axis=-1, keepdims=True) + EPS) * g_ref[...]).astype(BF16)
    q = MLP_CHUNK
    gw = MLP_WIDTH // MLP_GROUPS
    for ch in range(v.shape[0] // q):
        for g in range(MLP_GROUPS):
            mixed = jnp.dot(w_ref[g], v[ch * q:(ch + 1) * q, g * gw:(g + 1) * gw],
                            preferred_element_type=F32) + b_ref[g]
            u = jax.nn.gelu(u_ref[ch * q:(ch + 1) * q, g * gw:(g + 1) * gw].astype(F32))
            o_ref[ch * q:(ch + 1) * q, g * gw:(g + 1) * gw] = (u * mixed).astype(o_ref.dtype)


def chunk_mlp(mlp, g, w_s, b_s, tm=512):
    t = mlp.shape[0]
    w = MLP_WIDTH
    return pl.pallas_call(
        _gmlp_kernel,
        grid=(t // tm,),
        in_specs=[pl.BlockSpec((tm, w), lambda i: (i, 0)),
                  pl.BlockSpec((tm, w), lambda i: (i, 1)),
                  pl.BlockSpec((1, w), lambda i: (0, 0)),
                  pl.BlockSpec(w_s.shape, lambda i: (0, 0, 0)),
                  pl.BlockSpec(b_s.shape, lambda i: (0, 0, 0))],
        out_specs=pl.BlockSpec((tm, w), lambda i: (i, 0)),
        out_shape=jax.ShapeDtypeStruct((t, w), BF16),
        compiler_params=_cparams(("parallel",)),
        name="chunk_mlp",
    )(mlp, mlp, g, w_s, b_s)


def _merge_kernel(a0_ref, a1_ref, a2_ref, w_ref, g0_ref, g1_ref, g2_ref, o_ref):
    acc = jax.nn.sigmoid(g0_ref[...].astype(F32)) * jnp.dot(a0_ref[...], w_ref[0], preferred_element_type=F32)
    acc += jax.nn.sigmoid(g1_ref[...].astype(F32)) * jnp.dot(a1_ref[...], w_ref[1], preferred_element_type=F32)
    acc += jax.nn.sigmoid(g2_ref[...].astype(F32)) * jnp.dot(a2_ref[...], w_ref[2], preferred_element_type=F32)
    o_ref[...] = acc.astype(o_ref.dtype)


def merge_branches(o_na, o_ssm, o_mlp, w_br, gates, tm=1024, tn=512):
    t, k = o_na.shape
    tm = min(tm, t)
    d = w_br.shape[2]
    nb = d // tn
    a_spec = pl.BlockSpec((tm, k), lambda i, j: (i, 0))
    return pl.pallas_call(
        _merge_kernel,
        grid=(t // tm, nb),
        in_specs=[a_spec, a_spec, a_spec,
                  pl.BlockSpec((N_BRANCH, k, tn), lambda i, j: (0, 0, j)),
                  pl.BlockSpec((tm, tn), lambda i, j: (i, j)),
                  pl.BlockSpec((tm, tn), lambda i, j: (i, nb + j)),
                  pl.BlockSpec((tm, tn), lambda i, j: (i, 2 * nb + j))],
        out_specs=pl.BlockSpec((tm, tn), lambda i, j: (i, j)),
        out_shape=jax.ShapeDtypeStruct((t, d), BF16),
        compiler_params=_cparams(("parallel", "parallel")),
        name="merge_branches",
    )(o_na, o_ssm, o_mlp, w_br, gates, gates, gates)


def _out_proj_kernel(a_ref, w_ref, x_ref, gt_ref, o_ref):
    o_ref[...] = x_ref[...] + gt_ref[...] * jnp.dot(a_ref[...], w_ref[...], preferred_element_type=F32)


def out_proj(merged, w_out, x, gt, seg_len, tm=2048, tn=512):
    t, k = merged.shape
    d = w_out.shape[1]
    tm = min(tm, seg_len)
    spt = seg_len // tm
    return pl.pallas_call(
        _out_proj_kernel,
        grid=(t // tm, d // tn),
        in_specs=[pl.BlockSpec((tm, k), lambda i, j: (i, 0)),
                  pl.BlockSpec((k, tn), lambda i, j: (0, j)),
                  pl.BlockSpec((tm, tn), lambda i, j: (i, j)),
                  pl.BlockSpec((None, 1, tn), lambda i, j: (i // spt, 0, j))],
        out_specs=pl.BlockSpec((tm, tn), lambda i, j: (i, j)),
        out_shape=jax.ShapeDtypeStruct((t, d), F32),
        compiler_params=_cparams(("parallel", "parallel")),
        name="out_proj",
    )(merged, w_out, x, gt)


def _peer_scores_kernel(wq_ref, h_ref, key_ref, o_ref):
    qv = jnp.dot(wq_ref[...], h_ref[...], preferred_element_type=F32).astype(BF16)
    dq = PEER_QDIM // 2
    for hk in range(2 * PEER_HEADS):
        o_ref[hk * PEER_KEYS:(hk + 1) * PEER_KEYS, :] = jnp.dot(
            key_ref[hk], qv[hk * dq:(hk + 1) * dq, :], preferred_element_type=F32)


def peer_scores(wq_t, h_t, keys, tl=512):
    d, t = h_t.shape
    rows = 2 * PEER_HEADS * PEER_KEYS
    return pl.pallas_call(
        _peer_scores_kernel,
        grid=(t // tl,),
        in_specs=[pl.BlockSpec(wq_t.shape, lambda i: (0, 0)),
                  pl.BlockSpec((d, tl), lambda i: (0, i)),
                  pl.BlockSpec(keys.shape, lambda i: (0, 0, 0))],
        out_specs=pl.BlockSpec((rows, tl), lambda i: (0, i)),
        out_shape=jax.ShapeDtypeStruct((rows, t), F32),
        compiler_params=_cparams(("parallel",)),
        name="peer_scores",
    )(wq_t, h_t, keys)


def _top16(s, exact):
    nk = s.shape[0]
    kio = lax.broadcasted_iota(jnp.int32, s.shape, 0).astype(F32)
    rank = jnp.full(s.shape, float(PEER_TOPK), F32)
    vals = []
    for a in range(PEER_TOPK):
        m = jnp.max(s, axis=0, keepdims=True)
        hit = s == m
        if exact:
            hit = kio == jnp.min(jnp.where(hit, kio, float(nk)), axis=0, keepdims=True)
        rank = jnp.where(hit, float(a), rank)
        s = jnp.where(hit, -jnp.inf, s)
        vals.append(m)
    return jnp.concatenate(vals, axis=0), rank


def _peer_select_heads(s_ref, rank2_ref, cnt_ref, e1_ref, e2_ref, exact):
    nk = PEER_KEYS
    k = PEER_TOPK
    bad = jnp.zeros((1, s_ref.shape[1]), F32)
    for h in range(PEER_HEADS):
        s1 = s_ref[(2 * h) * nk:(2 * h + 1) * nk, :]
        s2 = s_ref[(2 * h + 1) * nk:(2 * h + 2) * nk, :]
        tv1, rank1 = _top16(s1, exact)
        tv2, rank2 = _top16(s2, exact)
        sub = lax.broadcasted_iota(jnp.int32, (8, s1.shape[1]), 0).astype(F32)
        pieces = []
        pos_pieces = []
        for a in range(k // 2):
            nb = k // (a + 1)
            for b0 in range(0, nb, 8):
                vals = tv1[a:a + 1, :] + tv2[b0:b0 + 8, :]
                pieces.append(vals if nb - b0 >= 8 else jnp.where(sub < float(nb - b0), vals, -jnp.inf))
                pos_pieces.append(sub + float(a * k + b0))
        pieces.append(tv1[k // 2:k, :] + tv2[0:1, :])
        pos_pieces.append((sub + float(k // 2)) * float(k))
        cand = jnp.concatenate(pieces, axis=0)
        pio = jnp.concatenate(pos_pieces, axis=0)
        cnt_cells = jnp.zeros(cand.shape, F32)
        zsum = jnp.zeros((1, cand.shape[1]), F32)
        top = tv1[0:1, :] + tv2[0:1, :]
        for _ in range(k):
            m = jnp.max(cand, axis=0, keepdims=True)
            hit = cand == m
            if exact:
                hit = pio == jnp.min(jnp.where(hit, pio, float(k * k)), axis=0, keepdims=True)
            cand = jnp.where(hit, -jnp.inf, cand)
            cnt_cells = jnp.where(hit, 1.0, cnt_cells)
            zsum = zsum + jnp.exp(m - top)
        cnt = jnp.zeros(s1.shape, F32)
        for a in range(k):
            if a < k // 2:
                lo = 0 if a == 0 else 8 * (a + 1)
                cnt_a = jnp.sum(cnt_cells[lo:8 * (a + 2), :], axis=0, keepdims=True)
            else:
                r = 8 * (k // 2 + 1) + a - k // 2
                cnt_a = cnt_cells[r:r + 1, :]
            cnt = jnp.where(rank1 == float(a), cnt_a, cnt)
        if not exact:
            n1 = jnp.sum(jnp.where(rank1 < float(k), 1.0, 0.0), axis=0, keepdims=True)
            n2 = jnp.sum(jnp.where(rank2 < float(k), 1.0, 0.0), axis=0, keepdims=True)
            nc = jnp.sum(cnt_cells, axis=0, keepdims=True)
            bad = bad + jnp.abs(n1 - float(k)) + jnp.abs(n2 - float(k)) + jnp.abs(nc - float(k))
        rank2_ref[h * nk:(h + 1) * nk, :] = rank2.astype(rank2_ref.dtype)
        cnt_ref[h * nk:(h + 1) * nk, :] = cnt
        e1_ref[h * nk:(h + 1) * nk, :] = jnp.exp(s1 - tv1[0:1, :])
        e2_ref[h * nk:(h + 1) * nk, :] = (jnp.exp(s2 - tv2[0:1, :]) / zsum).astype(e2_ref.dtype)
    return bad


def _peer_select_kernel(s_ref, rank2_ref, cnt_ref, e1_ref, e2_ref):
    bad = _peer_select_heads(s_ref, rank2_ref, cnt_ref, e1_ref, e2_ref, exact=False)

    @pl.when(jnp.max(bad) > 0.0)
    def _():
        _peer_select_heads(s_ref, rank2_ref, cnt_ref, e1_ref, e2_ref, exact=True)


def peer_select(scores, tl=128):
    rows, t = scores.shape
    half = rows // 2
    spec = pl.BlockSpec((half, tl), lambda i: (0, i))
    shp = jax.ShapeDtypeStruct((half, t), F32)
    shp_b = jax.ShapeDtypeStruct((half, t), BF16)
    return pl.pallas_call(
        _peer_select_kernel,
        grid=(t // tl,),
        in_specs=[pl.BlockSpec((rows, tl), lambda i: (0, i))],
        out_specs=[spec, spec, spec, spec],
        out_shape=[shp_b, shp, shp, shp_b],
        compiler_params=_cparams(("parallel",)),
        name="peer_select",
    )(scores)


def _peer_dense_kernel(h_ref, u_ref, vta_ref, vtb_ref, rank2_ref, cnt_ref, e1_ref, e2_ref, o_ref, za_scr, zb_scr,
                       *, nstep, rows_per_half, lane_tile):
    e = pl.program_id(1)

    @pl.when(e == 0)
    def _():
        o_ref[...] = jnp.zeros_like(o_ref)
        zb_scr[...] = jnp.zeros_like(zb_scr)

    nk = PEER_KEYS
    eh = rows_per_half * nk
    nsub = rows_per_half // SUB_KEYS
    mrows = o_ref.shape[0] // nsub
    lanes = [slice(lt * lane_tile, (lt + 1) * lane_tile) for lt in range(o_ref.shape[1] // lane_tile)]
    for half, (z_new, z_old, vt_ref) in enumerate(((za_scr, zb_scr, vta_ref), (zb_scr, za_scr, vtb_ref))):
        for sc in range(nsub):
            i0 = half * rows_per_half + sc * SUB_KEYS
            acts = [jax.nn.gelu(jnp.dot(u_ref[i0 * nk:(i0 + SUB_KEYS) * nk, :], h_ref[:, ls],
                                        preferred_element_type=F32)) for ls in lanes]
            for ls in lanes:
                o_ref[sc * mrows:(sc + 1) * mrows, ls] += jnp.dot(
                    vt_ref[sc * mrows:(sc + 1) * mrows, :eh], z_old[:, ls], preferred_element_type=F32)
            for ls, act in zip(lanes, acts):
                for k in range(SUB_KEYS):
                    w = None
                    for h in range(PEER_HEADS):
                        row = h * nk + e * (2 * rows_per_half) + i0 + k
                        cnt_b = jnp.broadcast_to(cnt_ref[pl.ds(row, 1), ls], (BF16_ROWS, lane_tile))
                        e1_b = jnp.broadcast_to(e1_ref[pl.ds(row, 1), ls], (BF16_ROWS, lane_tile))
                        cnt_b = jnp.concatenate([cnt_b.astype(BF16)] * (nk // BF16_ROWS), axis=0)
                        e1_b = jnp.concatenate([e1_b.astype(BF16)] * (nk // BF16_ROWS), axis=0)
                        sel = rank2_ref[h * nk:(h + 1) * nk, ls] < cnt_b
                        term = jnp.where(sel, e2_ref[h * nk:(h + 1) * nk, ls], jnp.zeros((), BF16)) * e1_b
                        w = term if w is None else w + term
                    r0 = (sc * SUB_KEYS + k) * nk
                    z_new[r0:r0 + nk, ls] = w * act[k * nk:(k + 1) * nk, :].astype(BF16)

    @pl.when(e == nstep - 1)
    def _():
        for ls in lanes:
            o_ref[:, ls] += jnp.dot(vtb_ref[:, eh:], zb_scr[:, ls], preferred_element_type=F32)


def peer_dense(h_t, u_tab, v_t, rank2, cnt, e1, e2, tm=512, et=1024, lane_tile=256):
    d, t = h_t.shape
    ne = u_tab.shape[0]
    nstep = ne // et
    eh = et // 2
    map_spec = pl.BlockSpec((rank2.shape[0], tm), lambda i, e: (0, i))
    return pl.pallas_call(
        functools.partial(_peer_dense_kernel, nstep=nstep, rows_per_half=eh // PEER_KEYS, lane_tile=lane_tile),
        grid=(t // tm, nstep),
        in_specs=[pl.BlockSpec((d, tm), lambda i, e: (0, i)),
                  pl.BlockSpec((et, d), lambda i, e: (e, 0)),
                  pl.BlockSpec((d, eh), lambda i, e: (0, jnp.maximum(2 * e - 1, 0))),
                  pl.BlockSpec((d, et), lambda i, e: (0, e)),
                  map_spec, map_spec, map_spec, map_spec],
        out_specs=pl.BlockSpec((d, tm), lambda i, e: (0, i)),
        out_shape=jax.ShapeDtypeStruct((d, t), F32),
        scratch_shapes=[pltpu.VMEM((eh, tm), BF16), pltpu.VMEM((eh, tm), BF16)],
        compiler_params=_cparams(("parallel", "arbitrary")),
        name="peer_dense",
    )(h_t, u_tab, v_t, v_t, rank2, cnt, e1, e2)


def _peer_resid_norm_kernel(x_ref, p_ref, gt_ref, g_ref, sc_ref, sh_ref, x_out_ref, h_ref):
    x = x_ref[...] + gt_ref[...] * p_ref[...].T
    x_out_ref[...] = x
    y = x * lax.rsqrt(jnp.mean(x * x, axis=-1, keepdims=True) + EPS) * g_ref[...]
    h_ref[...] = (y * (1.0 + sc_ref[...]) + sh_ref[...]).astype(h_ref.dtype)


def _peer_resid_final_kernel(x_ref, p_ref, gt_ref, g_ref, o_ref):
    x = x_ref[...] + gt_ref[...] * p_ref[...].T
    o_ref[...] = x * lax.rsqrt(jnp.mean(x * x, axis=-1, keepdims=True) + EPS) * g_ref[...]


def peer_residual_norm(x, peer_t, gt, g, sc, sh, seg_len, tm=512):
    t, d = x.shape
    spt = seg_len // tm
    row_spec = pl.BlockSpec((tm, d), lambda i: (i, 0))
    seg_spec = pl.BlockSpec((None, 1, d), lambda i: (i // spt, 0, 0))
    in_specs = [row_spec, pl.BlockSpec((d, tm), lambda i: (0, i)), seg_spec, pl.BlockSpec((1, d), lambda i: (0, 0))]
    if sc is None:
        return pl.pallas_call(
            _peer_resid_final_kernel,
            grid=(t // tm,),
            in_specs=in_specs,
            out_specs=row_spec,
            out_shape=jax.ShapeDtypeStruct((t, d), F32),
            compiler_params=_cparams(("parallel",)),
            name="peer_resid_final",
        )(x, peer_t, gt, g)
    return pl.pallas_call(
        _peer_resid_norm_kernel,
        grid=(t // tm,),
        in_specs=in_specs + [seg_spec, seg_spec],
        out_specs=[row_spec, row_spec],
        out_shape=[jax.ShapeDtypeStruct((t, d), F32), jax.ShapeDtypeStruct((t, d), BF16)],
        compiler_params=_cparams(("parallel",)),
        name="peer_resid_norm",
    )(x, peer_t, gt, g, sc, sh)


def _layer(x, h, mod, lw, layer, depth, nseq, seq, ctx):
    sh1, sc1, gt1, sh2, sc2, gt2 = mod
    seg_len = x.shape[0] // sh1.shape[0]
    ssm = matmul(h, lw["w_ssm"], 1024, 128 * 11, name="proj_ssm")
    mlp = matmul(h, lw["w_mlp"], 1024, 512, out_dtype=BF16, name="proj_mlp")
    gates = matmul(h, lw["w_gate"], 1024, 512, out_dtype=BF16, name="proj_gate")
    if ctx[0] == "context":
        q = matmul(h, lw["w_qkv"], 1024, 512, out_dtype=BF16, name="proj_q", col0=0, n=NA_WIDTH)
        kbuf = matmul_into_layer(h, lw["w_qkv"], NA_WIDTH, NA_WIDTH, ctx[1], layer, depth, seq, 1024, 512, "proj_k")
        vbuf = matmul_into_layer(h, lw["w_qkv"], 2 * NA_WIDTH, NA_WIDTH, ctx[2], layer, depth, seq, 1024, 512, "proj_v")
        o_na = context_attention(q, kbuf, vbuf, layer, nseq, seq)
        init = None
    else:
        _, cache_k, cache_v, init = ctx
        qkv = matmul(h, lw["w_qkv"], 1024, 512, out_dtype=BF16, name="proj_qkv")
        o_na = neighbourhood_attention(qkv, cache_k, cache_v, layer, lw["na_bias"], nseq, seq)
    xc = conv_silu(ssm, lw["conv_w"], lw["conv_b"], nseq, seq)
    y, fin = ssd_scan(xc, ssm, lw["dt_bias"], lw["a_log"], init, layer)
    t = nseq * seq
    o_ssm = ssm_finish(y.reshape(2, t, SSM_INNER), xc.reshape(t, SSM_CONV_DIM), ssm, lw["d_skip"], lw["ssm_norm_g"])
    o_mlp = chunk_mlp(mlp, lw["mlp_norm_g"], lw["mlp_w_s"], lw["mlp_b_s"])
    merged = merge_branches(o_na, o_ssm, o_mlp, lw["w_br"], gates)
    x = out_proj(merged, lw["w_out"], x, gt1, seg_len)
    h2t = norm_mod(x, lw["g_ffn"], sc2, sh2, seg_len, transpose=True)
    scores = peer_scores(lw["peer_wq_t"], h2t, lw["peer_keys"])
    rank2, cnt, e1, e2 = peer_select(scores)
    peer_t = peer_dense(h2t, lw["peer_u"], lw["peer_v_t"], rank2, cnt, e1, e2)
    extras = (kbuf, vbuf, fin) if ctx[0] == "context" else ()
    return x, peer_t, extras


def _pad_lanes(v):
    pad = LANES - v.shape[-1]
    return jnp.pad(v, [(0, 0)] * (v.ndim - 1) + [(0, pad)])[..., None, :]


def kernel(x_prompt, x_sample, cache_k, cache_v, state_ssm, c, c_ctx, w_mod, b_mod, g_norm_mix, g_norm_ffn, g_norm_final, w_in, na_rel_bias, ssm_conv_w, ssm_conv_b, ssm_dt_bias, ssm_a_log, ssm_d, ssm_norm_g, mlp_norm_g, mlp_w_s, mlp_b_s, w_br_na, w_br_ssm, w_br_mlp, w_out, peer_w_q, peer_sub_keys, peer_u, peer_v):
    batch, seq, d = x_prompt.shape
    dec_batch, dec_seq, _ = x_sample.shape
    depth = w_in.shape[0]
    past = cache_k.shape[2]

    cvec = jnp.concatenate([c_ctx[None, :], c, jnp.zeros((8 - 1 - dec_batch, d), F32)], axis=0)
    mod_all = modulation(cvec, w_mod, b_mod)

    o_q = 0
    o_z = 3 * NA_WIDTH
    o_xbc = o_z + SSM_INNER
    o_dt = o_xbc + SSM_CONV_DIM
    o_u = o_dt + 2 * SSM_HEADS
    o_g = o_u + 2 * MLP_WIDTH
    dt_pad = jnp.zeros((depth, d, LANES - SSM_HEADS), F32)
    w_ssm = jnp.concatenate([w_in[:, :, o_z:o_dt], w_in[:, :, o_dt:o_dt + SSM_HEADS], dt_pad,
                             w_in[:, :, o_dt + SSM_HEADS:o_u], dt_pad], axis=-1).astype(BF16)
    w_qkv = w_in[:, :, o_q:o_z].astype(BF16)
    w_mlp = w_in[:, :, o_u:o_g].astype(BF16)
    w_gate = w_in[:, :, o_g:].astype(BF16)
    w_br = jnp.stack([w_br_na, w_br_ssm, w_br_mlp], axis=1).astype(BF16)
    w_out_b = w_out.astype(BF16)
    peer_wq_t = jnp.swapaxes(peer_w_q, 1, 2).astype(BF16)
    peer_keys = peer_sub_keys.reshape(depth, 2 * PEER_HEADS, PEER_KEYS, PEER_QDIM // 2).astype(BF16)
    peer_u_b = peer_u.astype(BF16)
    peer_v_t = jnp.swapaxes(peer_v, 1, 2).astype(BF16)
    dt_bias = _pad_lanes(ssm_dt_bias)
    a_log = _pad_lanes(ssm_a_log)
    d_skip = jnp.repeat(ssm_d, SSM_HEAD_DIM, axis=-1)[:, None, :]
    mlp_b = jnp.broadcast_to(mlp_b_s[..., None], mlp_b_s.shape + (MLP_CHUNK,))
    mlp_w = mlp_w_s.astype(BF16)

    cache_k4 = cache_k.reshape(dec_batch, depth, past, NA_WIDTH)
    cache_v4 = cache_v.reshape(dec_batch, depth, past, NA_WIDTH)

    xp = x_prompt.reshape(batch * seq, d)
    xs = x_sample.reshape(dec_batch * dec_seq, d)
    mods = []
    for l in range(depth):
        m = mod_all[l].reshape(8, 6, d)
        mods.append(([m[0:1, j][:, None, :] for j in range(6)],
                     [m[1:1 + dec_batch, j][:, None, :] for j in range(6)]))
    kbuf = vbuf = None
    new_s = []
    for l in range(depth):
        lw = {
            "g_mix": g_norm_mix[l][None, :], "g_ffn": g_norm_ffn[l][None, :],
            "w_qkv": w_qkv[l], "w_ssm": w_ssm[l], "w_mlp": w_mlp[l], "w_gate": w_gate[l],
            "na_bias": _na_bias_table(na_rel_bias[l]),
            "conv_w": ssm_conv_w[l], "conv_b": ssm_conv_b[l][None, :],
            "dt_bias": dt_bias[l], "a_log": a_log[l], "d_skip": d_skip[l],
            "ssm_norm_g": ssm_norm_g[l][None, :], "mlp_norm_g": mlp_norm_g[l][None, :],
            "mlp_w_s": mlp_w[l], "mlp_b_s": mlp_b[l],
            "w_br": w_br[l], "w_out": w_out_b[l],
            "peer_wq_t": peer_wq_t[l], "peer_keys": peer_keys[l], "peer_u": peer_u_b[l], "peer_v_t": peer_v_t[l],
        }
        mod_ctx, mod_lat = mods[l]
        if l == 0:
            hp = norm_mod(xp, lw["g_mix"], mod_ctx[1], mod_ctx[0], batch * seq)
            hs = norm_mod(xs, lw["g_mix"], mod_lat[1], mod_lat[0], dec_seq)
        xp, peer_p, (kbuf, vbuf, fin) = _layer(xp, hp, mod_ctx, lw, l, depth, batch, seq, ("context", kbuf, vbuf))
        xs, peer_s, _ = _layer(xs, hs, mod_lat, lw, l, depth, dec_batch, dec_seq,
                               ("latent", cache_k4, cache_v4, state_ssm))
        if l + 1 < depth:
            nxt_ctx, nxt_lat = mods[l + 1]
            g_next = g_norm_mix[l + 1][None, :]
            xp, hp = peer_residual_norm(xp, peer_p, mod_ctx[5], g_next, nxt_ctx[1], nxt_ctx[0], batch * seq)
            xs, hs = peer_residual_norm(xs, peer_s, mod_lat[5], g_next, nxt_lat[1], nxt_lat[0], dec_seq)
        else:
            gf = g_norm_final[None, :]
            y_prompt = peer_residual_norm(xp, peer_p, mod_ctx[5], gf, None, None, batch * seq)
            y_sample = peer_residual_norm(xs, peer_s, mod_lat[5], gf, None, None, dec_seq)
        new_s.append(fin)
    cache_shape = (batch, depth, seq, NA_HEADS, NA_HEAD_DIM)
    return (y_prompt.reshape(batch, seq, d), y_sample.reshape(dec_batch, dec_seq, d),
            kbuf.reshape(cache_shape), vbuf.reshape(cache_shape), jnp.stack(new_s, axis=1))
```

```python
import functools

import numpy as np
import jax
import jax.numpy as jnp
from jax import lax
from jax.experimental import pallas as pl
from jax.experimental.pallas import tpu as pltpu

F32 = jnp.float32
BF16 = jnp.bfloat16

D_MODEL = 2048
DEPTH = 4
GRID_W = 64
EPS = 1e-6
NA_HEADS = 8
NA_HEAD_DIM = 128
NA_WIDTH = NA_HEADS * NA_HEAD_DIM
NA_WIN_ROWS = 8
NA_WIN_COLS = 16
NA_ROW_GROUP = 16
SSM_HEADS = 16
SSM_HEAD_DIM = 64
SSM_INNER = SSM_HEADS * SSM_HEAD_DIM
SSM_GROUPS = 2
SSM_STATE = 128
SSM_CHUNK = 128
SSM_BC = SSM_GROUPS * SSM_STATE
SSM_CONV_DIM = SSM_INNER + 2 * SSM_BC
MLP_GROUPS = 8
MLP_WIDTH = 1024
MLP_CHUNK = 128
PEER_HEADS = 8
PEER_KEYS = 128
PEER_EXPERTS = PEER_KEYS * PEER_KEYS
PEER_QDIM = 256
PEER_TOPK = 16
N_BRANCH = 3

LANES = 128
BF16_ROWS = 16
SUB_KEYS = 2
SSM_COLS = SSM_INNER + SSM_CONV_DIM + 2 * LANES
DT_BLOCK0 = (SSM_INNER + SSM_CONV_DIM) // LANES
VMEM_LIMIT = 60 * 1024 * 1024
NEG_BIG = -1e30


def _cparams(sem):
    return pltpu.CompilerParams(dimension_semantics=sem, vmem_limit_bytes=VMEM_LIMIT)


def _nt_dot(a, b):
    return lax.dot_general(a, b, (((1,), (1,)), ((), ())), preferred_element_type=F32)


def _tn_dot(a, b):
    return lax.dot_general(a, b, (((0,), (0,)), ((), ())), preferred_element_type=F32)


def _mod_kernel(c_ref, w_ref, b_ref, o_ref):
    c = c_ref[...]
    a = c * jax.nn.sigmoid(c)
    o_ref[...] = jnp.dot(a, w_ref[...], preferred_element_type=F32,
                         precision=lax.Precision.HIGHEST) + b_ref[...]


def modulation(cvec, w_mod, b_mod):
    depth, d, n = w_mod.shape
    tn = 1024
    return pl.pallas_call(
        _mod_kernel,
        grid=(depth, n // tn),
        in_specs=[
            pl.BlockSpec((8, d), lambda l, j: (0, 0)),
            pl.BlockSpec((None, d, tn), lambda l, j: (l, 0, j)),
            pl.BlockSpec((None, 1, tn), lambda l, j: (l, 0, j)),
        ],
        out_specs=pl.BlockSpec((None, 8, tn), lambda l, j: (l, 0, j)),
        out_shape=jax.ShapeDtypeStruct((depth, 8, n), F32),
        compiler_params=_cparams(("parallel", "parallel")),
        name="modulation",
    )(cvec, w_mod, b_mod.reshape(depth, 1, n))


def _norm_mod_kernel(x_ref, g_ref, sc_ref, sh_ref, o_ref, *, transpose):
    x = x_ref[...]
    y = x * lax.rsqrt(jnp.mean(x * x, axis=-1, keepdims=True) + EPS) * g_ref[...]
    h = y * (1.0 + sc_ref[...]) + sh_ref[...]
    if transpose:
        h = h.T
    o_ref[...] = h.astype(o_ref.dtype)


def norm_mod(x, g, sc, sh, seg_len, transpose=False, tm=512):
    t, d = x.shape
    spt = seg_len // tm
    seg_spec = pl.BlockSpec((None, 1, d), lambda i: (i // spt, 0, 0))
    if transpose:
        out_spec = pl.BlockSpec((d, tm), lambda i: (0, i))
        out_shape = jax.ShapeDtypeStruct((d, t), BF16)
    else:
        out_spec = pl.BlockSpec((tm, d), lambda i: (i, 0))
        out_shape = jax.ShapeDtypeStruct((t, d), BF16)
    return pl.pallas_call(
        functools.partial(_norm_mod_kernel, transpose=transpose),
        grid=(t // tm,),
        in_specs=[pl.BlockSpec((tm, d), lambda i: (i, 0)),
                  pl.BlockSpec((1, d), lambda i: (0, 0)), seg_spec, seg_spec],
        out_specs=out_spec,
        out_shape=out_shape,
        compiler_params=_cparams(("parallel",)),
        name="norm_mod_t" if transpose else "norm_mod",
    )(x, g, sc, sh)


def _mm_kernel(a_ref, b_ref, o_ref):
    o_ref[...] = jnp.dot(a_ref[...], b_ref[...], preferred_element_type=F32).astype(o_ref.dtype)


def matmul(a, b, tm, tn, out_dtype=F32, name="matmul", col0=0, n=None):
    m, k = a.shape
    n = b.shape[1] if n is None else n
    tm = min(tm, m)
    c0 = col0 // tn
    return pl.pallas_call(
        _mm_kernel,
        grid=(m // tm, n // tn),
        in_specs=[pl.BlockSpec((tm, k), lambda i, j: (i, 0)),
                  pl.BlockSpec((k, tn), lambda i, j: (0, c0 + j))],
        out_specs=pl.BlockSpec((tm, tn), lambda i, j: (i, j)),
        out_shape=jax.ShapeDtypeStruct((m, n), out_dtype),
        compiler_params=_cparams(("parallel", "parallel")),
        name=name,
    )(a, b)


def _mm_layer_kernel(a_ref, b_ref, buf_ref, o_ref):
    del buf_ref
    o_ref[...] = jnp.dot(a_ref[...], b_ref[...], preferred_element_type=F32).reshape(o_ref.shape)


def matmul_into_layer(a, b, col0, n, buf, layer, seq, tm, tn, name):
    m, k = a.shape
    tm = min(tm, m)
    sp = tm // seq
    c0 = col0 // tn
    return pl.pallas_call(
        _mm_layer_kernel,
        grid=(m // tm, n // tn),
        in_specs=[pl.BlockSpec((tm, k), lambda i, j: (i, 0)),
                  pl.BlockSpec((k, tn), lambda i, j: (0, c0 + j)),
                  pl.BlockSpec(memory_space=pl.ANY)],
        out_specs=pl.BlockSpec((sp, None, seq, tn), lambda i, j: (i, layer, 0, j)),
        out_shape=jax.ShapeDtypeStruct(buf.shape, F32),
        input_output_aliases={2: 0},
        compiler_params=_cparams(("parallel", "parallel")),
        name=name,
    )(a, b, buf)


def _ctx_attn_kernel(q_ref, k_ref, v_ref, o_ref):
    scale = NA_HEAD_DIM ** -0.5
    hd = NA_HEAD_DIM
    for h in range(NA_HEADS):
        hs = slice(h * hd, (h + 1) * hd)
        q = q_ref[:, hs].astype(BF16)
        k = k_ref[:, hs].astype(BF16)
        v = v_ref[:, hs].astype(BF16)
        s = _nt_dot(q, k) * scale
        m = jnp.max(s, axis=-1, keepdims=True)
        p = jnp.exp(s - m)
        p = p / jnp.sum(p, axis=-1, keepdims=True)
        o_ref[:, hs] = jnp.dot(p.astype(BF16), v, preferred_element_type=F32).astype(o_ref.dtype)


def context_attention(q, kbuf, vbuf, layer, nseq, seq):
    w = NA_WIDTH
    kv_spec = pl.BlockSpec((None, None, seq, w), lambda b: (b, layer, 0, 0))
    return pl.pallas_call(
        _ctx_attn_kernel,
        grid=(nseq,),
        in_specs=[pl.BlockSpec((seq, w), lambda b: (b, 0)), kv_spec, kv_spec],
        out_specs=pl.BlockSpec((seq, w), lambda b: (b, 0)),
        out_shape=jax.ShapeDtypeStruct((nseq * seq, w), BF16),
        compiler_params=_cparams(("parallel",)),
        name="ctx_attention",
    )(q, kbuf, vbuf)


def _na_kernel(q_ref, k_ref, v_ref, kc_ref, vc_ref, bias_ref, o_ref, kb_scr, vb_scr, *, rows):
    scale = NA_HEAD_DIM ** -0.5
    band = NA_WIN_ROWS * GRID_W
    kb_scr[...] = k_ref[...].astype(BF16)
    vb_scr[...] = v_ref[...].astype(BF16)
    kc = kc_ref[...].astype(BF16)
    vc = vc_ref[...].astype(BF16)

    def rows_step(it, carry):
        r0 = it * NA_ROW_GROUP
        q0 = pl.multiple_of(r0 * GRID_W, NA_ROW_GROUP * GRID_W)
        q = q_ref[pl.ds(q0, NA_ROW_GROUP * GRID_W), :].astype(BF16)
        k0s = []
        s_rows = []
        for j in range(NA_ROW_GROUP):
            r = r0 + j
            rs = jnp.clip(r - NA_WIN_ROWS // 2, 0, rows - NA_WIN_ROWS)
            k0 = pl.multiple_of(rs * GRID_W, GRID_W)
            k0s.append(k0)
            s_rows.append(_nt_dot(q[j * GRID_W:(j + 1) * GRID_W], kb_scr[pl.ds(k0, band), :]) * scale
                          + bias_ref[r - rs])
        s_win = jnp.concatenate(s_rows, axis=0)
        s_ctx = _nt_dot(q, kc) * scale
        m = jnp.maximum(jnp.max(s_win, axis=-1, keepdims=True), jnp.max(s_ctx, axis=-1, keepdims=True))
        p_win = jnp.exp(s_win - m)
        p_ctx = jnp.exp(s_ctx - m)
        den = jnp.sum(p_win, axis=-1, keepdims=True) + jnp.sum(p_ctx, axis=-1, keepdims=True)
        p_win = (p_win / den).astype(BF16)
        o_ctx = jnp.dot((p_ctx / den).astype(BF16), vc, preferred_element_type=F32)
        for j in range(NA_ROW_GROUP):
            o = jnp.dot(p_win[j * GRID_W:(j + 1) * GRID_W], vb_scr[pl.ds(k0s[j], band), :],
                        preferred_element_type=F32) + o_ctx[j * GRID_W:(j + 1) * GRID_W]
            o_ref[pl.ds(pl.multiple_of((r0 + j) * GRID_W, GRID_W), GRID_W), :] = o.astype(o_ref.dtype)
        return carry

    lax.fori_loop(0, rows // NA_ROW_GROUP, rows_step, 0)


def _na_bias_table(rel_bias):
    col = np.arange(GRID_W)
    cstart = np.clip(col - NA_WIN_COLS // 2, 0, GRID_W - NA_WIN_COLS)
    valid = (col[None, :] >= cstart[:, None]) & (col[None, :] < cstart[:, None] + NA_WIN_COLS)
    dcol = col[None, :] - col[:, None] + NA_WIN_COLS - 1
    onehot = ((np.arange(2 * NA_WIN_COLS - 1)[:, None, None] == dcol[None]) & valid[None]).astype(np.float32)
    tabc = jnp.einsum("hrk,kcd->hrcd", rel_bias, jnp.asarray(onehot), precision=lax.Precision.HIGHEST)
    tabc = tabc + jnp.asarray(np.where(valid, 0.0, NEG_BIG).astype(np.float32))
    wr = NA_WIN_ROWS
    tab = jnp.stack([tabc[:, wr - 1 - o:2 * wr - 1 - o].transpose(0, 2, 1, 3) for o in range(wr)], axis=1)
    return tab.reshape(rel_bias.shape[0], wr, GRID_W, wr * GRID_W).astype(F32)


def neighbourhood_attention(qkv, cache_k, cache_v, layer, bias_tab, nseq, seq):
    hd = NA_HEAD_DIM
    past = cache_k.shape[2]
    rows = seq // GRID_W
    assert seq % GRID_W == 0 and rows >= NA_WIN_ROWS and rows % NA_ROW_GROUP == 0, (seq, rows)
    band = NA_WIN_ROWS * GRID_W
    return pl.pallas_call(
        functools.partial(_na_kernel, rows=rows),
        grid=(nseq, NA_HEADS),
        in_specs=[pl.BlockSpec((seq, hd), lambda b, h: (b, h)),
                  pl.BlockSpec((seq, hd), lambda b, h: (b, NA_HEADS + h)),
                  pl.BlockSpec((seq, hd), lambda b, h: (b, 2 * NA_HEADS + h)),
                  pl.BlockSpec((None, None, past, hd), lambda b, h: (b, layer, 0, h)),
                  pl.BlockSpec((None, None, past, hd), lambda b, h: (b, layer, 0, h)),
                  pl.BlockSpec((None, NA_WIN_ROWS, GRID_W, band), lambda b, h: (h, 0, 0, 0))],
        out_specs=pl.BlockSpec((seq, hd), lambda b, h: (b, h)),
        out_shape=jax.ShapeDtypeStruct((nseq * seq, NA_WIDTH), BF16),
        scratch_shapes=[pltpu.VMEM((seq, hd), BF16), pltpu.VMEM((seq, hd), BF16)],
        compiler_params=_cparams(("parallel", "parallel")),
        name="nbr_attention",
    )(qkv, qkv, qkv, cache_k, cache_v, bias_tab)


def _conv_kernel(x_ref, w_ref, b_ref, o_ref, *, seq):
    x = x_ref[...]
    n = x.shape[0]
    pos = lax.rem(lax.broadcasted_iota(jnp.int32, x.shape, 0), seq)
    xm2 = jnp.where(pos >= 2, pltpu.roll(x, 2, 0), 0.0)
    xm1 = jnp.where(pos >= 1, pltpu.roll(x, 1, 0), 0.0)
    xp1 = jnp.where(pos < seq - 1, pltpu.roll(x, n - 1, 0), 0.0)
    w = w_ref[...]
    y = w[0:1] * xm2 + w[1:2] * xm1 + w[2:3] * x + w[3:4] * xp1 + b_ref[...]
    o_ref[...] = y * jax.nn.sigmoid(y)


def conv_silu(ssm, conv_w, conv_b, nseq, seq, rows=4096):
    tc = 256
    c0 = SSM_INNER // tc
    t = nseq * seq
    tr = min(max(rows, seq), t)
    out = pl.pallas_call(
        functools.partial(_conv_kernel, seq=seq),
        grid=(t // tr, SSM_CONV_DIM // tc),
        in_specs=[pl.BlockSpec((tr, tc), lambda s, j: (s, c0 + j)),
                  pl.BlockSpec((4, tc), lambda s, j: (0, j)),
                  pl.BlockSpec((1, tc), lambda s, j: (0, j))],
        out_specs=pl.BlockSpec((tr, tc), lambda s, j: (s, j)),
        out_shape=jax.ShapeDtypeStruct((t, SSM_CONV_DIM), F32),
        compiler_params=_cparams(("parallel", "parallel")),
        name="conv_silu",
    )(ssm, conv_w, conv_b)
    return out.reshape(nseq, seq, SSM_CONV_DIM)


def _select_columns(x, sel):
    hi = x.astype(BF16)
    r = x - hi.astype(F32)
    mid = r.astype(BF16)
    lo = (r - mid.astype(F32)).astype(BF16)
    return (jnp.dot(hi, sel, preferred_element_type=F32) + jnp.dot(mid, sel, preferred_element_type=F32)
            + jnp.dot(lo, sel, preferred_element_type=F32))


def _ssd_kernel(*refs, has_init, nchunk):
    if has_init:
        xc_ref, dt_ref, tri_ref, dtb_ref, alog_ref, sel_n_ref, sel_p_ref, init_ref, y_ref, fin_ref, st_scr = refs
    else:
        xc_ref, dt_ref, tri_ref, dtb_ref, alog_ref, sel_n_ref, sel_p_ref, y_ref, fin_ref, st_scr = refs
        init_ref = None
    c = pl.program_id(2)

    @pl.when(c == 0)
    def _():
        if has_init:
            st_scr[...] = init_ref[...]
        else:
            st_scr[...] = jnp.zeros_like(st_scr)

    p = SSM_HEAD_DIM
    n = SSM_STATE
    xc = xc_ref[...]
    dt = jax.nn.softplus(dt_ref[...] + dtb_ref[...])
    a = dt * (-jnp.exp(alog_ref[...]))
    tri = tri_ref[...]
    mask = tri > 0.5
    cum = jnp.dot(tri, a, preferred_element_type=F32, precision=lax.Precision.HIGHEST)
    cum_t = cum.T
    cum_x = _select_columns(cum, sel_n_ref[...])
    dt_x = _select_columns(dt, sel_p_ref[...])
    xd_all = xc[:, :SSM_INNER] * dt_x
    rep = SSM_HEADS // SSM_GROUPS
    gmat = []
    bmat = []
    cmat = []
    for g in range(SSM_GROUPS):
        b_g = xc[:, SSM_INNER + g * n:SSM_INNER + (g + 1) * n].astype(BF16)
        c_g = xc[:, SSM_INNER + SSM_BC + g * n:SSM_INNER + SSM_BC + (g + 1) * n].astype(BF16)
        bmat.append(b_g)
        cmat.append(c_g)
        gmat.append(_nt_dot(c_g, b_g))
    for h in range(SSM_HEADS):
        g = h // rep
        cum_h = cum_x[:, h * n:(h + 1) * n]
        decay = jnp.exp(jnp.where(mask, cum_h - cum_t[h:h + 1, :], -jnp.inf))
        xd = xd_all[:, h * p:(h + 1) * p]
        y_diag = jnp.dot((gmat[g] * decay).astype(BF16), xd.astype(BF16), preferred_element_type=F32)
        tot = jnp.min(cum_h, axis=0, keepdims=True)
        e_end = jnp.exp(tot[:, :p] - cum_h[:, :p])
        chunk_state = _tn_dot((xd * e_end).astype(BF16), bmat[g])
        prev = st_scr[h]
        y_off = _nt_dot(cmat[g], prev.astype(BF16)) * jnp.exp(cum_h[:, :p])
        y_ref[:, h * p:(h + 1) * p] = y_diag + y_off
        st_scr[h] = prev * jnp.exp(tot) + chunk_state

    @pl.when(c == nchunk - 1)
    def _():
        fin_ref[...] = st_scr[...]


def ssd_scan(xc, ssm, dt_bias, a_log, init, layer):
    nseq, seq, _ = xc.shape
    q = SSM_CHUNK
    nc = seq // q
    idx = np.arange(q)
    tri = jnp.asarray(np.stack([idx[None, :] <= idx[:, None], idx[None, :] >= idx[:, None]]).astype(np.float32))

    def cc(d, c):
        return c + d * (nc - 1 - 2 * c)

    in_specs = [pl.BlockSpec((None, q, SSM_CONV_DIM), lambda d, s, c: (s, cc(d, c), 0)),
                pl.BlockSpec((None, q, LANES), lambda d, s, c: (s, cc(d, c), DT_BLOCK0 + d)),
                pl.BlockSpec((None, q, q), lambda d, s, c: (d, 0, 0)),
                pl.BlockSpec((None, 1, LANES), lambda d, s, c: (d, 0, 0)),
                pl.BlockSpec((None, 1, LANES), lambda d, s, c: (d, 0, 0)),
                pl.BlockSpec((LANES, SSM_HEADS * SSM_STATE), lambda d, s, c: (0, 0)),
                pl.BlockSpec((LANES, SSM_INNER), lambda d, s, c: (0, 0))]
    head = np.arange(LANES)[:, None]
    sel_n = jnp.asarray(head == np.arange(SSM_HEADS * SSM_STATE)[None, :] // SSM_STATE, BF16)
    sel_p = jnp.asarray(head == np.arange(SSM_INNER)[None, :] // SSM_HEAD_DIM, BF16)
    args = [xc, ssm.reshape(nseq, seq, SSM_COLS), tri, dt_bias, a_log, sel_n, sel_p]
    if init is not None:
        in_specs.append(pl.BlockSpec((None, None, None, SSM_HEADS, SSM_HEAD_DIM, SSM_STATE),
                                     lambda d, s, c: (s, layer, d, 0, 0, 0)))
        args.append(init)
    return pl.pallas_call(
        functools.partial(_ssd_kernel, has_init=init is not None, nchunk=nc),
        grid=(2, nseq, nc),
        in_specs=in_specs,
        out_specs=[pl.BlockSpec((None, None, q, SSM_INNER), lambda d, s, c: (d, s, cc(d, c), 0)),
                   pl.BlockSpec((None, None, SSM_HEADS, SSM_HEAD_DIM, SSM_STATE), lambda d, s, c: (s, d, 0, 0, 0))],
        out_shape=[jax.ShapeDtypeStruct((2, nseq, seq, SSM_INNER), F32),
                   jax.ShapeDtypeStruct((nseq, 2, SSM_HEADS, SSM_HEAD_DIM, SSM_STATE), F32)],
        scratch_shapes=[pltpu.VMEM((SSM_HEADS, SSM_HEAD_DIM, SSM_STATE), F32)],
        compiler_params=_cparams(("parallel", "parallel", "arbitrary")),
        name="ssd_scan",
    )(*args)


def _ssm_finish_kernel(y_ref, xs_ref, z_ref, dsk_ref, g_ref, o_ref):
    z = z_ref[...]
    y = (y_ref[0] + y_ref[1] + dsk_ref[...] * xs_ref[...]) * (z * jax.nn.sigmoid(z))
    y = y * lax.rsqrt(jnp.mean(y * y, axis=-1, keepdims=True) + EPS) * g_ref[...]
    o_ref[...] = y.astype(o_ref.dtype)


def ssm_finish(y, xc, ssm, d_skip, g, tm=512):
    t = ssm.shape[0]
    w = SSM_INNER
    return pl.pallas_call(
        _ssm_finish_kernel,
        grid=(t // tm,),
        in_specs=[pl.BlockSpec((2, tm, w), lambda i: (0, i, 0)),
                  pl.BlockSpec((tm, w), lambda i: (i, 0)),
                  pl.BlockSpec((tm, w), lambda i: (i, 0)),
                  pl.BlockSpec((1, w), lambda i: (0, 0)),
                  pl.BlockSpec((1, w), lambda i: (0, 0))],
        out_specs=pl.BlockSpec((tm, w), lambda i: (i, 0)),
        out_shape=jax.ShapeDtypeStruct((t, w), BF16),
        compiler_params=_cparams(("parallel",)),
        name="ssm_finish",
    )(y, xc, ssm, d_skip, g)


def _gmlp_kernel(u_ref, v_ref, g_ref, w_ref, b_ref, o_ref):
    v = jax.nn.gelu(v_ref[...].astype(F32))
    v = (v * lax.rsqrt(jnp.mean(v * v, axis=-1, keepdims=True) + EPS) * g_ref[...]).astype(BF16)
    q = MLP_CHUNK
    gw = MLP_WIDTH // MLP_GROUPS
    for ch in range(v.shape[0] // q):
        for g in range(MLP_GROUPS):
            mixed = jnp.dot(w_ref[g], v[ch * q:(ch + 1) * q, g * gw:(g + 1) * gw],
                            preferred_element_type=F32) + b_ref[g]
            u = jax.nn.gelu(u_ref[ch * q:(ch + 1) * q, g * gw:(g + 1) * gw].astype(F32))
            o_ref[ch * q:(ch + 1) * q, g * gw:(g + 1) * gw] = (u * mixed).astype(o_ref.dtype)


def chunk_mlp(mlp, g, w_s, b_s, tm=512):
    t = mlp.shape[0]
    w = MLP_WIDTH
    return pl.pallas_call(
        _gmlp_kernel,
        grid=(t // tm,),
        in_specs=[pl.BlockSpec((tm, w), lambda i: (i, 0)),
                  pl.BlockSpec((tm, w), lambda i: (i, 1)),
                  pl.BlockSpec((1, w), lambda i: (0, 0)),
                  pl.BlockSpec(w_s.shape, lambda i: (0, 0, 0)),
                  pl.BlockSpec(b_s.shape, lambda i: (0, 0, 0))],
        out_specs=pl.BlockSpec((tm, w), lambda i: (i, 0)),
        out_shape=jax.ShapeDtypeStruct((t, w), BF16),
        compiler_params=_cparams(("parallel",)),
        name="chunk_mlp",
    )(mlp, mlp, g, w_s, b_s)


def _merge_kernel(a0_ref, a1_ref, a2_ref, w_ref, g0_ref, g1_ref, g2_ref, o_ref):
    acc = jax.nn.sigmoid(g0_ref[...].astype(F32)) * jnp.dot(a0_ref[...], w_ref[0], preferred_element_type=F32)
    acc += jax.nn.sigmoid(g1_ref[...].astype(F32)) * jnp.dot(a1_ref[...], w_ref[1], preferred_element_type=F32)
    acc += jax.nn.sigmoid(g2_ref[...].astype(F32)) * jnp.dot(a2_ref[...], w_ref[2], preferred_element_type=F32)
    o_ref[...] = acc.astype(o_ref.dtype)


def merge_branches(o_na, o_ssm, o_mlp, w_br, gates, tm=1024, tn=512):
    t, k = o_na.shape
    tm = min(tm, t)
    d = w_br.shape[2]
    nb = d // tn
    a_spec = pl.BlockSpec((tm, k), lambda i, j: (i, 0))
    return pl.pallas_call(
        _merge_kernel,
        grid=(t // tm, nb),
        in_specs=[a_spec, a_spec, a_spec,
                  pl.BlockSpec((N_BRANCH, k, tn), lambda i, j: (0, 0, j)),
                  pl.BlockSpec((tm, tn), lambda i, j: (i, j)),
                  pl.BlockSpec((tm, tn), lambda i, j: (i, nb + j)),
                  pl.BlockSpec((tm, tn), lambda i, j: (i, 2 * nb + j))],
        out_specs=pl.BlockSpec((tm, tn), lambda i, j: (i, j)),
        out_shape=jax.ShapeDtypeStruct((t, d), BF16),
        compiler_params=_cparams(("parallel", "parallel")),
        name="merge_branches",
    )(o_na, o_ssm, o_mlp, w_br, gates, gates, gates)


def _out_proj_kernel(a_ref, w_ref, x_ref, gt_ref, o_ref):
    o_ref[...] = x_ref[...] + gt_ref[...] * jnp.dot(a_ref[...], w_ref[...], preferred_element_type=F32)


def out_proj(merged, w_out, x, gt, seg_len, tm=2048, tn=512):
    t, k = merged.shape
    d = w_out.shape[1]
    tm = min(tm, seg_len)
    spt = seg_len // tm
    return pl.pallas_call(
        _out_proj_kernel,
        grid=(t // tm, d // tn),
        in_specs=[pl.BlockSpec((tm, k), lambda i, j: (i, 0)),
                  pl.BlockSpec((k, tn), lambda i, j: (0, j)),
                  pl.BlockSpec((tm, tn), lambda i, j: (i, j)),
                  pl.BlockSpec((None, 1, tn), lambda i, j: (i // spt, 0, j))],
        out_specs=pl.BlockSpec((tm, tn), lambda i, j: (i, j)),
        out_shape=jax.ShapeDtypeStruct((t, d), F32),
        compiler_params=_cparams(("parallel", "parallel")),
        name="out_proj",
    )(merged, w_out, x, gt)


def _peer_scores_kernel(wq_ref, h_ref, key_ref, o_ref):
    qv = jnp.dot(wq_ref[...], h_ref[...], preferred_element_type=F32).astype(BF16)
    dq = PEER_QDIM // 2
    for hk in range(2 * PEER_HEADS):
        o_ref[hk * PEER_KEYS:(hk + 1) * PEER_KEYS, :] = jnp.dot(
            key_ref[hk], qv[hk * dq:(hk + 1) * dq, :], preferred_element_type=F32)


def peer_scores(wq_t, h_t, keys, tl=512):
    d, t = h_t.shape
    rows = 2 * PEER_HEADS * PEER_KEYS
    return pl.pallas_call(
        _peer_scores_kernel,
        grid=(t // tl,),
        in_specs=[pl.BlockSpec(wq_t.shape, lambda i: (0, 0)),
                  pl.BlockSpec((d, tl), lambda i: (0, i)),
                  pl.BlockSpec(keys.shape, lambda i: (0, 0, 0))],
        out_specs=pl.BlockSpec((rows, tl), lambda i: (0, i)),
        out_shape=jax.ShapeDtypeStruct((rows, t), F32),
        compiler_params=_cparams(("parallel",)),
        name="peer_scores",
    )(wq_t, h_t, keys)


def _top16(s, exact):
    nk = s.shape[0]
    kio = lax.broadcasted_iota(jnp.int32, s.shape, 0).astype(F32)
    rank = jnp.full(s.shape, float(PEER_TOPK), F32)
    vals = []
    for a in range(PEER_TOPK):
        m = jnp.max(s, axis=0, keepdims=True)
        hit = s == m
        if exact:
            hit = kio == jnp.min(jnp.where(hit, kio, float(nk)), axis=0, keepdims=True)
        rank = jnp.where(hit, float(a), rank)
        s = jnp.where(hit, -jnp.inf, s)
        vals.append(m)
    return jnp.concatenate(vals, axis=0), rank


def _peer_select_heads(s_ref, rank2_ref, cnt_ref, e1_ref, e2_ref, exact):
    nk = PEER_KEYS
    k = PEER_TOPK
    bad = jnp.zeros((1, s_ref.shape[1]), F32)
    for h in range(PEER_HEADS):
        s1 = s_ref[(2 * h) * nk:(2 * h + 1) * nk, :]
        s2 = s_ref[(2 * h + 1) * nk:(2 * h + 2) * nk, :]
        tv1, rank1 = _top16(s1, exact)
        tv2, rank2 = _top16(s2, exact)
        sub = lax.broadcasted_iota(jnp.int32, (8, s1.shape[1]), 0).astype(F32)
        pieces = []
        pos_pieces = []
        for a in range(k // 2):
            nb = k // (a + 1)
            for b0 in range(0, nb, 8):
                vals = tv1[a:a + 1, :] + tv2[b0:b0 + 8, :]
                pieces.append(vals if nb - b0 >= 8 else jnp.where(sub < float(nb - b0), vals, -jnp.inf))
                pos_pieces.append(sub + float(a * k + b0))
        pieces.append(tv1[k // 2:k, :] + tv2[0:1, :])
        pos_pieces.append((sub + float(k // 2)) * float(k))
        cand = jnp.concatenate(pieces, axis=0)
        pio = jnp.concatenate(pos_pieces, axis=0)
        cnt_cells = jnp.zeros(cand.shape, F32)
        zsum = jnp.zeros((1, cand.shape[1]), F32)
        top = tv1[0:1, :] + tv2[0:1, :]
        for _ in range(k):
            m = jnp.max(cand, axis=0, keepdims=True)
            hit = cand == m
            if exact:
                hit = pio == jnp.min(jnp.where(hit, pio, float(k * k)), axis=0, keepdims=True)
            cand = jnp.where(hit, -jnp.inf, cand)
            cnt_cells = jnp.where(hit, 1.0, cnt_cells)
            zsum = zsum + jnp.exp(m - top)
        cnt = jnp.zeros(s1.shape, F32)
        for a in range(k):
            if a < k // 2:
                lo = 0 if a == 0 else 8 * (a + 1)
                cnt_a = jnp.sum(cnt_cells[lo:8 * (a + 2), :], axis=0, keepdims=True)
            else:
                r = 8 * (k // 2 + 1) + a - k // 2
                cnt_a = cnt_cells[r:r + 1, :]
            cnt = jnp.where(rank1 == float(a), cnt_a, cnt)
        if not exact:
            n1 = jnp.sum(jnp.where(rank1 < float(k), 1.0, 0.0), axis=0, keepdims=True)
            n2 = jnp.sum(jnp.where(rank2 < float(k), 1.0, 0.0), axis=0, keepdims=True)
            nc = jnp.sum(cnt_cells, axis=0, keepdims=True)
            bad = bad + jnp.abs(n1 - float(k)) + jnp.abs(n2 - float(k)) + jnp.abs(nc - float(k))
        rank2_ref[h * nk:(h + 1) * nk, :] = rank2.astype(rank2_ref.dtype)
        cnt_ref[h * nk:(h + 1) * nk, :] = cnt
        e1_ref[h * nk:(h + 1) * nk, :] = jnp.exp(s1 - tv1[0:1, :])
        e2_ref[h * nk:(h + 1) * nk, :] = (jnp.exp(s2 - tv2[0:1, :]) / zsum).astype(e2_ref.dtype)
    return bad


def _peer_select_kernel(s_ref, rank2_ref, cnt_ref, e1_ref, e2_ref):
    bad = _peer_select_heads(s_ref, rank2_ref, cnt_ref, e1_ref, e2_ref, exact=False)

    @pl.when(jnp.max(bad) > 0.0)
    def _():
        _peer_select_heads(s_ref, rank2_ref, cnt_ref, e1_ref, e2_ref, exact=True)


def peer_select(scores, tl=128):
    rows, t = scores.shape
    half = rows // 2
    spec = pl.BlockSpec((half, tl), lambda i: (0, i))
    shp = jax.ShapeDtypeStruct((half, t), F32)
    shp_b = jax.ShapeDtypeStruct((half, t), BF16)
    return pl.pallas_call(
        _peer_select_kernel,
        grid=(t // tl,),
        in_specs=[pl.BlockSpec((rows, tl), lambda i: (0, i))],
        out_specs=[spec, spec, spec, spec],
        out_shape=[shp_b, shp, shp, shp_b],
        compiler_params=_cparams(("parallel",)),
        name="peer_select",
    )(scores)


def _peer_dense_kernel(h_ref, u_ref, vta_ref, vtb_ref, rank2_ref, cnt_ref, e1_ref, e2_ref, o_ref, za_scr, zb_scr,
                       *, nstep, rows_per_half, lane_tile):
    e = pl.program_id(1)

    @pl.when(e == 0)
    def _():
        o_ref[...] = jnp.zeros_like(o_ref)
        zb_scr[...] = jnp.zeros_like(zb_scr)

    nk = PEER_KEYS
    eh = rows_per_half * nk
    nsub = rows_per_half // SUB_KEYS
    mrows = o_ref.shape[0] // nsub
    lanes = [slice(lt * lane_tile, (lt + 1) * lane_tile) for lt in range(o_ref.shape[1] // lane_tile)]
    for half, (z_new, z_old, vt_ref) in enumerate(((za_scr, zb_scr, vta_ref), (zb_scr, za_scr, vtb_ref))):
        for sc in range(nsub):
            i0 = half * rows_per_half + sc * SUB_KEYS
            acts = [jax.nn.gelu(jnp.dot(u_ref[i0 * nk:(i0 + SUB_KEYS) * nk, :], h_ref[:, ls],
                                        preferred_element_type=F32)) for ls in lanes]
            for ls in lanes:
                o_ref[sc * mrows:(sc + 1) * mrows, ls] += jnp.dot(
                    vt_ref[sc * mrows:(sc + 1) * mrows, :eh], z_old[:, ls], preferred_element_type=F32)
            for ls, act in zip(lanes, acts):
                for k in range(SUB_KEYS):
                    w = None
                    for h in range(PEER_HEADS):
                        row = h * nk + e * (2 * rows_per_half) + i0 + k
                        cnt_b = jnp.broadcast_to(cnt_ref[pl.ds(row, 1), ls], (BF16_ROWS, lane_tile))
                        e1_b = jnp.broadcast_to(e1_ref[pl.ds(row, 1), ls], (BF16_ROWS, lane_tile))
                        cnt_b = jnp.concatenate([cnt_b.astype(BF16)] * (nk // BF16_ROWS), axis=0)
                        e1_b = jnp.concatenate([e1_b.astype(BF16)] * (nk // BF16_ROWS), axis=0)
                        sel = rank2_ref[h * nk:(h + 1) * nk, ls] < cnt_b
                        term = jnp.where(sel, e2_ref[h * nk:(h + 1) * nk, ls], jnp.zeros((), BF16)) * e1_b
                        w = term if w is None else w + term
                    r0 = (sc * SUB_KEYS + k) * nk
                    z_new[r0:r0 + nk, ls] = w * act[k * nk:(k + 1) * nk, :].astype(BF16)

    @pl.when(e == nstep - 1)
    def _():
        for ls in lanes:
            o_ref[:, ls] += jnp.dot(vtb_ref[:, eh:], zb_scr[:, ls], preferred_element_type=F32)


def peer_dense(h_t, u_tab, v_t, rank2, cnt, e1, e2, tm=512, et=1024, lane_tile=256):
    d, t = h_t.shape
    ne = u_tab.shape[0]
    assert ne == PEER_EXPERTS and ne % et == 0 and t % tm == 0 and tm % lane_tile == 0, (ne, et, t, tm)
    assert (et // 2) % (SUB_KEYS * PEER_KEYS) == 0 and d % (et // 2 // (SUB_KEYS * PEER_KEYS)) == 0
    nstep = ne // et
    eh = et // 2
    map_spec = pl.BlockSpec((rank2.shape[0], tm), lambda i, e: (0, i))
    return pl.pallas_call(
        functools.partial(_peer_dense_kernel, nstep=nstep, rows_per_half=eh // PEER_KEYS, lane_tile=lane_tile),
        grid=(t // tm, nstep),
        in_specs=[pl.BlockSpec((d, tm), lambda i, e: (0, i)),
                  pl.BlockSpec((et, d), lambda i, e: (e, 0)),
                  pl.BlockSpec((d, eh), lambda i, e: (0, jnp.maximum(2 * e - 1, 0))),
                  pl.BlockSpec((d, et), lambda i, e: (0, e)),
                  map_spec, map_spec, map_spec, map_spec],
        out_specs=pl.BlockSpec((d, tm), lambda i, e: (0, i)),
        out_shape=jax.ShapeDtypeStruct((d, t), F32),
        scratch_shapes=[pltpu.VMEM((eh, tm), BF16), pltpu.VMEM((eh, tm), BF16)],
        compiler_params=_cparams(("parallel", "arbitrary")),
        name="peer_dense",
    )(h_t, u_tab, v_t, v_t, rank2, cnt, e1, e2)


def _peer_resid_norm_kernel(x_ref, p_ref, gt_ref, g_ref, sc_ref, sh_ref, x_out_ref, h_ref):
    x = x_ref[...] + gt_ref[...] * p_ref[...].T
    x_out_ref[...] = x
    y = x * lax.rsqrt(jnp.mean(x * x, axis=-1, keepdims=True) + EPS) * g_ref[...]
    h_ref[...] = (y * (1.0 + sc_ref[...]) + sh_ref[...]).astype(h_ref.dtype)


def _peer_resid_final_kernel(x_ref, p_ref, gt_ref, g_ref, o_ref):
    x = x_ref[...] + gt_ref[...] * p_ref[...].T
    o_ref[...] = x * lax.rsqrt(jnp.mean(x * x, axis=-1, keepdims=True) + EPS) * g_ref[...]


def peer_residual_norm(x, peer_t, gt, g, sc, sh, seg_len, tm=512):
    t, d = x.shape
    spt = seg_len // tm
    row_spec = pl.BlockSpec((tm, d), lambda i: (i, 0))
    seg_spec = pl.BlockSpec((None, 1, d), lambda i: (i // spt, 0, 0))
    in_specs = [row_spec, pl.BlockSpec((d, tm), lambda i: (0, i)), seg_spec, pl.BlockSpec((1, d), lambda i: (0, 0))]
    if sc is None:
        return pl.pallas_call(
            _peer_resid_final_kernel,
            grid=(t // tm,),
            in_specs=in_specs,
            out_specs=row_spec,
            out_shape=jax.ShapeDtypeStruct((t, d), F32),
            compiler_params=_cparams(("parallel",)),
            name="peer_resid_final",
        )(x, peer_t, gt, g)
    return pl.pallas_call(
        _peer_resid_norm_kernel,
        grid=(t // tm,),
        in_specs=in_specs + [seg_spec, seg_spec],
        out_specs=[row_spec, row_spec],
        out_shape=[jax.ShapeDtypeStruct((t, d), F32), jax.ShapeDtypeStruct((t, d), BF16)],
        compiler_params=_cparams(("parallel",)),
        name="peer_resid_norm",
    )(x, peer_t, gt, g, sc, sh)


def _layer(x, h, mod, lw, layer, nseq, seq, ctx):
    sh1, sc1, gt1, sh2, sc2, gt2 = mod
    seg_len = x.shape[0] // sh1.shape[0]
    ssm = matmul(h, lw["w_ssm"], 1024, 128 * 11, name="proj_ssm")
    mlp = matmul(h, lw["w_mlp"], 1024, 1024, out_dtype=BF16, name="proj_mlp")
    gates = matmul(h, lw["w_gate"], 1024, 1024, out_dtype=BF16, name="proj_gate")
    if ctx[0] == "context":
        q = matmul(h, lw["w_qkv"], 1024, 1024, out_dtype=BF16, name="proj_q", col0=0, n=NA_WIDTH)
        kbuf = matmul_into_layer(h, lw["w_qkv"], NA_WIDTH, NA_WIDTH, ctx[1], layer, seq, 1024, 1024, "proj_k")
        vbuf = matmul_into_layer(h, lw["w_qkv"], 2 * NA_WIDTH, NA_WIDTH, ctx[2], layer, seq, 1024, 1024, "proj_v")
        o_na = context_attention(q, kbuf, vbuf, layer, nseq, seq)
        init = None
    else:
        _, cache_k, cache_v, init = ctx
        qkv = matmul(h, lw["w_qkv"], 1024, 1024, out_dtype=BF16, name="proj_qkv")
        o_na = neighbourhood_attention(qkv, cache_k, cache_v, layer, lw["na_bias"], nseq, seq)
    xc = conv_silu(ssm, lw["conv_w"], lw["conv_b"], nseq, seq)
    y, fin = ssd_scan(xc, ssm, lw["dt_bias"], lw["a_log"], init, layer)
    t = nseq * seq
    o_ssm = ssm_finish(y.reshape(2, t, SSM_INNER), xc.reshape(t, SSM_CONV_DIM), ssm, lw["d_skip"], lw["ssm_norm_g"])
    o_mlp = chunk_mlp(mlp, lw["mlp_norm_g"], lw["mlp_w_s"], lw["mlp_b_s"])
    merged = merge_branches(o_na, o_ssm, o_mlp, lw["w_br"], gates)
    x = out_proj(merged, lw["w_out"], x, gt1, seg_len)
    h2t = norm_mod(x, lw["g_ffn"], sc2, sh2, seg_len, transpose=True)
    scores = peer_scores(lw["peer_wq_t"], h2t, lw["peer_keys"])
    rank2, cnt, e1, e2 = peer_select(scores)
    peer_t = peer_dense(h2t, lw["peer_u"], lw["peer_v_t"], rank2, cnt, e1, e2)
    extras = (kbuf, vbuf, fin) if ctx[0] == "context" else ()
    return x, peer_t, extras


def _pad_lanes(v):
    pad = LANES - v.shape[-1]
    return jnp.pad(v, [(0, 0)] * (v.ndim - 1) + [(0, pad)])[..., None, :]


def kernel(x_prompt, x_sample, cache_k, cache_v, state_ssm, c, c_ctx, w_mod, b_mod, g_norm_mix, g_norm_ffn, g_norm_final, w_in, na_rel_bias, ssm_conv_w, ssm_conv_b, ssm_dt_bias, ssm_a_log, ssm_d, ssm_norm_g, mlp_norm_g, mlp_w_s, mlp_b_s, w_br_na, w_br_ssm, w_br_mlp, w_out, peer_w_q, peer_sub_keys, peer_u, peer_v):
    batch, seq, d = x_prompt.shape
    dec_batch, dec_seq, _ = x_sample.shape
    depth = w_in.shape[0]
    past = cache_k.shape[2]

    cvec = jnp.concatenate([c_ctx[None, :], c, jnp.zeros((8 - 1 - dec_batch, d), F32)], axis=0)
    mod_all = modulation(cvec, w_mod, b_mod)

    o_q = 0
    o_z = 3 * NA_WIDTH
    o_xbc = o_z + SSM_INNER
    o_dt = o_xbc + SSM_CONV_DIM
    o_u = o_dt + 2 * SSM_HEADS
    o_g = o_u + 2 * MLP_WIDTH
    dt_pad = jnp.zeros((depth, d, LANES - SSM_HEADS), F32)
    w_ssm = jnp.concatenate([w_in[:, :, o_z:o_dt], w_in[:, :, o_dt:o_dt + SSM_HEADS], dt_pad,
                             w_in[:, :, o_dt + SSM_HEADS:o_u], dt_pad], axis=-1).astype(BF16)
    w_qkv = w_in[:, :, o_q:o_z].astype(BF16)
    w_mlp = w_in[:, :, o_u:o_g].astype(BF16)
    w_gate = w_in[:, :, o_g:].astype(BF16)
    w_br = jnp.stack([w_br_na, w_br_ssm, w_br_mlp], axis=1).astype(BF16)
    w_out_b = w_out.astype(BF16)
    peer_wq_t = jnp.swapaxes(peer_w_q, 1, 2).astype(BF16)
    peer_keys = peer_sub_keys.reshape(depth, 2 * PEER_HEADS, PEER_KEYS, PEER_QDIM // 2).astype(BF16)
    peer_u_b = peer_u.astype(BF16)
    peer_v_t = jnp.swapaxes(peer_v, 1, 2).astype(BF16)
    dt_bias = _pad_lanes(ssm_dt_bias)
    a_log = _pad_lanes(ssm_a_log)
    d_skip = jnp.repeat(ssm_d, SSM_HEAD_DIM, axis=-1)[:, None, :]
    mlp_b = jnp.broadcast_to(mlp_b_s[..., None], mlp_b_s.shape + (MLP_CHUNK,))
    mlp_w = mlp_w_s.astype(BF16)

    cache_k4 = cache_k.reshape(dec_batch, depth, past, NA_WIDTH)
    cache_v4 = cache_v.reshape(dec_batch, depth, past, NA_WIDTH)

    xp = x_prompt.reshape(batch * seq, d)
    xs = x_sample.reshape(dec_batch * dec_seq, d)
    mods = []
    for l in range(depth):
        m = mod_all[l].reshape(8, 6, d)
        mods.append(([m[0:1, j][:, None, :] for j in range(6)],
                     [m[1:1 + dec_batch, j][:, None, :] for j in range(6)]))
    kbuf = jnp.zeros((batch, depth, seq, NA_WIDTH), F32)
    vbuf = jnp.zeros((batch, depth, seq, NA_WIDTH), F32)
    new_s = []
    for l in range(depth):
        lw = {
            "g_mix": g_norm_mix[l][None, :], "g_ffn": g_norm_ffn[l][None, :],
            "w_qkv": w_qkv[l], "w_ssm": w_ssm[l], "w_mlp": w_mlp[l], "w_gate": w_gate[l],
            "na_bias": _na_bias_table(na_rel_bias[l]),
            "conv_w": ssm_conv_w[l], "conv_b": ssm_conv_b[l][None, :],
            "dt_bias": dt_bias[l], "a_log": a_log[l], "d_skip": d_skip[l],
            "ssm_norm_g": ssm_norm_g[l][None, :], "mlp_norm_g": mlp_norm_g[l][None, :],
            "mlp_w_s": mlp_w[l], "mlp_b_s": mlp_b[l],
            "w_br": w_br[l], "w_out": w_out_b[l],
            "peer_wq_t": peer_wq_t[l], "peer_keys": peer_keys[l], "peer_u": peer_u_b[l], "peer_v_t": peer_v_t[l],
        }
        mod_ctx, mod_lat = mods[l]
        if l == 0:
            hp = norm_mod(xp, lw["g_mix"], mod_ctx[1], mod_ctx[0], batch * seq)
            hs = norm_mod(xs, lw["g_mix"], mod_lat[1], mod_lat[0], dec_seq)
        xp, peer_p, (kbuf, vbuf, fin) = _layer(xp, hp, mod_ctx, lw, l, batch, seq, ("context", kbuf, vbuf))
        xs, peer_s, _ = _layer(xs, hs, mod_lat, lw, l, dec_batch, dec_seq,
                               ("latent", cache_k4, cache_v4, state_ssm))
        if l + 1 < depth:
            nxt_ctx, nxt_lat = mods[l + 1]
            g_next = g_norm_mix[l + 1][None, :]
            xp, hp = peer_residual_norm(xp, peer_p, mod_ctx[5], g_next, nxt_ctx[1], nxt_ctx[0], batch * seq)
            xs, hs = peer_residual_norm(xs, peer_s, mod_lat[5], g_next, nxt_lat[1], nxt_lat[0], dec_seq)
        else:
            gf = g_norm_final[None, :]
            y_prompt = peer_residual_norm(xp, peer_p, mod_ctx[5], gf, None, None, batch * seq)
            y_sample = peer_residual_norm(xs, peer_s, mod_lat[5], gf, None, None, dec_seq)
        new_s.append(fin)
    cache_shape = (batch, depth, seq, NA_HEADS, NA_HEAD_DIM)
    return (y_prompt.reshape(batch, seq, d), y_sample.reshape(dec_batch, dec_seq, d),
            kbuf.reshape(cache_shape), vbuf.reshape(cache_shape), jnp.stack(new_s, axis=1))
```

```python
import functools

import numpy as np
import jax
import jax.numpy as jnp
from jax import lax
from jax.experimental import pallas as pl
from jax.experimental.pallas import tpu as pltpu

F32 = jnp.float32
BF16 = jnp.bfloat16

D_MODEL = 2048
DEPTH = 4
GRID_W = 64
EPS = 1e-6
NA_HEADS = 8
NA_HEAD_DIM = 128
NA_WIDTH = NA_HEADS * NA_HEAD_DIM
NA_WIN_ROWS = 8
NA_WIN_COLS = 16
NA_ROW_GROUP = 16
SSM_HEADS = 16
SSM_HEAD_DIM = 64
SSM_INNER = SSM_HEADS * SSM_HEAD_DIM
SSM_GROUPS = 2
SSM_STATE = 128
SSM_CHUNK = 128
SSM_BC = SSM_GROUPS * SSM_STATE
SSM_CONV_DIM = SSM_INNER + 2 * SSM_BC
MLP_GROUPS = 8
MLP_WIDTH = 1024
MLP_CHUNK = 128
PEER_HEADS = 8
PEER_KEYS = 128
PEER_EXPERTS = PEER_KEYS * PEER_KEYS
PEER_QDIM = 256
PEER_TOPK = 16
N_BRANCH = 3

LANES = 128
BF16_ROWS = 16
SUB_KEYS = 2
SSM_COLS = SSM_INNER + SSM_CONV_DIM + 2 * LANES
DT_BLOCK0 = (SSM_INNER + SSM_CONV_DIM) // LANES
VMEM_LIMIT = 60 * 1024 * 1024
NEG_BIG = -1e30


def _cparams(sem):
    return pltpu.CompilerParams(dimension_semantics=sem, vmem_limit_bytes=VMEM_LIMIT)


def _nt_dot(a, b):
    return lax.dot_general(a, b, (((1,), (1,)), ((), ())), preferred_element_type=F32)


def _tn_dot(a, b):
    return lax.dot_general(a, b, (((0,), (0,)), ((), ())), preferred_element_type=F32)


def _mod_kernel(c_ref, w_ref, b_ref, o_ref):
    c = c_ref[...]
    a = c * jax.nn.sigmoid(c)
    o_ref[...] = jnp.dot(a, w_ref[...], preferred_element_type=F32,
                         precision=lax.Precision.HIGHEST) + b_ref[...]


def modulation(cvec, w_mod, b_mod):
    depth, d, n = w_mod.shape
    tn = 1024
    return pl.pallas_call(
        _mod_kernel,
        grid=(depth, n // tn),
        in_specs=[
            pl.BlockSpec((8, d), lambda l, j: (0, 0)),
            pl.BlockSpec((None, d, tn), lambda l, j: (l, 0, j)),
            pl.BlockSpec((None, 1, tn), lambda l, j: (l, 0, j)),
        ],
        out_specs=pl.BlockSpec((None, 8, tn), lambda l, j: (l, 0, j)),
        out_shape=jax.ShapeDtypeStruct((depth, 8, n), F32),
        compiler_params=_cparams(("parallel", "parallel")),
        name="modulation",
    )(cvec, w_mod, b_mod.reshape(depth, 1, n))


def _norm_mod_kernel(x_ref, g_ref, sc_ref, sh_ref, o_ref, *, transpose):
    x = x_ref[...]
    y = x * lax.rsqrt(jnp.mean(x * x, axis=-1, keepdims=True) + EPS) * g_ref[...]
    h = y * (1.0 + sc_ref[...]) + sh_ref[...]
    if transpose:
        h = h.T
    o_ref[...] = h.astype(o_ref.dtype)


def norm_mod(x, g, sc, sh, seg_len, transpose=False, tm=512):
    t, d = x.shape
    spt = seg_len // tm
    seg_spec = pl.BlockSpec((None, 1, d), lambda i: (i // spt, 0, 0))
    if transpose:
        out_spec = pl.BlockSpec((d, tm), lambda i: (0, i))
        out_shape = jax.ShapeDtypeStruct((d, t), BF16)
    else:
        out_spec = pl.BlockSpec((tm, d), lambda i: (i, 0))
        out_shape = jax.ShapeDtypeStruct((t, d), BF16)
    return pl.pallas_call(
        functools.partial(_norm_mod_kernel, transpose=transpose),
        grid=(t // tm,),
        in_specs=[pl.BlockSpec((tm, d), lambda i: (i, 0)),
                  pl.BlockSpec((1, d), lambda i: (0, 0)), seg_spec, seg_spec],
        out_specs=out_spec,
        out_shape=out_shape,
        compiler_params=_cparams(("parallel",)),
        name="norm_mod_t" if transpose else "norm_mod",
    )(x, g, sc, sh)


def _mm_kernel(a_ref, b_ref, o_ref):
    o_ref[...] = jnp.dot(a_ref[...], b_ref[...], preferred_element_type=F32).astype(o_ref.dtype)


def matmul(a, b, tm, tn, out_dtype=F32, name="matmul", col0=0, n=None):
    m, k = a.shape
    n = b.shape[1] if n is None else n
    tm = min(tm, m)
    c0 = col0 // tn
    return pl.pallas_call(
        _mm_kernel,
        grid=(m // tm, n // tn),
        in_specs=[pl.BlockSpec((tm, k), lambda i, j: (i, 0)),
                  pl.BlockSpec((k, tn), lambda i, j: (0, c0 + j))],
        out_specs=pl.BlockSpec((tm, tn), lambda i, j: (i, j)),
        out_shape=jax.ShapeDtypeStruct((m, n), out_dtype),
        compiler_params=_cparams(("parallel", "parallel")),
        name=name,
    )(a, b)


def _mm_layer_kernel(a_ref, b_ref, buf_ref, o_ref, cache_ref):
    del buf_ref
    acc = jnp.dot(a_ref[...], b_ref[...], preferred_element_type=F32)
    o_ref[...] = acc.astype(o_ref.dtype)
    heads = pltpu.einshape("m(hd)->mhd", acc, h=cache_ref.shape[2])
    cache_ref[...] = heads.reshape(cache_ref.shape)


def matmul_into_layer(a, b, col0, buf, layer, tm, name):
    m, k = a.shape
    _, _, seq, heads, hd = buf.shape
    n = heads * hd
    tm = min(tm, m)
    sp = tm // seq
    c0 = col0 // n
    return pl.pallas_call(
        _mm_layer_kernel,
        grid=(m // tm,),
        in_specs=[pl.BlockSpec((tm, k), lambda i: (i, 0)),
                  pl.BlockSpec((k, n), lambda i: (0, c0)),
                  pl.BlockSpec(memory_space=pl.ANY)],
        out_specs=[pl.BlockSpec((tm, n), lambda i: (i, 0)),
                   pl.BlockSpec((sp, None, seq, heads, hd), lambda i: (i, layer, 0, 0, 0))],
        out_shape=[jax.ShapeDtypeStruct((m, n), BF16), jax.ShapeDtypeStruct(buf.shape, F32)],
        input_output_aliases={2: 1},
        compiler_params=_cparams(("parallel",)),
        name=name,
    )(a, b, buf)


def _ctx_attn_kernel(q_ref, k_ref, v_ref, o_ref):
    scale = NA_HEAD_DIM ** -0.5
    hd = NA_HEAD_DIM
    for h in range(NA_HEADS):
        hs = slice(h * hd, (h + 1) * hd)
        q = q_ref[:, hs].astype(BF16)
        k = k_ref[:, hs].astype(BF16)
        v = v_ref[:, hs].astype(BF16)
        s = _nt_dot(q, k) * scale
        m = jnp.max(s, axis=-1, keepdims=True)
        p = jnp.exp(s - m)
        p = p / jnp.sum(p, axis=-1, keepdims=True)
        o_ref[:, hs] = jnp.dot(p.astype(BF16), v, preferred_element_type=F32).astype(o_ref.dtype)


def context_attention(q, k, v, nseq, seq):
    w = NA_WIDTH
    spec = pl.BlockSpec((seq, w), lambda b: (b, 0))
    return pl.pallas_call(
        _ctx_attn_kernel,
        grid=(nseq,),
        in_specs=[spec, spec, spec],
        out_specs=spec,
        out_shape=jax.ShapeDtypeStruct((nseq * seq, w), BF16),
        compiler_params=_cparams(("parallel",)),
        name="ctx_attention",
    )(q, k, v)


def _na_kernel(q_ref, k_ref, v_ref, kc_ref, vc_ref, bias_ref, o_ref, kb_scr, vb_scr, *, rows):
    scale = NA_HEAD_DIM ** -0.5
    band = NA_WIN_ROWS * GRID_W
    kb_scr[...] = k_ref[...].astype(BF16)
    vb_scr[...] = v_ref[...].astype(BF16)
    kc = kc_ref[...].astype(BF16)
    vc = vc_ref[...].astype(BF16)

    def rows_step(it, carry):
        r0 = it * NA_ROW_GROUP
        q0 = pl.multiple_of(r0 * GRID_W, NA_ROW_GROUP * GRID_W)
        q = q_ref[pl.ds(q0, NA_ROW_GROUP * GRID_W), :].astype(BF16)
        k0s = []
        s_rows = []
        for j in range(NA_ROW_GROUP):
            r = r0 + j
            rs = jnp.clip(r - NA_WIN_ROWS // 2, 0, rows - NA_WIN_ROWS)
            k0 = pl.multiple_of(rs * GRID_W, GRID_W)
            k0s.append(k0)
            s_rows.append(_nt_dot(q[j * GRID_W:(j + 1) * GRID_W], kb_scr[pl.ds(k0, band), :]) * scale
                          + bias_ref[r - rs])
        s_win = jnp.concatenate(s_rows, axis=0)
        s_ctx = _nt_dot(q, kc) * scale
        m = jnp.maximum(jnp.max(s_win, axis=-1, keepdims=True), jnp.max(s_ctx, axis=-1, keepdims=True))
        p_win = jnp.exp(s_win - m)
        p_ctx = jnp.exp(s_ctx - m)
        den = jnp.sum(p_win, axis=-1, keepdims=True) + jnp.sum(p_ctx, axis=-1, keepdims=True)
        p_win = (p_win / den).astype(BF16)
        o_ctx = jnp.dot((p_ctx / den).astype(BF16), vc, preferred_element_type=F32)
        for j in range(NA_ROW_GROUP):
            o = jnp.dot(p_win[j * GRID_W:(j + 1) * GRID_W], vb_scr[pl.ds(k0s[j], band), :],
                        preferred_element_type=F32) + o_ctx[j * GRID_W:(j + 1) * GRID_W]
            o_ref[pl.ds(pl.multiple_of((r0 + j) * GRID_W, GRID_W), GRID_W), :] = o.astype(o_ref.dtype)
        return carry

    lax.fori_loop(0, rows // NA_ROW_GROUP, rows_step, 0)


def _na_bias_table(rel_bias):
    col = np.arange(GRID_W)
    cstart = np.clip(col - NA_WIN_COLS // 2, 0, GRID_W - NA_WIN_COLS)
    valid = (col[None, :] >= cstart[:, None]) & (col[None, :] < cstart[:, None] + NA_WIN_COLS)
    dcol = col[None, :] - col[:, None] + NA_WIN_COLS - 1
    onehot = ((np.arange(2 * NA_WIN_COLS - 1)[:, None, None] == dcol[None]) & valid[None]).astype(np.float32)
    tabc = jnp.einsum("hrk,kcd->hrcd", rel_bias, jnp.asarray(onehot), precision=lax.Precision.HIGHEST)
    tabc = tabc + jnp.asarray(np.where(valid, 0.0, NEG_BIG).astype(np.float32))
    wr = NA_WIN_ROWS
    tab = jnp.stack([tabc[:, wr - 1 - o:2 * wr - 1 - o].transpose(0, 2, 1, 3) for o in range(wr)], axis=1)
    return tab.reshape(rel_bias.shape[0], wr, GRID_W, wr * GRID_W).astype(F32)


def neighbourhood_attention(qkv, cache_k, cache_v, layer, bias_tab, nseq, seq):
    hd = NA_HEAD_DIM
    past = cache_k.shape[2]
    rows = seq // GRID_W
    assert seq % GRID_W == 0 and rows >= NA_WIN_ROWS and rows % NA_ROW_GROUP == 0, (seq, rows)
    band = NA_WIN_ROWS * GRID_W
    return pl.pallas_call(
        functools.partial(_na_kernel, rows=rows),
        grid=(nseq, NA_HEADS),
        in_specs=[pl.BlockSpec((seq, hd), lambda b, h: (b, h)),
                  pl.BlockSpec((seq, hd), lambda b, h: (b, NA_HEADS + h)),
                  pl.BlockSpec((seq, hd), lambda b, h: (b, 2 * NA_HEADS + h)),
                  pl.BlockSpec((None, None, past, hd), lambda b, h: (b, layer, 0, h)),
                  pl.BlockSpec((None, None, past, hd), lambda b, h: (b, layer, 0, h)),
                  pl.BlockSpec((None, NA_WIN_ROWS, GRID_W, band), lambda b, h: (h, 0, 0, 0))],
        out_specs=pl.BlockSpec((seq, hd), lambda b, h: (b, h)),
        out_shape=jax.ShapeDtypeStruct((nseq * seq, NA_WIDTH), BF16),
        scratch_shapes=[pltpu.VMEM((seq, hd), BF16), pltpu.VMEM((seq, hd), BF16)],
        compiler_params=_cparams(("parallel", "parallel")),
        name="nbr_attention",
    )(qkv, qkv, qkv, cache_k, cache_v, bias_tab)


def _conv_kernel(x_ref, w_ref, b_ref, o_ref, *, seq):
    x = x_ref[...]
    n = x.shape[0]
    pos = lax.rem(lax.broadcasted_iota(jnp.int32, x.shape, 0), seq)
    xm2 = jnp.where(pos >= 2, pltpu.roll(x, 2, 0), 0.0)
    xm1 = jnp.where(pos >= 1, pltpu.roll(x, 1, 0), 0.0)
    xp1 = jnp.where(pos < seq - 1, pltpu.roll(x, n - 1, 0), 0.0)
    w = w_ref[...]
    y = w[0:1] * xm2 + w[1:2] * xm1 + w[2:3] * x + w[3:4] * xp1 + b_ref[...]
    o_ref[...] = y * jax.nn.sigmoid(y)


def conv_silu(ssm, conv_w, conv_b, nseq, seq, rows=4096):
    tc = 256
    c0 = SSM_INNER // tc
    t = nseq * seq
    tr = min(max(rows, seq), t)
    out = pl.pallas_call(
        functools.partial(_conv_kernel, seq=seq),
        grid=(t // tr, SSM_CONV_DIM // tc),
        in_specs=[pl.BlockSpec((tr, tc), lambda s, j: (s, c0 + j)),
                  pl.BlockSpec((4, tc), lambda s, j: (0, j)),
                  pl.BlockSpec((1, tc), lambda s, j: (0, j))],
        out_specs=pl.BlockSpec((tr, tc), lambda s, j: (s, j)),
        out_shape=jax.ShapeDtypeStruct((t, SSM_CONV_DIM), F32),
        compiler_params=_cparams(("parallel", "parallel")),
        name="conv_silu",
    )(ssm, conv_w, conv_b)
    return out.reshape(nseq, seq, SSM_CONV_DIM)


def _select_columns(x, sel):
    hi = x.astype(BF16)
    r = x - hi.astype(F32)
    mid = r.astype(BF16)
    lo = (r - mid.astype(F32)).astype(BF16)
    return (jnp.dot(hi, sel, preferred_element_type=F32) + jnp.dot(mid, sel, preferred_element_type=F32)
            + jnp.dot(lo, sel, preferred_element_type=F32))


def _ssd_kernel(*refs, has_init, nchunk):
    if has_init:
        xc_ref, dt_ref, tri_ref, dtb_ref, alog_ref, sel_n_ref, sel_p_ref, init_ref, y_ref, fin_ref, st_scr = refs
    else:
        xc_ref, dt_ref, tri_ref, dtb_ref, alog_ref, sel_n_ref, sel_p_ref, y_ref, fin_ref, st_scr = refs
        init_ref = None
    c = pl.program_id(2)

    @pl.when(c == 0)
    def _():
        if has_init:
            st_scr[...] = init_ref[...]
        else:
            st_scr[...] = jnp.zeros_like(st_scr)

    p = SSM_HEAD_DIM
    n = SSM_STATE
    xc = xc_ref[...]
    dt = jax.nn.softplus(dt_ref[...] + dtb_ref[...])
    a = dt * (-jnp.exp(alog_ref[...]))
    tri = tri_ref[...]
    mask = tri > 0.5
    cum = jnp.dot(tri, a, preferred_element_type=F32, precision=lax.Precision.HIGHEST)
    cum_t = cum.T
    cum_x = _select_columns(cum, sel_n_ref[...])
    dt_x = _select_columns(dt, sel_p_ref[...])
    xd_all = xc[:, :SSM_INNER] * dt_x
    rep = SSM_HEADS // SSM_GROUPS
    gmat = []
    bmat = []
    cmat = []
    for g in range(SSM_GROUPS):
        b_g = xc[:, SSM_INNER + g * n:SSM_INNER + (g + 1) * n].astype(BF16)
        c_g = xc[:, SSM_INNER + SSM_BC + g * n:SSM_INNER + SSM_BC + (g + 1) * n].astype(BF16)
        bmat.append(b_g)
        cmat.append(c_g)
        gmat.append(_nt_dot(c_g, b_g))
    for h in range(SSM_HEADS):
        g = h // rep
        cum_h = cum_x[:, h * n:(h + 1) * n]
        decay = jnp.exp(jnp.where(mask, cum_h - cum_t[h:h + 1, :], -jnp.inf))
        xd = xd_all[:, h * p:(h + 1) * p]
        y_diag = jnp.dot((gmat[g] * decay).astype(BF16), xd.astype(BF16), preferred_element_type=F32)
        tot = jnp.min(cum_h, axis=0, keepdims=True)
        e_end = jnp.exp(tot[:, :p] - cum_h[:, :p])
        chunk_state = _tn_dot((xd * e_end).astype(BF16), bmat[g])
        prev = st_scr[h]
        y_off = _nt_dot(cmat[g], prev.astype(BF16)) * jnp.exp(cum_h[:, :p])
        y_ref[:, h * p:(h + 1) * p] = y_diag + y_off
        st_scr[h] = prev * jnp.exp(tot) + chunk_state

    @pl.when(c == nchunk - 1)
    def _():
        fin_ref[...] = st_scr[...]


def ssd_scan(xc, ssm, dt_bias, a_log, init, layer):
    nseq, seq, _ = xc.shape
    q = SSM_CHUNK
    nc = seq // q
    idx = np.arange(q)
    tri = jnp.asarray(np.stack([idx[None, :] <= idx[:, None], idx[None, :] >= idx[:, None]]).astype(np.float32))

    def cc(d, c):
        return c + d * (nc - 1 - 2 * c)

    in_specs = [pl.BlockSpec((None, q, SSM_CONV_DIM), lambda d, s, c: (s, cc(d, c), 0)),
                pl.BlockSpec((None, q, LANES), lambda d, s, c: (s, cc(d, c), DT_BLOCK0 + d)),
                pl.BlockSpec((None, q, q), lambda d, s, c: (d, 0, 0)),
                pl.BlockSpec((None, 1, LANES), lambda d, s, c: (d, 0, 0)),
                pl.BlockSpec((None, 1, LANES), lambda d, s, c: (d, 0, 0)),
                pl.BlockSpec((LANES, SSM_HEADS * SSM_STATE), lambda d, s, c: (0, 0)),
                pl.BlockSpec((LANES, SSM_INNER), lambda d, s, c: (0, 0))]
    head = np.arange(LANES)[:, None]
    sel_n = jnp.asarray(head == np.arange(SSM_HEADS * SSM_STATE)[None, :] // SSM_STATE, BF16)
    sel_p = jnp.asarray(head == np.arange(SSM_INNER)[None, :] // SSM_HEAD_DIM, BF16)
    args = [xc, ssm.reshape(nseq, seq, SSM_COLS), tri, dt_bias, a_log, sel_n, sel_p]
    if init is not None:
        in_specs.append(pl.BlockSpec((None, None, None, SSM_HEADS, SSM_HEAD_DIM, SSM_STATE),
                                     lambda d, s, c: (s, layer, d, 0, 0, 0)))
        args.append(init)
    return pl.pallas_call(
        functools.partial(_ssd_kernel, has_init=init is not None, nchunk=nc),
        grid=(2, nseq, nc),
        in_specs=in_specs,
        out_specs=[pl.BlockSpec((None, None, q, SSM_INNER), lambda d, s, c: (d, s, cc(d, c), 0)),
                   pl.BlockSpec((None, None, SSM_HEADS, SSM_HEAD_DIM, SSM_STATE), lambda d, s, c: (s, d, 0, 0, 0))],
        out_shape=[jax.ShapeDtypeStruct((2, nseq, seq, SSM_INNER), F32),
                   jax.ShapeDtypeStruct((nseq, 2, SSM_HEADS, SSM_HEAD_DIM, SSM_STATE), F32)],
        scratch_shapes=[pltpu.VMEM((SSM_HEADS, SSM_HEAD_DIM, SSM_STATE), F32)],
        compiler_params=_cparams(("parallel", "parallel", "arbitrary")),
        name="ssd_scan",
    )(*args)


def _ssm_finish_kernel(y_ref, xs_ref, z_ref, dsk_ref, g_ref, o_ref):
    z = z_ref[...]
    y = (y_ref[0] + y_ref[1] + dsk_ref[...] * xs_ref[...]) * (z * jax.nn.sigmoid(z))
    y = y * lax.rsqrt(jnp.mean(y * y, axis=-1, keepdims=True) + EPS) * g_ref[...]
    o_ref[...] = y.astype(o_ref.dtype)


def ssm_finish(y, xc, ssm, d_skip, g, tm=512):
    t = ssm.shape[0]
    w = SSM_INNER
    return pl.pallas_call(
        _ssm_finish_kernel,
        grid=(t // tm,),
        in_specs=[pl.BlockSpec((2, tm, w), lambda i: (0, i, 0)),
                  pl.BlockSpec((tm, w), lambda i: (i, 0)),
                  pl.BlockSpec((tm, w), lambda i: (i, 0)),
                  pl.BlockSpec((1, w), lambda i: (0, 0)),
                  pl.BlockSpec((1, w), lambda i: (0, 0))],
        out_specs=pl.BlockSpec((tm, w), lambda i: (i, 0)),
        out_shape=jax.ShapeDtypeStruct((t, w), BF16),
        compiler_params=_cparams(("parallel",)),
        name="ssm_finish",
    )(y, xc, ssm, d_skip, g)


def _gmlp_kernel(u_ref, v_ref, g_ref, w_ref, b_ref, o_ref):
    v = jax.nn.gelu(v_ref[...].astype(F32))
    v = (v * lax.rsqrt(jnp.mean(v * v, axis=-1, keepdims=True) + EPS) * g_ref[...]).astype(BF16)
    q = MLP_CHUNK
    gw = MLP_WIDTH // MLP_GROUPS
    for ch in range(v.shape[0] // q):
        for g in range(MLP_GROUPS):
            mixed = jnp.dot(w_ref[g], v[ch * q:(ch + 1) * q, g * gw:(g + 1) * gw],
                            preferred_element_type=F32) + b_ref[g]
            u = jax.nn.gelu(u_ref[ch * q:(ch + 1) * q, g * gw:(g + 1) * gw].astype(F32))
            o_ref[ch * q:(ch + 1) * q, g * gw:(g + 1) * gw] = (u * mixed).astype(o_ref.dtype)


def chunk_mlp(mlp, g, w_s, b_s, tm=512):
    t = mlp.shape[0]
    w = MLP_WIDTH
    return pl.pallas_call(
        _gmlp_kernel,
        grid=(t // tm,),
        in_specs=[pl.BlockSpec((tm, w), lambda i: (i, 0)),
                  pl.BlockSpec((tm, w), lambda i: (i, 1)),
                  pl.BlockSpec((1, w), lambda i: (0, 0)),
                  pl.BlockSpec(w_s.shape, lambda i: (0, 0, 0)),
                  pl.BlockSpec(b_s.shape, lambda i: (0, 0, 0))],
        out_specs=pl.BlockSpec((tm, w), lambda i: (i, 0)),
        out_shape=jax.ShapeDtypeStruct((t, w), BF16),
        compiler_params=_cparams(("parallel",)),
        name="chunk_mlp",
    )(mlp, mlp, g, w_s, b_s)


def _merge_kernel(a0_ref, a1_ref, a2_ref, w_ref, g0_ref, g1_ref, g2_ref, o_ref):
    acc = jax.nn.sigmoid(g0_ref[...].astype(F32)) * jnp.dot(a0_ref[...], w_ref[0], preferred_element_type=F32)
    acc += jax.nn.sigmoid(g1_ref[...].astype(F32)) * jnp.dot(a1_ref[...], w_ref[1], preferred_element_type=F32)
    acc += jax.nn.sigmoid(g2_ref[...].astype(F32)) * jnp.dot(a2_ref[...], w_ref[2], preferred_element_type=F32)
    o_ref[...] = acc.astype(o_ref.dtype)


def merge_branches(o_na, o_ssm, o_mlp, w_br, gates, tm=1024, tn=512):
    t, k = o_na.shape
    tm = min(tm, t)
    d = w_br.shape[2]
    nb = d // tn
    a_spec = pl.BlockSpec((tm, k), lambda i, j: (i, 0))
    return pl.pallas_call(
        _merge_kernel,
        grid=(t // tm, nb),
        in_specs=[a_spec, a_spec, a_spec,
                  pl.BlockSpec((N_BRANCH, k, tn), lambda i, j: (0, 0, j)),
                  pl.BlockSpec((tm, tn), lambda i, j: (i, j)),
                  pl.BlockSpec((tm, tn), lambda i, j: (i, nb + j)),
                  pl.BlockSpec((tm, tn), lambda i, j: (i, 2 * nb + j))],
        out_specs=pl.BlockSpec((tm, tn), lambda i, j: (i, j)),
        out_shape=jax.ShapeDtypeStruct((t, d), BF16),
        compiler_params=_cparams(("parallel", "parallel")),
        name="merge_branches",
    )(o_na, o_ssm, o_mlp, w_br, gates, gates, gates)


def _out_proj_kernel(a_ref, w_ref, x_ref, gt_ref, o_ref):
    o_ref[...] = x_ref[...] + gt_ref[...] * jnp.dot(a_ref[...], w_ref[...], preferred_element_type=F32)


def out_proj(merged, w_out, x, gt, seg_len, tm=2048, tn=512):
    t, k = merged.shape
    d = w_out.shape[1]
    tm = min(tm, seg_len)
    spt = seg_len // tm
    return pl.pallas_call(
        _out_proj_kernel,
        grid=(t // tm, d // tn),
        in_specs=[pl.BlockSpec((tm, k), lambda i, j: (i, 0)),
                  pl.BlockSpec((k, tn), lambda i, j: (0, j)),
                  pl.BlockSpec((tm, tn), lambda i, j: (i, j)),
                  pl.BlockSpec((None, 1, tn), lambda i, j: (i // spt, 0, j))],
        out_specs=pl.BlockSpec((tm, tn), lambda i, j: (i, j)),
        out_shape=jax.ShapeDtypeStruct((t, d), F32),
        compiler_params=_cparams(("parallel", "parallel")),
        name="out_proj",
    )(merged, w_out, x, gt)


def _peer_scores_kernel(wq_ref, h_ref, key_ref, o_ref):
    qv = jnp.dot(wq_ref[...], h_ref[...], preferred_element_type=F32).astype(BF16)
    dq = PEER_QDIM // 2
    for hk in range(2 * PEER_HEADS):
        o_ref[hk * PEER_KEYS:(hk + 1) * PEER_KEYS, :] = jnp.dot(
            key_ref[hk], qv[hk * dq:(hk + 1) * dq, :], preferred_element_type=F32)


def peer_scores(wq_t, h_t, keys, tl=512):
    d, t = h_t.shape
    rows = 2 * PEER_HEADS * PEER_KEYS
    return pl.pallas_call(
        _peer_scores_kernel,
        grid=(t // tl,),
        in_specs=[pl.BlockSpec(wq_t.shape, lambda i: (0, 0)),
                  pl.BlockSpec((d, tl), lambda i: (0, i)),
                  pl.BlockSpec(keys.shape, lambda i: (0, 0, 0))],
        out_specs=pl.BlockSpec((rows, tl), lambda i: (0, i)),
        out_shape=jax.ShapeDtypeStruct((rows, t), F32),
        compiler_params=_cparams(("parallel",)),
        name="peer_scores",
    )(wq_t, h_t, keys)


def _top16(s, exact):
    nk = s.shape[0]
    kio = lax.broadcasted_iota(jnp.int32, s.shape, 0).astype(F32)
    rank = jnp.full(s.shape, float(PEER_TOPK), F32)
    vals = []
    for a in range(PEER_TOPK):
        m = jnp.max(s, axis=0, keepdims=True)
        hit = s == m
        if exact:
            hit = kio == jnp.min(jnp.where(hit, kio, float(nk)), axis=0, keepdims=True)
        rank = jnp.where(hit, float(a), rank)
        s = jnp.where(hit, -jnp.inf, s)
        vals.append(m)
    return jnp.concatenate(vals, axis=0), rank


def _peer_select_heads(s_ref, rank2_ref, cnt_ref, e1_ref, e2_ref, exact):
    nk = PEER_KEYS
    k = PEER_TOPK
    bad = jnp.zeros((1, s_ref.shape[1]), F32)
    for h in range(PEER_HEADS):
        s1 = s_ref[(2 * h) * nk:(2 * h + 1) * nk, :]
        s2 = s_ref[(2 * h + 1) * nk:(2 * h + 2) * nk, :]
        tv1, rank1 = _top16(s1, exact)
        tv2, rank2 = _top16(s2, exact)
        sub = lax.broadcasted_iota(jnp.int32, (8, s1.shape[1]), 0).astype(F32)
        pieces = []
        pos_pieces = []
        for a in range(k // 2):
            nb = k // (a + 1)
            for b0 in range(0, nb, 8):
                vals = tv1[a:a + 1, :] + tv2[b0:b0 + 8, :]
                pieces.append(vals if nb - b0 >= 8 else jnp.where(sub < float(nb - b0), vals, -jnp.inf))
                pos_pieces.append(sub + float(a * k + b0))
        pieces.append(tv1[k // 2:k, :] + tv2[0:1, :])
        pos_pieces.append((sub + float(k // 2)) * float(k))
        cand = jnp.concatenate(pieces, axis=0)
        pio = jnp.concatenate(pos_pieces, axis=0)
        cnt_cells = jnp.zeros(cand.shape, F32)
        zsum = jnp.zeros((1, cand.shape[1]), F32)
        top = tv1[0:1, :] + tv2[0:1, :]
        for _ in range(k):
            m = jnp.max(cand, axis=0, keepdims=True)
            hit = cand == m
            if exact:
                hit = pio == jnp.min(jnp.where(hit, pio, float(k * k)), axis=0, keepdims=True)
            cand = jnp.where(hit, -jnp.inf, cand)
            cnt_cells = jnp.where(hit, 1.0, cnt_cells)
            zsum = zsum + jnp.exp(m - top)
        cnt = jnp.zeros(s1.shape, F32)
        for a in range(k):
            if a < k // 2:
                lo = 0 if a == 0 else 8 * (a + 1)
                cnt_a = jnp.sum(cnt_cells[lo:8 * (a + 2), :], axis=0, keepdims=True)
            else:
                r = 8 * (k // 2 + 1) + a - k // 2
                cnt_a = cnt_cells[r:r + 1, :]
            cnt = jnp.where(rank1 == float(a), cnt_a, cnt)
        if not exact:
            n1 = jnp.sum(jnp.where(rank1 < float(k), 1.0, 0.0), axis=0, keepdims=True)
            n2 = jnp.sum(jnp.where(rank2 < float(k), 1.0, 0.0), axis=0, keepdims=True)
            nc = jnp.sum(cnt_cells, axis=0, keepdims=True)
            bad = bad + jnp.abs(n1 - float(k)) + jnp.abs(n2 - float(k)) + jnp.abs(nc - float(k))
        rank2_ref[h * nk:(h + 1) * nk, :] = rank2.astype(rank2_ref.dtype)
        cnt_ref[h * nk:(h + 1) * nk, :] = cnt
        e1_ref[h * nk:(h + 1) * nk, :] = jnp.exp(s1 - tv1[0:1, :])
        e2_ref[h * nk:(h + 1) * nk, :] = (jnp.exp(s2 - tv2[0:1, :]) / zsum).astype(e2_ref.dtype)
    return bad


def _peer_select_kernel(s_ref, rank2_ref, cnt_ref, e1_ref, e2_ref):
    bad = _peer_select_heads(s_ref, rank2_ref, cnt_ref, e1_ref, e2_ref, exact=False)

    @pl.when(jnp.max(bad) > 0.0)
    def _():
        _peer_select_heads(s_ref, rank2_ref, cnt_ref, e1_ref, e2_ref, exact=True)


def peer_select(scores, tl=128):
    rows, t = scores.shape
    half = rows // 2
    spec = pl.BlockSpec((half, tl), lambda i: (0, i))
    shp = jax.ShapeDtypeStruct((half, t), F32)
    shp_b = jax.ShapeDtypeStruct((half, t), BF16)
    return pl.pallas_call(
        _peer_select_kernel,
        grid=(t // tl,),
        in_specs=[pl.BlockSpec((rows, tl), lambda i: (0, i))],
        out_specs=[spec, spec, spec, spec],
        out_shape=[shp_b, shp, shp, shp_b],
        compiler_params=_cparams(("parallel",)),
        name="peer_select",
    )(scores)


def _peer_dense_kernel(h_ref, u_ref, vta_ref, vtb_ref, rank2_ref, cnt_ref, e1_ref, e2_ref, o_ref, za_scr, zb_scr,
                       *, nstep, rows_per_half, lane_tile):
    e = pl.program_id(1)

    @pl.when(e == 0)
    def _():
        o_ref[...] = jnp.zeros_like(o_ref)
        zb_scr[...] = jnp.zeros_like(zb_scr)

    nk = PEER_KEYS
    eh = rows_per_half * nk
    nsub = rows_per_half // SUB_KEYS
    mrows = o_ref.shape[0] // nsub
    lanes = [slice(lt * lane_tile, (lt + 1) * lane_tile) for lt in range(o_ref.shape[1] // lane_tile)]
    for half, (z_new, z_old, vt_ref) in enumerate(((za_scr, zb_scr, vta_ref), (zb_scr, za_scr, vtb_ref))):
        for sc in range(nsub):
            i0 = half * rows_per_half + sc * SUB_KEYS
            acts = [jax.nn.gelu(jnp.dot(u_ref[i0 * nk:(i0 + SUB_KEYS) * nk, :], h_ref[:, ls],
                                        preferred_element_type=F32)) for ls in lanes]
            for ls in lanes:
                o_ref[sc * mrows:(sc + 1) * mrows, ls] += jnp.dot(
                    vt_ref[sc * mrows:(sc + 1) * mrows, :eh], z_old[:, ls], preferred_element_type=F32)
            for ls, act in zip(lanes, acts):
                for k in range(SUB_KEYS):
                    w = None
                    for h in range(PEER_HEADS):
                        row = h * nk + e * (2 * rows_per_half) + i0 + k
                        cnt_b = jnp.broadcast_to(cnt_ref[pl.ds(row, 1), ls], (BF16_ROWS, lane_tile))
                        e1_b = jnp.broadcast_to(e1_ref[pl.ds(row, 1), ls], (BF16_ROWS, lane_tile))
                        cnt_b = jnp.concatenate([cnt_b.astype(BF16)] * (nk // BF16_ROWS), axis=0)
                        e1_b = jnp.concatenate([e1_b.astype(BF16)] * (nk // BF16_ROWS), axis=0)
                        sel = rank2_ref[h * nk:(h + 1) * nk, ls] < cnt_b
                        term = jnp.where(sel, e2_ref[h * nk:(h + 1) * nk, ls], jnp.zeros((), BF16)) * e1_b
                        w = term if w is None else w + term
                    r0 = (sc * SUB_KEYS + k) * nk
                    z_new[r0:r0 + nk, ls] = w * act[k * nk:(k + 1) * nk, :].astype(BF16)

    @pl.when(e == nstep - 1)
    def _():
        for ls in lanes:
            o_ref[:, ls] += jnp.dot(vtb_ref[:, eh:], zb_scr[:, ls], preferred_element_type=F32)


def peer_dense(h_t, u_tab, v_t, rank2, cnt, e1, e2, tm=512, et=1024, lane_tile=256):
    d, t = h_t.shape
    ne = u_tab.shape[0]
    assert ne == PEER_EXPERTS and ne % et == 0 and t % tm == 0 and tm % lane_tile == 0, (ne, et, t, tm)
    assert (et // 2) % (SUB_KEYS * PEER_KEYS) == 0 and d % (et // 2 // (SUB_KEYS * PEER_KEYS)) == 0
    nstep = ne // et
    eh = et // 2
    map_spec = pl.BlockSpec((rank2.shape[0], tm), lambda i, e: (0, i))
    return pl.pallas_call(
        functools.partial(_peer_dense_kernel, nstep=nstep, rows_per_half=eh // PEER_KEYS, lane_tile=lane_tile),
        grid=(t // tm, nstep),
        in_specs=[pl.BlockSpec((d, tm), lambda i, e: (0, i)),
                  pl.BlockSpec((et, d), lambda i, e: (e, 0)),
                  pl.BlockSpec((d, eh), lambda i, e: (0, jnp.maximum(2 * e - 1, 0))),
                  pl.BlockSpec((d, et), lambda i, e: (0, e)),
                  map_spec, map_spec, map_spec, map_spec],
        out_specs=pl.BlockSpec((d, tm), lambda i, e: (0, i)),
        out_shape=jax.ShapeDtypeStruct((d, t), F32),
        scratch_shapes=[pltpu.VMEM((eh, tm), BF16), pltpu.VMEM((eh, tm), BF16)],
        compiler_params=_cparams(("parallel", "arbitrary")),
        name="peer_dense",
    )(h_t, u_tab, v_t, v_t, rank2, cnt, e1, e2)


def _peer_resid_norm_kernel(x_ref, p_ref, gt_ref, g_ref, sc_ref, sh_ref, x_out_ref, h_ref):
    x = x_ref[...] + gt_ref[...] * p_ref[...].T
    x_out_ref[...] = x
    y = x * lax.rsqrt(jnp.mean(x * x, axis=-1, keepdims=True) + EPS) * g_ref[...]
    h_ref[...] = (y * (1.0 + sc_ref[...]) + sh_ref[...]).astype(h_ref.dtype)


def _peer_resid_final_kernel(x_ref, p_ref, gt_ref, g_ref, o_ref):
    x = x_ref[...] + gt_ref[...] * p_ref[...].T
    o_ref[...] = x * lax.rsqrt(jnp.mean(x * x, axis=-1, keepdims=True) + EPS) * g_ref[...]


def peer_residual_norm(x, peer_t, gt, g, sc, sh, seg_len, tm=512):
    t, d = x.shape
    spt = seg_len // tm
    row_spec = pl.BlockSpec((tm, d), lambda i: (i, 0))
    seg_spec = pl.BlockSpec((None, 1, d), lambda i: (i // spt, 0, 0))
    in_specs = [row_spec, pl.BlockSpec((d, tm), lambda i: (0, i)), seg_spec, pl.BlockSpec((1, d), lambda i: (0, 0))]
    if sc is None:
        return pl.pallas_call(
            _peer_resid_final_kernel,
            grid=(t // tm,),
            in_specs=in_specs,
            out_specs=row_spec,
            out_shape=jax.ShapeDtypeStruct((t, d), F32),
            compiler_params=_cparams(("parallel",)),
            name="peer_resid_final",
        )(x, peer_t, gt, g)
    return pl.pallas_call(
        _peer_resid_norm_kernel,
        grid=(t // tm,),
        in_specs=in_specs + [seg_spec, seg_spec],
        out_specs=[row_spec, row_spec],
        out_shape=[jax.ShapeDtypeStruct((t, d), F32), jax.ShapeDtypeStruct((t, d), BF16)],
        compiler_params=_cparams(("parallel",)),
        name="peer_resid_norm",
    )(x, peer_t, gt, g, sc, sh)


def _layer(x, h, mod, lw, layer, nseq, seq, ctx):
    sh1, sc1, gt1, sh2, sc2, gt2 = mod
    seg_len = x.shape[0] // sh1.shape[0]
    ssm = matmul(h, lw["w_ssm"], 1024, 128 * 11, name="proj_ssm")
    mlp = matmul(h, lw["w_mlp"], 1024, 1024, out_dtype=BF16, name="proj_mlp")
    gates = matmul(h, lw["w_gate"], 1024, 1024, out_dtype=BF16, name="proj_gate")
    if ctx[0] == "context":
        q = matmul(h, lw["w_qkv"], 1024, 1024, out_dtype=BF16, name="proj_q", col0=0, n=NA_WIDTH)
        k, kbuf = matmul_into_layer(h, lw["w_qkv"], NA_WIDTH, ctx[1], layer, 1024, "proj_k")
        v, vbuf = matmul_into_layer(h, lw["w_qkv"], 2 * NA_WIDTH, ctx[2], layer, 1024, "proj_v")
        o_na = context_attention(q, k, v, nseq, seq)
        init = None
    else:
        _, cache_k, cache_v, init = ctx
        qkv = matmul(h, lw["w_qkv"], 1024, 1024, out_dtype=BF16, name="proj_qkv")
        o_na = neighbourhood_attention(qkv, cache_k, cache_v, layer, lw["na_bias"], nseq, seq)
    xc = conv_silu(ssm, lw["conv_w"], lw["conv_b"], nseq, seq)
    y, fin = ssd_scan(xc, ssm, lw["dt_bias"], lw["a_log"], init, layer)
    t = nseq * seq
    o_ssm = ssm_finish(y.reshape(2, t, SSM_INNER), xc.reshape(t, SSM_CONV_DIM), ssm, lw["d_skip"], lw["ssm_norm_g"])
    o_mlp = chunk_mlp(mlp, lw["mlp_norm_g"], lw["mlp_w_s"], lw["mlp_b_s"])
    merged = merge_branches(o_na, o_ssm, o_mlp, lw["w_br"], gates)
    x = out_proj(merged, lw["w_out"], x, gt1, seg_len)
    h2t = norm_mod(x, lw["g_ffn"], sc2, sh2, seg_len, transpose=True)
    scores = peer_scores(lw["peer_wq_t"], h2t, lw["peer_keys"])
    rank2, cnt, e1, e2 = peer_select(scores)
    peer_t = peer_dense(h2t, lw["peer_u"], lw["peer_v_t"], rank2, cnt, e1, e2)
    extras = (kbuf, vbuf, fin) if ctx[0] == "context" else ()
    return x, peer_t, extras


def _pad_lanes(v):
    pad = LANES - v.shape[-1]
    return jnp.pad(v, [(0, 0)] * (v.ndim - 1) + [(0, pad)])[..., None, :]


def kernel(x_prompt, x_sample, cache_k, cache_v, state_ssm, c, c_ctx, w_mod, b_mod, g_norm_mix, g_norm_ffn, g_norm_final, w_in, na_rel_bias, ssm_conv_w, ssm_conv_b, ssm_dt_bias, ssm_a_log, ssm_d, ssm_norm_g, mlp_norm_g, mlp_w_s, mlp_b_s, w_br_na, w_br_ssm, w_br_mlp, w_out, peer_w_q, peer_sub_keys, peer_u, peer_v):
    batch, seq, d = x_prompt.shape
    dec_batch, dec_seq, _ = x_sample.shape
    depth = w_in.shape[0]
    past = cache_k.shape[2]

    cvec = jnp.concatenate([c_ctx[None, :], c, jnp.zeros((8 - 1 - dec_batch, d), F32)], axis=0)
    mod_all = modulation(cvec, w_mod, b_mod)

    o_q = 0
    o_z = 3 * NA_WIDTH
    o_xbc = o_z + SSM_INNER
    o_dt = o_xbc + SSM_CONV_DIM
    o_u = o_dt + 2 * SSM_HEADS
    o_g = o_u + 2 * MLP_WIDTH
    dt_pad = jnp.zeros((depth, d, LANES - SSM_HEADS), F32)
    w_ssm = jnp.concatenate([w_in[:, :, o_z:o_dt], w_in[:, :, o_dt:o_dt + SSM_HEADS], dt_pad,
                             w_in[:, :, o_dt + SSM_HEADS:o_u], dt_pad], axis=-1).astype(BF16)
    w_qkv = w_in[:, :, o_q:o_z].astype(BF16)
    w_mlp = w_in[:, :, o_u:o_g].astype(BF16)
    w_gate = w_in[:, :, o_g:].astype(BF16)
    w_br = jnp.stack([w_br_na, w_br_ssm, w_br_mlp], axis=1).astype(BF16)
    w_out_b = w_out.astype(BF16)
    peer_wq_t = jnp.swapaxes(peer_w_q, 1, 2).astype(BF16)
    peer_keys = peer_sub_keys.reshape(depth, 2 * PEER_HEADS, PEER_KEYS, PEER_QDIM // 2).astype(BF16)
    peer_u_b = peer_u.astype(BF16)
    peer_v_t = jnp.swapaxes(peer_v, 1, 2).astype(BF16)
    dt_bias = _pad_lanes(ssm_dt_bias)
    a_log = _pad_lanes(ssm_a_log)
    d_skip = jnp.repeat(ssm_d, SSM_HEAD_DIM, axis=-1)[:, None, :]
    mlp_b = jnp.broadcast_to(mlp_b_s[..., None], mlp_b_s.shape + (MLP_CHUNK,))
    mlp_w = mlp_w_s.astype(BF16)

    cache_k4 = cache_k.reshape(dec_batch, depth, past, NA_WIDTH)
    cache_v4 = cache_v.reshape(dec_batch, depth, past, NA_WIDTH)

    xp = x_prompt.reshape(batch * seq, d)
    xs = x_sample.reshape(dec_batch * dec_seq, d)
    mods = []
    for l in range(depth):
        m = mod_all[l].reshape(8, 6, d)
        mods.append(([m[0:1, j][:, None, :] for j in range(6)],
                     [m[1:1 + dec_batch, j][:, None, :] for j in range(6)]))
    kbuf = jnp.zeros((batch, depth, seq, NA_HEADS, NA_HEAD_DIM), F32)
    vbuf = jnp.zeros((batch, depth, seq, NA_HEADS, NA_HEAD_DIM), F32)
    new_s = []
    for l in range(depth):
        lw = {
            "g_mix": g_norm_mix[l][None, :], "g_ffn": g_norm_ffn[l][None, :],
            "w_qkv": w_qkv[l], "w_ssm": w_ssm[l], "w_mlp": w_mlp[l], "w_gate": w_gate[l],
            "na_bias": _na_bias_table(na_rel_bias[l]),
            "conv_w": ssm_conv_w[l], "conv_b": ssm_conv_b[l][None, :],
            "dt_bias": dt_bias[l], "a_log": a_log[l], "d_skip": d_skip[l],
            "ssm_norm_g": ssm_norm_g[l][None, :], "mlp_norm_g": mlp_norm_g[l][None, :],
            "mlp_w_s": mlp_w[l], "mlp_b_s": mlp_b[l],
            "w_br": w_br[l], "w_out": w_out_b[l],
            "peer_wq_t": peer_wq_t[l], "peer_keys": peer_keys[l], "peer_u": peer_u_b[l], "peer_v_t": peer_v_t[l],
        }
        mod_ctx, mod_lat = mods[l]
        if l == 0:
            hp = norm_mod(xp, lw["g_mix"], mod_ctx[1], mod_ctx[0], batch * seq)
            hs = norm_mod(xs, lw["g_mix"], mod_lat[1], mod_lat[0], dec_seq)
        xp, peer_p, (kbuf, vbuf, fin) = _layer(xp, hp, mod_ctx, lw, l, batch, seq, ("context", kbuf, vbuf))
        xs, peer_s, _ = _layer(xs, hs, mod_lat, lw, l, dec_batch, dec_seq,
                               ("latent", cache_k4, cache_v4, state_ssm))
        if l + 1 < depth:
            nxt_ctx, nxt_lat = mods[l + 1]
            g_next = g_norm_mix[l + 1][None, :]
            xp, hp = peer_residual_norm(xp, peer_p, mod_ctx[5], g_next, nxt_ctx[1], nxt_ctx[0], batch * seq)
            xs, hs = peer_residual_norm(xs, peer_s, mod_lat[5], g_next, nxt_lat[1], nxt_lat[0], dec_seq)
        else:
            gf = g_norm_final[None, :]
            y_prompt = peer_residual_norm(xp, peer_p, mod_ctx[5], gf, None, None, batch * seq)
            y_sample = peer_residual_norm(xs, peer_s, mod_lat[5], gf, None, None, dec_seq)
        new_s.append(fin)
    return (y_prompt.reshape(batch, seq, d), y_sample.reshape(dec_batch, dec_seq, d),
            kbuf, vbuf, jnp.stack(new_s, axis=1))
```

```python
import functools

import numpy as np
import jax
import jax.numpy as jnp
from jax import lax
from jax.experimental import pallas as pl
from jax.experimental.pallas import tpu as pltpu

F32 = jnp.float32
BF16 = jnp.bfloat16

D_MODEL = 2048
DEPTH = 4
GRID_W = 64
EPS = 1e-6
NA_HEADS = 8
NA_HEAD_DIM = 128
NA_WIDTH = NA_HEADS * NA_HEAD_DIM
NA_WIN_ROWS = 8
NA_WIN_COLS = 16
NA_ROW_GROUP = 16
SSM_HEADS = 16
SSM_HEAD_DIM = 64
SSM_INNER = SSM_HEADS * SSM_HEAD_DIM
SSM_GROUPS = 2
SSM_STATE = 128
SSM_CHUNK = 128
SSM_BC = SSM_GROUPS * SSM_STATE
SSM_CONV_DIM = SSM_INNER + 2 * SSM_BC
MLP_GROUPS = 8
MLP_WIDTH = 1024
MLP_CHUNK = 128
PEER_HEADS = 8
PEER_KEYS = 128
PEER_EXPERTS = PEER_KEYS * PEER_KEYS
PEER_QDIM = 256
PEER_TOPK = 16
N_BRANCH = 3

LANES = 128
BF16_ROWS = 16
SUB_KEYS = 2
SSM_COLS = SSM_INNER + SSM_CONV_DIM + 2 * LANES
DT_BLOCK0 = (SSM_INNER + SSM_CONV_DIM) // LANES
VMEM_LIMIT = 60 * 1024 * 1024
NEG_BIG = -1e30


def _cparams(sem):
    return pltpu.CompilerParams(dimension_semantics=sem, vmem_limit_bytes=VMEM_LIMIT)


def _nt_dot(a, b):
    return lax.dot_general(a, b, (((1,), (1,)), ((), ())), preferred_element_type=F32)


def _tn_dot(a, b):
    return lax.dot_general(a, b, (((0,), (0,)), ((), ())), preferred_element_type=F32)


def _mod_kernel(c_ref, w_ref, b_ref, o_ref):
    c = c_ref[...]
    a = c * jax.nn.sigmoid(c)
    o_ref[...] = jnp.dot(a, w_ref[...], preferred_element_type=F32,
                         precision=lax.Precision.HIGHEST) + b_ref[...]


def modulation(cvec, w_mod, b_mod):
    depth, d, n = w_mod.shape
    tn = 1024
    return pl.pallas_call(
        _mod_kernel,
        grid=(depth, n // tn),
        in_specs=[
            pl.BlockSpec((8, d), lambda l, j: (0, 0)),
            pl.BlockSpec((None, d, tn), lambda l, j: (l, 0, j)),
            pl.BlockSpec((None, 1, tn), lambda l, j: (l, 0, j)),
        ],
        out_specs=pl.BlockSpec((None, 8, tn), lambda l, j: (l, 0, j)),
        out_shape=jax.ShapeDtypeStruct((depth, 8, n), F32),
        compiler_params=_cparams(("parallel", "parallel")),
        name="modulation",
    )(cvec, w_mod, b_mod.reshape(depth, 1, n))


def _norm_mod_kernel(x_ref, g_ref, sc_ref, sh_ref, o_ref, *, transpose):
    x = x_ref[...]
    y = x * lax.rsqrt(jnp.mean(x * x, axis=-1, keepdims=True) + EPS) * g_ref[...]
    h = y * (1.0 + sc_ref[...]) + sh_ref[...]
    if transpose:
        h = h.T
    o_ref[...] = h.astype(o_ref.dtype)


def norm_mod(x, g, sc, sh, seg_len, transpose=False, tm=512):
    t, d = x.shape
    spt = seg_len // tm
    seg_spec = pl.BlockSpec((None, 1, d), lambda i: (i // spt, 0, 0))
    if transpose:
        out_spec = pl.BlockSpec((d, tm), lambda i: (0, i))
        out_shape = jax.ShapeDtypeStruct((d, t), BF16)
    else:
        out_spec = pl.BlockSpec((tm, d), lambda i: (i, 0))
        out_shape = jax.ShapeDtypeStruct((t, d), BF16)
    return pl.pallas_call(
        functools.partial(_norm_mod_kernel, transpose=transpose),
        grid=(t // tm,),
        in_specs=[pl.BlockSpec((tm, d), lambda i: (i, 0)),
                  pl.BlockSpec((1, d), lambda i: (0, 0)), seg_spec, seg_spec],
        out_specs=out_spec,
        out_shape=out_shape,
        compiler_params=_cparams(("parallel",)),
        name="norm_mod_t" if transpose else "norm_mod",
    )(x, g, sc, sh)


def _mm_kernel(a_ref, b_ref, o_ref):
    o_ref[...] = jnp.dot(a_ref[...], b_ref[...], preferred_element_type=F32).astype(o_ref.dtype)


def matmul(a, b, tm, tn, out_dtype=F32, name="matmul", col0=0, n=None):
    m, k = a.shape
    n = b.shape[1] if n is None else n
    tm = min(tm, m)
    c0 = col0 // tn
    return pl.pallas_call(
        _mm_kernel,
        grid=(m // tm, n // tn),
        in_specs=[pl.BlockSpec((tm, k), lambda i, j: (i, 0)),
                  pl.BlockSpec((k, tn), lambda i, j: (0, c0 + j))],
        out_specs=pl.BlockSpec((tm, tn), lambda i, j: (i, j)),
        out_shape=jax.ShapeDtypeStruct((m, n), out_dtype),
        compiler_params=_cparams(("parallel", "parallel")),
        name=name,
    )(a, b)


def _mm_layer_kernel(a_ref, b_ref, buf_ref, o_ref, cache_ref):
    del buf_ref
    acc = jnp.dot(a_ref[...], b_ref[...], preferred_element_type=F32)
    o_ref[...] = acc.astype(o_ref.dtype)
    heads = pltpu.einshape("m(hd)->mhd", acc, h=cache_ref.shape[2])
    cache_ref[...] = heads.reshape(cache_ref.shape)


def matmul_into_layer(a, b, col0, buf, layer, tm, name):
    m, k = a.shape
    _, _, seq, heads, hd = buf.shape
    n = heads * hd
    tm = min(tm, m)
    sp = tm // seq
    c0 = col0 // n
    return pl.pallas_call(
        _mm_layer_kernel,
        grid=(m // tm,),
        in_specs=[pl.BlockSpec((tm, k), lambda i: (i, 0)),
                  pl.BlockSpec((k, n), lambda i: (0, c0)),
                  pl.BlockSpec(memory_space=pl.ANY)],
        out_specs=[pl.BlockSpec((tm, n), lambda i: (i, 0)),
                   pl.BlockSpec((sp, None, seq, heads, hd), lambda i: (i, layer, 0, 0, 0))],
        out_shape=[jax.ShapeDtypeStruct((m, n), BF16), jax.ShapeDtypeStruct(buf.shape, F32)],
        input_output_aliases={2: 1},
        compiler_params=_cparams(("parallel",)),
        name=name,
    )(a, b, buf)


def _ctx_attn_kernel(q_ref, k_ref, v_ref, o_ref):
    scale = NA_HEAD_DIM ** -0.5
    hd = NA_HEAD_DIM
    for h in range(NA_HEADS):
        hs = slice(h * hd, (h + 1) * hd)
        q = q_ref[:, hs].astype(BF16)
        k = k_ref[:, hs].astype(BF16)
        v = v_ref[:, hs].astype(BF16)
        s = _nt_dot(q, k) * scale
        m = jnp.max(s, axis=-1, keepdims=True)
        p = jnp.exp(s - m)
        p = p / jnp.sum(p, axis=-1, keepdims=True)
        o_ref[:, hs] = jnp.dot(p.astype(BF16), v, preferred_element_type=F32).astype(o_ref.dtype)


def context_attention(q, k, v, nseq, seq):
    w = NA_WIDTH
    spec = pl.BlockSpec((seq, w), lambda b: (b, 0))
    return pl.pallas_call(
        _ctx_attn_kernel,
        grid=(nseq,),
        in_specs=[spec, spec, spec],
        out_specs=spec,
        out_shape=jax.ShapeDtypeStruct((nseq * seq, w), BF16),
        compiler_params=_cparams(("parallel",)),
        name="ctx_attention",
    )(q, k, v)


def _na_kernel(q_ref, k_ref, v_ref, kc_ref, vc_ref, bias_ref, o_ref, kb_scr, vb_scr, *, rows):
    scale = NA_HEAD_DIM ** -0.5
    band = NA_WIN_ROWS * GRID_W
    kb_scr[...] = k_ref[...].astype(BF16)
    vb_scr[...] = v_ref[...].astype(BF16)
    kc = kc_ref[...].astype(BF16)
    vc = vc_ref[...].astype(BF16)

    def rows_step(it, carry):
        r0 = it * NA_ROW_GROUP
        q0 = pl.multiple_of(r0 * GRID_W, NA_ROW_GROUP * GRID_W)
        q = q_ref[pl.ds(q0, NA_ROW_GROUP * GRID_W), :].astype(BF16)
        k0s = []
        s_rows = []
        for j in range(NA_ROW_GROUP):
            r = r0 + j
            rs = jnp.clip(r - NA_WIN_ROWS // 2, 0, rows - NA_WIN_ROWS)
            k0 = pl.multiple_of(rs * GRID_W, GRID_W)
            k0s.append(k0)
            s_rows.append(_nt_dot(q[j * GRID_W:(j + 1) * GRID_W], kb_scr[pl.ds(k0, band), :]) * scale
                          + bias_ref[r - rs])
        s_win = jnp.concatenate(s_rows, axis=0)
        s_ctx = _nt_dot(q, kc) * scale
        m = jnp.maximum(jnp.max(s_win, axis=-1, keepdims=True), jnp.max(s_ctx, axis=-1, keepdims=True))
        p_win = jnp.exp(s_win - m)
        p_ctx = jnp.exp(s_ctx - m)
        den = jnp.sum(p_win, axis=-1, keepdims=True) + jnp.sum(p_ctx, axis=-1, keepdims=True)
        p_win = (p_win / den).astype(BF16)
        o_ctx = jnp.dot((p_ctx / den).astype(BF16), vc, preferred_element_type=F32)
        for j in range(NA_ROW_GROUP):
            o = jnp.dot(p_win[j * GRID_W:(j + 1) * GRID_W], vb_scr[pl.ds(k0s[j], band), :],
                        preferred_element_type=F32) + o_ctx[j * GRID_W:(j + 1) * GRID_W]
            o_ref[pl.ds(pl.multiple_of((r0 + j) * GRID_W, GRID_W), GRID_W), :] = o.astype(o_ref.dtype)
        return carry

    lax.fori_loop(0, rows // NA_ROW_GROUP, rows_step, 0)


def _na_bias_table(rel_bias):
    col = np.arange(GRID_W)
    cstart = np.clip(col - NA_WIN_COLS // 2, 0, GRID_W - NA_WIN_COLS)
    valid = (col[None, :] >= cstart[:, None]) & (col[None, :] < cstart[:, None] + NA_WIN_COLS)
    dcol = col[None, :] - col[:, None] + NA_WIN_COLS - 1
    onehot = ((np.arange(2 * NA_WIN_COLS - 1)[:, None, None] == dcol[None]) & valid[None]).astype(np.float32)
    tabc = jnp.einsum("hrk,kcd->hrcd", rel_bias, jnp.asarray(onehot), precision=lax.Precision.HIGHEST)
    tabc = tabc + jnp.asarray(np.where(valid, 0.0, NEG_BIG).astype(np.float32))
    wr = NA_WIN_ROWS
    tab = jnp.stack([tabc[:, wr - 1 - o:2 * wr - 1 - o].transpose(0, 2, 1, 3) for o in range(wr)], axis=1)
    return tab.reshape(rel_bias.shape[0], wr, GRID_W, wr * GRID_W).astype(F32)


def neighbourhood_attention(qkv, cache_k, cache_v, layer, bias_tab, nseq, seq):
    hd = NA_HEAD_DIM
    past = cache_k.shape[2]
    rows = seq // GRID_W
    assert seq % GRID_W == 0 and rows >= NA_WIN_ROWS and rows % NA_ROW_GROUP == 0, (seq, rows)
    band = NA_WIN_ROWS * GRID_W
    return pl.pallas_call(
        functools.partial(_na_kernel, rows=rows),
        grid=(nseq, NA_HEADS),
        in_specs=[pl.BlockSpec((seq, hd), lambda b, h: (b, h)),
                  pl.BlockSpec((seq, hd), lambda b, h: (b, NA_HEADS + h)),
                  pl.BlockSpec((seq, hd), lambda b, h: (b, 2 * NA_HEADS + h)),
                  pl.BlockSpec((None, None, past, hd), lambda b, h: (b, layer, 0, h)),
                  pl.BlockSpec((None, None, past, hd), lambda b, h: (b, layer, 0, h)),
                  pl.BlockSpec((None, NA_WIN_ROWS, GRID_W, band), lambda b, h: (h, 0, 0, 0))],
        out_specs=pl.BlockSpec((seq, hd), lambda b, h: (b, h)),
        out_shape=jax.ShapeDtypeStruct((nseq * seq, NA_WIDTH), BF16),
        scratch_shapes=[pltpu.VMEM((seq, hd), BF16), pltpu.VMEM((seq, hd), BF16)],
        compiler_params=_cparams(("parallel", "parallel")),
        name="nbr_attention",
    )(qkv, qkv, qkv, cache_k, cache_v, bias_tab)


def _conv_kernel(x_ref, w_ref, b_ref, o_ref, *, seq):
    x = x_ref[...]
    n = x.shape[0]
    pos = lax.rem(lax.broadcasted_iota(jnp.int32, x.shape, 0), seq)
    xm2 = jnp.where(pos >= 2, pltpu.roll(x, 2, 0), 0.0)
    xm1 = jnp.where(pos >= 1, pltpu.roll(x, 1, 0), 0.0)
    xp1 = jnp.where(pos < seq - 1, pltpu.roll(x, n - 1, 0), 0.0)
    w = w_ref[...]
    y = w[0:1] * xm2 + w[1:2] * xm1 + w[2:3] * x + w[3:4] * xp1 + b_ref[...]
    o_ref[...] = y * jax.nn.sigmoid(y)


def conv_silu(ssm, conv_w, conv_b, nseq, seq, rows=4096):
    tc = 256
    c0 = SSM_INNER // tc
    t = nseq * seq
    tr = min(max(rows, seq), t)
    out = pl.pallas_call(
        functools.partial(_conv_kernel, seq=seq),
        grid=(t // tr, SSM_CONV_DIM // tc),
        in_specs=[pl.BlockSpec((tr, tc), lambda s, j: (s, c0 + j)),
                  pl.BlockSpec((4, tc), lambda s, j: (0, j)),
                  pl.BlockSpec((1, tc), lambda s, j: (0, j))],
        out_specs=pl.BlockSpec((tr, tc), lambda s, j: (s, j)),
        out_shape=jax.ShapeDtypeStruct((t, SSM_CONV_DIM), F32),
        compiler_params=_cparams(("parallel", "parallel")),
        name="conv_silu",
    )(ssm, conv_w, conv_b)
    return out.reshape(nseq, seq, SSM_CONV_DIM)


def _select_columns(x, sel):
    hi = x.astype(BF16)
    r = x - hi.astype(F32)
    mid = r.astype(BF16)
    lo = (r - mid.astype(F32)).astype(BF16)
    return (jnp.dot(hi, sel, preferred_element_type=F32) + jnp.dot(mid, sel, preferred_element_type=F32)
            + jnp.dot(lo, sel, preferred_element_type=F32))


def _ssd_kernel(*refs, has_init, nchunk):
    if has_init:
        xc_ref, dt_ref, tri_ref, dtb_ref, alog_ref, sel_n_ref, sel_p_ref, init_ref, y_ref, fin_ref, st_scr = refs
    else:
        xc_ref, dt_ref, tri_ref, dtb_ref, alog_ref, sel_n_ref, sel_p_ref, y_ref, fin_ref, st_scr = refs
        init_ref = None
    c = pl.program_id(2)

    @pl.when(c == 0)
    def _():
        if has_init:
            st_scr[...] = init_ref[...]
        else:
            st_scr[...] = jnp.zeros_like(st_scr)

    p = SSM_HEAD_DIM
    n = SSM_STATE
    xc = xc_ref[...]
    dt = jax.nn.softplus(dt_ref[...] + dtb_ref[...])
    a = dt * (-jnp.exp(alog_ref[...]))
    tri = tri_ref[...]
    mask = tri > 0.5
    cum = jnp.dot(tri, a, preferred_element_type=F32, precision=lax.Precision.HIGHEST)
    cum_t = cum.T
    cum_x = _select_columns(cum, sel_n_ref[...])
    dt_x = _select_columns(dt, sel_p_ref[...])
    xd_all = xc[:, :SSM_INNER] * dt_x
    rep = SSM_HEADS // SSM_GROUPS
    gmat = []
    bmat = []
    cmat = []
    for g in range(SSM_GROUPS):
        b_g = xc[:, SSM_INNER + g * n:SSM_INNER + (g + 1) * n].astype(BF16)
        c_g = xc[:, SSM_INNER + SSM_BC + g * n:SSM_INNER + SSM_BC + (g + 1) * n].astype(BF16)
        bmat.append(b_g)
        cmat.append(c_g)
        gmat.append(_nt_dot(c_g, b_g))
    for h in range(SSM_HEADS):
        g = h // rep
        cum_h = cum_x[:, h * n:(h + 1) * n]
        decay = jnp.exp(jnp.where(mask, cum_h - cum_t[h:h + 1, :], -jnp.inf))
        xd = xd_all[:, h * p:(h + 1) * p]
        y_diag = jnp.dot((gmat[g] * decay).astype(BF16), xd.astype(BF16), preferred_element_type=F32)
        tot = jnp.min(cum_h, axis=0, keepdims=True)
        e_end = jnp.exp(tot[:, :p] - cum_h[:, :p])
        chunk_state = _tn_dot((xd * e_end).astype(BF16), bmat[g])
        prev = st_scr[h]
        y_off = _nt_dot(cmat[g], prev.astype(BF16)) * jnp.exp(cum_h[:, :p])
        y_ref[:, h * p:(h + 1) * p] = y_diag + y_off
        st_scr[h] = prev * jnp.exp(tot) + chunk_state

    @pl.when(c == nchunk - 1)
    def _():
        fin_ref[...] = st_scr[...]


def ssd_scan(xc, ssm, dt_bias, a_log, init, layer):
    nseq, seq, _ = xc.shape
    q = SSM_CHUNK
    nc = seq // q
    idx = np.arange(q)
    tri = jnp.asarray(np.stack([idx[None, :] <= idx[:, None], idx[None, :] >= idx[:, None]]).astype(np.float32))

    def cc(d, c):
        return c + d * (nc - 1 - 2 * c)

    in_specs = [pl.BlockSpec((None, q, SSM_CONV_DIM), lambda d, s, c: (s, cc(d, c), 0)),
                pl.BlockSpec((None, q, LANES), lambda d, s, c: (s, cc(d, c), DT_BLOCK0 + d)),
                pl.BlockSpec((None, q, q), lambda d, s, c: (d, 0, 0)),
                pl.BlockSpec((None, 1, LANES), lambda d, s, c: (d, 0, 0)),
                pl.BlockSpec((None, 1, LANES), lambda d, s, c: (d, 0, 0)),
                pl.BlockSpec((LANES, SSM_HEADS * SSM_STATE), lambda d, s, c: (0, 0)),
                pl.BlockSpec((LANES, SSM_INNER), lambda d, s, c: (0, 0))]
    head = np.arange(LANES)[:, None]
    sel_n = jnp.asarray(head == np.arange(SSM_HEADS * SSM_STATE)[None, :] // SSM_STATE, BF16)
    sel_p = jnp.asarray(head == np.arange(SSM_INNER)[None, :] // SSM_HEAD_DIM, BF16)
    args = [xc, ssm.reshape(nseq, seq, SSM_COLS), tri, dt_bias, a_log, sel_n, sel_p]
    if init is not None:
        in_specs.append(pl.BlockSpec((None, None, None, SSM_HEADS, SSM_HEAD_DIM, SSM_STATE),
                                     lambda d, s, c: (s, layer, d, 0, 0, 0)))
        args.append(init)
    return pl.pallas_call(
        functools.partial(_ssd_kernel, has_init=init is not None, nchunk=nc),
        grid=(2, nseq, nc),
        in_specs=in_specs,
        out_specs=[pl.BlockSpec((None, None, q, SSM_INNER), lambda d, s, c: (d, s, cc(d, c), 0)),
                   pl.BlockSpec((None, None, SSM_HEADS, SSM_HEAD_DIM, SSM_STATE), lambda d, s, c: (s, d, 0, 0, 0))],
        out_shape=[jax.ShapeDtypeStruct((2, nseq, seq, SSM_INNER), F32),
                   jax.ShapeDtypeStruct((nseq, 2, SSM_HEADS, SSM_HEAD_DIM, SSM_STATE), F32)],
        scratch_shapes=[pltpu.VMEM((SSM_HEADS, SSM_HEAD_DIM, SSM_STATE), F32)],
        compiler_params=_cparams(("parallel", "parallel", "arbitrary")),
        name="ssd_scan",
    )(*args)


def _ssm_finish_kernel(y_ref, xs_ref, z_ref, dsk_ref, g_ref, o_ref):
    z = z_ref[...]
    y = (y_ref[0] + y_ref[1] + dsk_ref[...] * xs_ref[...]) * (z * jax.nn.sigmoid(z))
    y = y * lax.rsqrt(jnp.mean(y * y, axis=-1, keepdims=True) + EPS) * g_ref[...]
    o_ref[...] = y.astype(o_ref.dtype)


def ssm_finish(y, xc, ssm, d_skip, g, tm=512):
    t = ssm.shape[0]
    w = SSM_INNER
    return pl.pallas_call(
        _ssm_finish_kernel,
        grid=(t // tm,),
        in_specs=[pl.BlockSpec((2, tm, w), lambda i: (0, i, 0)),
                  pl.BlockSpec((tm, w), lambda i: (i, 0)),
                  pl.BlockSpec((tm, w), lambda i: (i, 0)),
                  pl.BlockSpec((1, w), lambda i: (0, 0)),
                  pl.BlockSpec((1, w), lambda i: (0, 0))],
        out_specs=pl.BlockSpec((tm, w), lambda i: (i, 0)),
        out_shape=jax.ShapeDtypeStruct((t, w), BF16),
        compiler_params=_cparams(("parallel",)),
        name="ssm_finish",
    )(y, xc, ssm, d_skip, g)


def _gmlp_kernel(u_ref, v_ref, g_ref, w_ref, b_ref, o_ref):
    v = jax.nn.gelu(v_ref[...].astype(F32))
    v = (v * lax.rsqrt(jnp.mean(v * v, axis=-1, keepdims=True) + EPS) * g_ref[...]).astype(BF16)
    q = MLP_CHUNK
    gw = MLP_WIDTH // MLP_GROUPS
    for ch in range(v.shape[0] // q):
        for g in range(MLP_GROUPS):
            mixed = jnp.dot(w_ref[g], v[ch * q:(ch + 1) * q, g * gw:(g + 1) * gw],
                            preferred_element_type=F32) + b_ref[g]
            u = jax.nn.gelu(u_ref[ch * q:(ch + 1) * q, g * gw:(g + 1) * gw].astype(F32))
            o_ref[ch * q:(ch + 1) * q, g * gw:(g + 1) * gw] = (u * mixed).astype(o_ref.dtype)


def chunk_mlp(mlp, g, w_s, b_s, tm=512):
    t = mlp.shape[0]
    w = MLP_WIDTH
    return pl.pallas_call(
        _gmlp_kernel,
        grid=(t // tm,),
        in_specs=[pl.BlockSpec((tm, w), lambda i: (i, 0)),
                  pl.BlockSpec((tm, w), lambda i: (i, 1)),
                  pl.BlockSpec((1, w), lambda i: (0, 0)),
                  pl.BlockSpec(w_s.shape, lambda i: (0, 0, 0)),
                  pl.BlockSpec(b_s.shape, lambda i: (0, 0, 0))],
        out_specs=pl.BlockSpec((tm, w), lambda i: (i, 0)),
        out_shape=jax.ShapeDtypeStruct((t, w), BF16),
        compiler_params=_cparams(("parallel",)),
        name="chunk_mlp",
    )(mlp, mlp, g, w_s, b_s)


def _merge_kernel(a0_ref, a1_ref, a2_ref, w_ref, g0_ref, g1_ref, g2_ref, o_ref):
    acc = jax.nn.sigmoid(g0_ref[...].astype(F32)) * jnp.dot(a0_ref[...], w_ref[0], preferred_element_type=F32)
    acc += jax.nn.sigmoid(g1_ref[...].astype(F32)) * jnp.dot(a1_ref[...], w_ref[1], preferred_element_type=F32)
    acc += jax.nn.sigmoid(g2_ref[...].astype(F32)) * jnp.dot(a2_ref[...], w_ref[2], preferred_element_type=F32)
    o_ref[...] = acc.astype(o_ref.dtype)


def merge_branches(o_na, o_ssm, o_mlp, w_br, gates, tm=1024, tn=512):
    t, k = o_na.shape
    tm = min(tm, t)
    d = w_br.shape[2]
    nb = d // tn
    a_spec = pl.BlockSpec((tm, k), lambda i, j: (i, 0))
    return pl.pallas_call(
        _merge_kernel,
        grid=(t // tm, nb),
        in_specs=[a_spec, a_spec, a_spec,
                  pl.BlockSpec((N_BRANCH, k, tn), lambda i, j: (0, 0, j)),
                  pl.BlockSpec((tm, tn), lambda i, j: (i, j)),
                  pl.BlockSpec((tm, tn), lambda i, j: (i, nb + j)),
                  pl.BlockSpec((tm, tn), lambda i, j: (i, 2 * nb + j))],
        out_specs=pl.BlockSpec((tm, tn), lambda i, j: (i, j)),
        out_shape=jax.ShapeDtypeStruct((t, d), BF16),
        compiler_params=_cparams(("parallel", "parallel")),
        name="merge_branches",
    )(o_na, o_ssm, o_mlp, w_br, gates, gates, gates)


def _out_proj_kernel(a_ref, w_ref, x_ref, gt_ref, o_ref):
    o_ref[...] = x_ref[...] + gt_ref[...] * jnp.dot(a_ref[...], w_ref[...], preferred_element_type=F32)


def out_proj(merged, w_out, x, gt, seg_len, tm=2048, tn=512):
    t, k = merged.shape
    d = w_out.shape[1]
    tm = min(tm, seg_len)
    spt = seg_len // tm
    return pl.pallas_call(
        _out_proj_kernel,
        grid=(t // tm, d // tn),
        in_specs=[pl.BlockSpec((tm, k), lambda i, j: (i, 0)),
                  pl.BlockSpec((k, tn), lambda i, j: (0, j)),
                  pl.BlockSpec((tm, tn), lambda i, j: (i, j)),
                  pl.BlockSpec((None, 1, tn), lambda i, j: (i // spt, 0, j))],
        out_specs=pl.BlockSpec((tm, tn), lambda i, j: (i, j)),
        out_shape=jax.ShapeDtypeStruct((t, d), F32),
        compiler_params=_cparams(("parallel", "parallel")),
        name="out_proj",
    )(merged, w_out, x, gt)


def _peer_norm_scores_kernel(x_ref, g_ref, sc_ref, sh_ref, wq_ref, key_ref, h_ref, o_ref):
    x = x_ref[...]
    y = x * lax.rsqrt(jnp.mean(x * x, axis=-1, keepdims=True) + EPS) * g_ref[...]
    h = (y * (1.0 + sc_ref[...]) + sh_ref[...]).T.astype(BF16)
    h_ref[...] = h
    qv = jnp.dot(wq_ref[...], h, preferred_element_type=F32).astype(BF16)
    dq = PEER_QDIM // 2
    for hk in range(2 * PEER_HEADS):
        o_ref[hk * PEER_KEYS:(hk + 1) * PEER_KEYS, :] = jnp.dot(
            key_ref[hk], qv[hk * dq:(hk + 1) * dq, :], preferred_element_type=F32)


def peer_norm_scores(x, g, sc, sh, seg_len, wq_t, keys, tl=512):
    t, d = x.shape
    rows = 2 * PEER_HEADS * PEER_KEYS
    spt = seg_len // tl
    seg_spec = pl.BlockSpec((None, 1, d), lambda i: (i // spt, 0, 0))
    return pl.pallas_call(
        _peer_norm_scores_kernel,
        grid=(t // tl,),
        in_specs=[pl.BlockSpec((tl, d), lambda i: (i, 0)),
                  pl.BlockSpec((1, d), lambda i: (0, 0)), seg_spec, seg_spec,
                  pl.BlockSpec(wq_t.shape, lambda i: (0, 0)),
                  pl.BlockSpec(keys.shape, lambda i: (0, 0, 0))],
        out_specs=[pl.BlockSpec((d, tl), lambda i: (0, i)), pl.BlockSpec((rows, tl), lambda i: (0, i))],
        out_shape=[jax.ShapeDtypeStruct((d, t), BF16), jax.ShapeDtypeStruct((rows, t), F32)],
        compiler_params=_cparams(("parallel",)),
        name="peer_norm_scores",
    )(x, g, sc, sh, wq_t, keys)


def _top16(s, exact):
    nk = s.shape[0]
    kio = lax.broadcasted_iota(jnp.int32, s.shape, 0).astype(F32)
    rank = jnp.full(s.shape, float(PEER_TOPK), F32)
    vals = []
    for a in range(PEER_TOPK):
        m = jnp.max(s, axis=0, keepdims=True)
        hit = s == m
        if exact:
            hit = kio == jnp.min(jnp.where(hit, kio, float(nk)), axis=0, keepdims=True)
        rank = jnp.where(hit, float(a), rank)
        s = jnp.where(hit, -jnp.inf, s)
        vals.append(m)
    return jnp.concatenate(vals, axis=0), rank


def _peer_select_heads(s_ref, rank2_ref, cnt_ref, e1_ref, e2_ref, exact):
    nk = PEER_KEYS
    k = PEER_TOPK
    bad = jnp.zeros((1, s_ref.shape[1]), F32)
    for h in range(PEER_HEADS):
        s1 = s_ref[(2 * h) * nk:(2 * h + 1) * nk, :]
        s2 = s_ref[(2 * h + 1) * nk:(2 * h + 2) * nk, :]
        tv1, rank1 = _top16(s1, exact)
        tv2, rank2 = _top16(s2, exact)
        sub = lax.broadcasted_iota(jnp.int32, (8, s1.shape[1]), 0).astype(F32)
        pieces = []
        pos_pieces = []
        for a in range(k // 2):
            nb = k // (a + 1)
            for b0 in range(0, nb, 8):
                vals = tv1[a:a + 1, :] + tv2[b0:b0 + 8, :]
                pieces.append(vals if nb - b0 >= 8 else jnp.where(sub < float(nb - b0), vals, -jnp.inf))
                pos_pieces.append(sub + float(a * k + b0))
        pieces.append(tv1[k // 2:k, :] + tv2[0:1, :])
        pos_pieces.append((sub + float(k // 2)) * float(k))
        cand = jnp.concatenate(pieces, axis=0)
        pio = jnp.concatenate(pos_pieces, axis=0)
        cnt_cells = jnp.zeros(cand.shape, F32)
        zsum = jnp.zeros((1, cand.shape[1]), F32)
        top = tv1[0:1, :] + tv2[0:1, :]
        for _ in range(k):
            m = jnp.max(cand, axis=0, keepdims=True)
            hit = cand == m
            if exact:
                hit = pio == jnp.min(jnp.where(hit, pio, float(k * k)), axis=0, keepdims=True)
            cand = jnp.where(hit, -jnp.inf, cand)
            cnt_cells = jnp.where(hit, 1.0, cnt_cells)
            zsum = zsum + jnp.exp(m - top)
        cnt = jnp.zeros(s1.shape, F32)
        for a in range(k):
            if a < k // 2:
                lo = 0 if a == 0 else 8 * (a + 1)
                cnt_a = jnp.sum(cnt_cells[lo:8 * (a + 2), :], axis=0, keepdims=True)
            else:
                r = 8 * (k // 2 + 1) + a - k // 2
                cnt_a = cnt_cells[r:r + 1, :]
            cnt = jnp.where(rank1 == float(a), cnt_a, cnt)
        if not exact:
            n1 = jnp.sum(jnp.where(rank1 < float(k), 1.0, 0.0), axis=0, keepdims=True)
            n2 = jnp.sum(jnp.where(rank2 < float(k), 1.0, 0.0), axis=0, keepdims=True)
            nc = jnp.sum(cnt_cells, axis=0, keepdims=True)
            bad = bad + jnp.abs(n1 - float(k)) + jnp.abs(n2 - float(k)) + jnp.abs(nc - float(k))
        rank2_ref[h * nk:(h + 1) * nk, :] = rank2.astype(rank2_ref.dtype)
        cnt_ref[h * nk:(h + 1) * nk, :] = cnt
        e1_ref[h * nk:(h + 1) * nk, :] = jnp.exp(s1 - tv1[0:1, :])
        e2_ref[h * nk:(h + 1) * nk, :] = (jnp.exp(s2 - tv2[0:1, :]) / zsum).astype(e2_ref.dtype)
    return bad


def _peer_select_kernel(s_ref, rank2_ref, cnt_ref, e1_ref, e2_ref):
    bad = _peer_select_heads(s_ref, rank2_ref, cnt_ref, e1_ref, e2_ref, exact=False)

    @pl.when(jnp.max(bad) > 0.0)
    def _():
        _peer_select_heads(s_ref, rank2_ref, cnt_ref, e1_ref, e2_ref, exact=True)


def peer_select(scores, tl=128):
    rows, t = scores.shape
    half = rows // 2
    spec = pl.BlockSpec((half, tl), lambda i: (0, i))
    shp = jax.ShapeDtypeStruct((half, t), F32)
    shp_b = jax.ShapeDtypeStruct((half, t), BF16)
    return pl.pallas_call(
        _peer_select_kernel,
        grid=(t // tl,),
        in_specs=[pl.BlockSpec((rows, tl), lambda i: (0, i))],
        out_specs=[spec, spec, spec, spec],
        out_shape=[shp_b, shp, shp, shp_b],
        compiler_params=_cparams(("parallel",)),
        name="peer_select",
    )(scores)


def _peer_dense_kernel(h_ref, u_ref, vta_ref, vtb_ref, rank2_ref, cnt_ref, e1_ref, e2_ref, o_ref, za_scr, zb_scr,
                       *, nstep, rows_per_half, lane_tile):
    e = pl.program_id(1)

    @pl.when(e == 0)
    def _():
        o_ref[...] = jnp.zeros_like(o_ref)
        zb_scr[...] = jnp.zeros_like(zb_scr)

    nk = PEER_KEYS
    eh = rows_per_half * nk
    nsub = rows_per_half // SUB_KEYS
    mrows = o_ref.shape[0] // nsub
    lanes = [slice(lt * lane_tile, (lt + 1) * lane_tile) for lt in range(o_ref.shape[1] // lane_tile)]
    for half, (z_new, z_old, vt_ref) in enumerate(((za_scr, zb_scr, vta_ref), (zb_scr, za_scr, vtb_ref))):
        for sc in range(nsub):
            i0 = half * rows_per_half + sc * SUB_KEYS
            acts = [jax.nn.gelu(jnp.dot(u_ref[i0 * nk:(i0 + SUB_KEYS) * nk, :], h_ref[:, ls],
                                        preferred_element_type=F32)) for ls in lanes]
            for ls in lanes:
                o_ref[sc * mrows:(sc + 1) * mrows, ls] += jnp.dot(
                    vt_ref[sc * mrows:(sc + 1) * mrows, :eh], z_old[:, ls], preferred_element_type=F32)
            for ls, act in zip(lanes, acts):
                for k in range(SUB_KEYS):
                    w = None
                    for h in range(PEER_HEADS):
                        row = h * nk + e * (2 * rows_per_half) + i0 + k
                        cnt_b = jnp.broadcast_to(cnt_ref[pl.ds(row, 1), ls], (BF16_ROWS, lane_tile))
                        e1_b = jnp.broadcast_to(e1_ref[pl.ds(row, 1), ls], (BF16_ROWS, lane_tile))
                        cnt_b = jnp.concatenate([cnt_b.astype(BF16)] * (nk // BF16_ROWS), axis=0)
                        e1_b = jnp.concatenate([e1_b.astype(BF16)] * (nk // BF16_ROWS), axis=0)
                        sel = rank2_ref[h * nk:(h + 1) * nk, ls] < cnt_b
                        term = jnp.where(sel, e2_ref[h * nk:(h + 1) * nk, ls], jnp.zeros((), BF16)) * e1_b
                        w = term if w is None else w + term
                    r0 = (sc * SUB_KEYS + k) * nk
                    z_new[r0:r0 + nk, ls] = w * act[k * nk:(k + 1) * nk, :].astype(BF16)

    @pl.when(e == nstep - 1)
    def _():
        for ls in lanes:
            o_ref[:, ls] += jnp.dot(vtb_ref[:, eh:], zb_scr[:, ls], preferred_element_type=F32)


def peer_dense(h_t, u_tab, v_t, rank2, cnt, e1, e2, tm=512, et=1024, lane_tile=256):
    d, t = h_t.shape
    ne = u_tab.shape[0]
    assert ne == PEER_EXPERTS and ne % et == 0 and t % tm == 0 and tm % lane_tile == 0, (ne, et, t, tm)
    assert (et // 2) % (SUB_KEYS * PEER_KEYS) == 0 and d % (et // 2 // (SUB_KEYS * PEER_KEYS)) == 0
    nstep = ne // et
    eh = et // 2
    map_spec = pl.BlockSpec((rank2.shape[0], tm), lambda i, e: (0, i))
    return pl.pallas_call(
        functools.partial(_peer_dense_kernel, nstep=nstep, rows_per_half=eh // PEER_KEYS, lane_tile=lane_tile),
        grid=(t // tm, nstep),
        in_specs=[pl.BlockSpec((d, tm), lambda i, e: (0, i)),
                  pl.BlockSpec((et, d), lambda i, e: (e, 0)),
                  pl.BlockSpec((d, eh), lambda i, e: (0, jnp.maximum(2 * e - 1, 0))),
                  pl.BlockSpec((d, et), lambda i, e: (0, e)),
                  map_spec, map_spec, map_spec, map_spec],
        out_specs=pl.BlockSpec((d, tm), lambda i, e: (0, i)),
        out_shape=jax.ShapeDtypeStruct((d, t), F32),
        scratch_shapes=[pltpu.VMEM((eh, tm), BF16), pltpu.VMEM((eh, tm), BF16)],
        compiler_params=_cparams(("parallel", "arbitrary")),
        name="peer_dense",
    )(h_t, u_tab, v_t, v_t, rank2, cnt, e1, e2)


def _peer_resid_norm_kernel(x_ref, p_ref, gt_ref, g_ref, sc_ref, sh_ref, x_out_ref, h_ref):
    x = x_ref[...] + gt_ref[...] * p_ref[...].T
    x_out_ref[...] = x
    y = x * lax.rsqrt(jnp.mean(x * x, axis=-1, keepdims=True) + EPS) * g_ref[...]
    h_ref[...] = (y * (1.0 + sc_ref[...]) + sh_ref[...]).astype(h_ref.dtype)


def _peer_resid_final_kernel(x_ref, p_ref, gt_ref, g_ref, o_ref):
    x = x_ref[...] + gt_ref[...] * p_ref[...].T
    o_ref[...] = x * lax.rsqrt(jnp.mean(x * x, axis=-1, keepdims=True) + EPS) * g_ref[...]


def peer_residual_norm(x, peer_t, gt, g, sc, sh, seg_len, tm=512):
    t, d = x.shape
    spt = seg_len // tm
    row_spec = pl.BlockSpec((tm, d), lambda i: (i, 0))
    seg_spec = pl.BlockSpec((None, 1, d), lambda i: (i // spt, 0, 0))
    in_specs = [row_spec, pl.BlockSpec((d, tm), lambda i: (0, i)), seg_spec, pl.BlockSpec((1, d), lambda i: (0, 0))]
    if sc is None:
        return pl.pallas_call(
            _peer_resid_final_kernel,
            grid=(t // tm,),
            in_specs=in_specs,
            out_specs=row_spec,
            out_shape=jax.ShapeDtypeStruct((t, d), F32),
            compiler_params=_cparams(("parallel",)),
            name="peer_resid_final",
        )(x, peer_t, gt, g)
    return pl.pallas_call(
        _peer_resid_norm_kernel,
        grid=(t // tm,),
        in_specs=in_specs + [seg_spec, seg_spec],
        out_specs=[row_spec, row_spec],
        out_shape=[jax.ShapeDtypeStruct((t, d), F32), jax.ShapeDtypeStruct((t, d), BF16)],
        compiler_params=_cparams(("parallel",)),
        name="peer_resid_norm",
    )(x, peer_t, gt, g, sc, sh)


def _layer(x, h, mod, lw, layer, nseq, seq, ctx):
    sh1, sc1, gt1, sh2, sc2, gt2 = mod
    seg_len = x.shape[0] // sh1.shape[0]
    ssm = matmul(h, lw["w_ssm"], 1024, 128 * 11, name="proj_ssm")
    mlp = matmul(h, lw["w_mlp"], 1024, 1024, out_dtype=BF16, name="proj_mlp")
    gates = matmul(h, lw["w_gate"], 1024, 1024, out_dtype=BF16, name="proj_gate")
    if ctx[0] == "context":
        q = matmul(h, lw["w_qkv"], 1024, 1024, out_dtype=BF16, name="proj_q", col0=0, n=NA_WIDTH)
        k, kbuf = matmul_into_layer(h, lw["w_qkv"], NA_WIDTH, ctx[1], layer, 1024, "proj_k")
        v, vbuf = matmul_into_layer(h, lw["w_qkv"], 2 * NA_WIDTH, ctx[2], layer, 1024, "proj_v")
        o_na = context_attention(q, k, v, nseq, seq)
        init = None
    else:
        _, cache_k, cache_v, init = ctx
        qkv = matmul(h, lw["w_qkv"], 1024, 1024, out_dtype=BF16, name="proj_qkv")
        o_na = neighbourhood_attention(qkv, cache_k, cache_v, layer, lw["na_bias"], nseq, seq)
    xc = conv_silu(ssm, lw["conv_w"], lw["conv_b"], nseq, seq)
    y, fin = ssd_scan(xc, ssm, lw["dt_bias"], lw["a_log"], init, layer)
    t = nseq * seq
    o_ssm = ssm_finish(y.reshape(2, t, SSM_INNER), xc.reshape(t, SSM_CONV_DIM), ssm, lw["d_skip"], lw["ssm_norm_g"])
    o_mlp = chunk_mlp(mlp, lw["mlp_norm_g"], lw["mlp_w_s"], lw["mlp_b_s"])
    merged = merge_branches(o_na, o_ssm, o_mlp, lw["w_br"], gates)
    x = out_proj(merged, lw["w_out"], x, gt1, seg_len)
    h2t, scores = peer_norm_scores(x, lw["g_ffn"], sc2, sh2, seg_len, lw["peer_wq_t"], lw["peer_keys"])
    rank2, cnt, e1, e2 = peer_select(scores)
    peer_t = peer_dense(h2t, lw["peer_u"], lw["peer_v_t"], rank2, cnt, e1, e2)
    extras = (kbuf, vbuf, fin) if ctx[0] == "context" else ()
    return x, peer_t, extras


def _pad_lanes(v):
    pad = LANES - v.shape[-1]
    return jnp.pad(v, [(0, 0)] * (v.ndim - 1) + [(0, pad)])[..., None, :]


def kernel(x_prompt, x_sample, cache_k, cache_v, state_ssm, c, c_ctx, w_mod, b_mod, g_norm_mix, g_norm_ffn, g_norm_final, w_in, na_rel_bias, ssm_conv_w, ssm_conv_b, ssm_dt_bias, ssm_a_log, ssm_d, ssm_norm_g, mlp_norm_g, mlp_w_s, mlp_b_s, w_br_na, w_br_ssm, w_br_mlp, w_out, peer_w_q, peer_sub_keys, peer_u, peer_v):
    batch, seq, d = x_prompt.shape
    dec_batch, dec_seq, _ = x_sample.shape
    depth = w_in.shape[0]
    past = cache_k.shape[2]

    cvec = jnp.concatenate([c_ctx[None, :], c, jnp.zeros((8 - 1 - dec_batch, d), F32)], axis=0)
    mod_all = modulation(cvec, w_mod, b_mod)

    o_q = 0
    o_z = 3 * NA_WIDTH
    o_xbc = o_z + SSM_INNER
    o_dt = o_xbc + SSM_CONV_DIM
    o_u = o_dt + 2 * SSM_HEADS
    o_g = o_u + 2 * MLP_WIDTH
    dt_pad = jnp.zeros((depth, d, LANES - SSM_HEADS), F32)
    w_ssm = jnp.concatenate([w_in[:, :, o_z:o_dt], w_in[:, :, o_dt:o_dt + SSM_HEADS], dt_pad,
                             w_in[:, :, o_dt + SSM_HEADS:o_u], dt_pad], axis=-1).astype(BF16)
    w_qkv = w_in[:, :, o_q:o_z].astype(BF16)
    w_mlp = w_in[:, :, o_u:o_g].astype(BF16)
    w_gate = w_in[:, :, o_g:].astype(BF16)
    w_br = jnp.stack([w_br_na, w_br_ssm, w_br_mlp], axis=1).astype(BF16)
    w_out_b = w_out.astype(BF16)
    peer_wq_t = jnp.swapaxes(peer_w_q, 1, 2).astype(BF16)
    peer_keys = peer_sub_keys.reshape(depth, 2 * PEER_HEADS, PEER_KEYS, PEER_QDIM // 2).astype(BF16)
    peer_u_b = peer_u.astype(BF16)
    peer_v_t = jnp.swapaxes(peer_v, 1, 2).astype(BF16)
    dt_bias = _pad_lanes(ssm_dt_bias)
    a_log = _pad_lanes(ssm_a_log)
    d_skip = jnp.repeat(ssm_d, SSM_HEAD_DIM, axis=-1)[:, None, :]
    mlp_b = jnp.broadcast_to(mlp_b_s[..., None], mlp_b_s.shape + (MLP_CHUNK,))
    mlp_w = mlp_w_s.astype(BF16)

    cache_k4 = cache_k.reshape(dec_batch, depth, past, NA_WIDTH)
    cache_v4 = cache_v.reshape(dec_batch, depth, past, NA_WIDTH)

    xp = x_prompt.reshape(batch * seq, d)
    xs = x_sample.reshape(dec_batch * dec_seq, d)
    mods = []
    for l in range(depth):
        m = mod_all[l].reshape(8, 6, d)
        mods.append(([m[0:1, j][:, None, :] for j in range(6)],
                     [m[1:1 + dec_batch, j][:, None, :] for j in range(6)]))
    kbuf = jnp.zeros((batch, depth, seq, NA_HEADS, NA_HEAD_DIM), F32)
    vbuf = jnp.zeros((batch, depth, seq, NA_HEADS, NA_HEAD_DIM), F32)
    new_s = []
    for l in range(depth):
        lw = {
            "g_mix": g_norm_mix[l][None, :], "g_ffn": g_norm_ffn[l][None, :],
            "w_qkv": w_qkv[l], "w_ssm": w_ssm[l], "w_mlp": w_mlp[l], "w_gate": w_gate[l],
            "na_bias": _na_bias_table(na_rel_bias[l]),
            "conv_w": ssm_conv_w[l], "conv_b": ssm_conv_b[l][None, :],
            "dt_bias": dt_bias[l], "a_log": a_log[l], "d_skip": d_skip[l],
            "ssm_norm_g": ssm_norm_g[l][None, :], "mlp_norm_g": mlp_norm_g[l][None, :],
            "mlp_w_s": mlp_w[l], "mlp_b_s": mlp_b[l],
            "w_br": w_br[l], "w_out": w_out_b[l],
            "peer_wq_t": peer_wq_t[l], "peer_keys": peer_keys[l], "peer_u": peer_u_b[l], "peer_v_t": peer_v_t[l],
        }
        mod_ctx, mod_lat = mods[l]
        if l == 0:
            hp = norm_mod(xp, lw["g_mix"], mod_ctx[1], mod_ctx[0], batch * seq)
            hs = norm_mod(xs, lw["g_mix"], mod_lat[1], mod_lat[0], dec_seq)
        xp, peer_p, (kbuf, vbuf, fin) = _layer(xp, hp, mod_ctx, lw, l, batch, seq, ("context", kbuf, vbuf))
        xs, peer_s, _ = _layer(xs, hs, mod_lat, lw, l, dec_batch, dec_seq,
                               ("latent", cache_k4, cache_v4, state_ssm))
        if l + 1 < depth:
            nxt_ctx, nxt_lat = mods[l + 1]
            g_next = g_norm_mix[l + 1][None, :]
            xp, hp = peer_residual_norm(xp, peer_p, mod_ctx[5], g_next, nxt_ctx[1], nxt_ctx[0], batch * seq)
            xs, hs = peer_residual_norm(xs, peer_s, mod_lat[5], g_next, nxt_lat[1], nxt_lat[0], dec_seq)
        else:
            gf = g_norm_final[None, :]
            y_prompt = peer_residual_norm(xp, peer_p, mod_ctx[5], gf, None, None, batch * seq)
            y_sample = peer_residual_norm(xs, peer_s, mod_lat[5], gf, None, None, dec_seq)
        new_s.append(fin)
    return (y_prompt.reshape(batch, seq, d), y_sample.reshape(dec_batch, dec_seq, d),
            kbuf, vbuf, jnp.stack(new_s, axis=1))
```
